```python
import math
import jax, jax.numpy as jnp
from jax import lax
import numpy as np


D_MODEL = 2048
BATCH = 4
SEQ = 4096
DEPTH = 2

GRID_W = 64
CTX_LEN = 256
BLOCK = 128
WINDOW = 128
HEAD_DIM = 128
ATTN_HQ = 8
ATTN_HKV = 2
ATTN_GROUP = ATTN_HQ // ATTN_HKV
ATTN_WIDTH = ATTN_HQ * HEAD_DIM
SSD_HEADS = 16
SSD_P = 64
SSD_INNER = SSD_HEADS * SSD_P
SSD_GROUPS = 2
SSD_STATE = 128
SSD_CONV = 5
SSD_CONV_CH = SSD_INNER + 2 * SSD_GROUPS * SSD_STATE
CHUNK = 128
RET_HEADS = 8
RET_DK = HEAD_DIM
RET_DV = HEAD_DIM
RET_WIDTH = RET_HEADS * RET_DV
N_BRANCH = 3
D_FF = 4 * D_MODEL
ROPE_BASE = 10000.0
DEEPNORM_ALPHA = (2 * DEPTH) ** 0.25
DEEPNORM_BETA = (8 * DEPTH) ** -0.25
LN_EPS = 1e-6
NEG_INF = -1e30
SPLIT_SIZES = (ATTN_WIDTH, ATTN_HKV * HEAD_DIM, ATTN_HKV * HEAD_DIM, SSD_INNER, SSD_CONV_CH, 2 * SSD_HEADS,
               RET_HEADS * RET_DK, RET_HEADS * RET_DK, RET_WIDTH, RET_WIDTH, N_BRANCH * D_MODEL)
IN_COLS = sum(SPLIT_SIZES)

kernel_name = 'hybrid_gated_attn_ssd_retention_block'


def layer_norm(x):
    xf = x.astype(jnp.float32)
    mu = jnp.mean(xf, axis=-1, keepdims=True)
    var = jnp.mean(jnp.square(xf - mu), axis=-1, keepdims=True)
    return ((xf - mu) * lax.rsqrt(var + LN_EPS)).astype(x.dtype)


def rms_norm(x, g):
    xf = x.astype(jnp.float32)
    y = xf * lax.rsqrt(jnp.mean(jnp.square(xf), axis=-1, keepdims=True) + LN_EPS)
    return y.astype(x.dtype) * g


def ada_params(cond, ada_w, ada_b):
    mod = jax.nn.silu(cond) @ ada_w + ada_b
    return jnp.split(mod[..., None, :], 6, axis=-1)


def modulate(x, shift, scale):
    return layer_norm(x) * (1.0 + scale) + shift


def axial_rope_table(n_tokens):
    rows = n_tokens // GRID_W
    row = jnp.repeat(jnp.arange(rows), GRID_W).astype(jnp.float32)
    col = (jnp.arange(rows * GRID_W) % GRID_W).astype(jnp.float32)
    n_freq = HEAD_DIM // 4
    inv = ROPE_BASE ** (-jnp.arange(n_freq, dtype=jnp.float32) / n_freq)
    ang = jnp.concatenate([row[:, None] * inv, col[:, None] * inv], axis=-1)
    return jnp.cos(ang), jnp.sin(ang)


def apply_rope(x, cos, sin):
    half = x.shape[-1] // 2
    x1, x2 = x[..., :half], x[..., half:]
    c = cos[None, :, None, :]
    s = sin[None, :, None, :]
    return jnp.concatenate([x1 * c - x2 * s, x2 * c + x1 * s], axis=-1).astype(x.dtype)


def centred_conv(x, w, b):
    y = lax.conv_general_dilated(x, w.astype(x.dtype)[:, None, :], window_strides=(1,),
                                 padding=[(SSD_CONV // 2, SSD_CONV // 2)],
                                 dimension_numbers=('NWC', 'WIO', 'NWC'),
                                 feature_group_count=x.shape[-1])
    return y + b


def mixer_inputs(h, w_in, conv_w, conv_b):
    bsz, n = h.shape[:2]
    parts = jnp.split(h @ w_in, np.cumsum(SPLIT_SIZES)[:-1].tolist(), axis=-1)
    aq, ak, av, z, xbc, dt_raw, rq, rk, rv, rg, gates = parts
    aq = aq.reshape(bsz, n, ATTN_HQ, HEAD_DIM)
    ak = ak.reshape(bsz, n, ATTN_HKV, HEAD_DIM)
    av = av.reshape(bsz, n, ATTN_HKV, HEAD_DIM)
    xbc = jax.nn.silu(centred_conv(xbc, conv_w, conv_b))
    xs, bm, cm = jnp.split(xbc, [SSD_INNER, SSD_INNER + SSD_GROUPS * SSD_STATE], axis=-1)
    xs = xs.reshape(bsz, n, SSD_HEADS, SSD_P)
    bm = bm.reshape(bsz, n, SSD_GROUPS, SSD_STATE)
    cm = cm.reshape(bsz, n, SSD_GROUPS, SSD_STATE)
    rq = rq.reshape(bsz, n, RET_HEADS, RET_DK)
    rk = rk.reshape(bsz, n, RET_HEADS, RET_DK)
    rv = rv.reshape(bsz, n, RET_HEADS, RET_DV)
    return (aq, ak, av, z, xs, bm, cm, dt_raw, rq, rk, rv, rg, gates)


def windowed_attention(q, k, v, k_ctx, v_ctx, sink):
    bsz, n = q.shape[:2]
    t_len = k_ctx.shape[1]
    nb = n // BLOCK
    scale = HEAD_DIM ** -0.5
    qb = q.reshape(bsz, nb, BLOCK, ATTN_HKV, ATTN_GROUP, HEAD_DIM)

    def band(t):
        tb = t.reshape(bsz, nb, BLOCK, ATTN_HKV, HEAD_DIM)
        tp = jnp.pad(tb, ((0, 0), (1, 1), (0, 0), (0, 0), (0, 0)))
        return jnp.concatenate([tp[:, :-2], tp[:, 1:-1], tp[:, 2:]], axis=2)

    kb, vb = band(k), band(v)
    s_loc = jnp.einsum('bnqhgd,bnkhd->bnhgqk', qb, kb).astype(jnp.float32) * scale
    qi = jnp.arange(BLOCK)[:, None] + BLOCK
    kj = jnp.arange(3 * BLOCK)[None, :]
    key_pos = jnp.arange(nb)[:, None] * BLOCK - BLOCK + jnp.arange(3 * BLOCK)[None, :]
    mask = (jnp.abs(qi - kj) <= WINDOW)[None] & ((key_pos >= 0) & (key_pos < n))[:, None, :]
    s_loc = jnp.where(mask[None, :, None, None], s_loc, NEG_INF)
    s_ctx = jnp.einsum('bnqhgd,bthd->bnhgqt', qb, k_ctx).astype(jnp.float32) * scale
    s_sink = jnp.broadcast_to(sink.astype(jnp.float32).reshape(1, 1, ATTN_HKV, ATTN_GROUP, 1, 1),
                              (bsz, nb, ATTN_HKV, ATTN_GROUP, BLOCK, 1))
    p = jax.nn.softmax(jnp.concatenate([s_loc, s_ctx, s_sink], axis=-1), axis=-1).astype(v.dtype)
    p_loc = p[..., :3 * BLOCK]
    p_ctx = p[..., 3 * BLOCK:3 * BLOCK + t_len]
    o = (jnp.einsum('bnhgqk,bnkhd->bnqhgd', p_loc, vb)
         + jnp.einsum('bnhgqt,bthd->bnqhgd', p_ctx, v_ctx))
    return o.reshape(bsz, n, ATTN_WIDTH)


def context_attention(q, k, v, sink):
    bsz, t_len = q.shape[:2]
    qg = q.reshape(bsz, t_len, ATTN_HKV, ATTN_GROUP, HEAD_DIM)
    s = jnp.einsum('bqhgd,bkhd->bhgqk', qg, k).astype(jnp.float32) * (HEAD_DIM ** -0.5)
    s_sink = jnp.broadcast_to(sink.astype(jnp.float32).reshape(1, ATTN_HKV, ATTN_GROUP, 1, 1),
                              (bsz, ATTN_HKV, ATTN_GROUP, t_len, 1))
    p = jax.nn.softmax(jnp.concatenate([s, s_sink], axis=-1), axis=-1)[..., :t_len].astype(v.dtype)
    o = jnp.einsum('bhgqk,bkhd->bqhgd', p, v)
    return o.reshape(bsz, t_len, ATTN_WIDTH)


def ssd_chunked(xs, dt, a, bm, cm, h0):
    bsz, n = xs.shape[:2]
    nc = n // CHUNK
    hg = SSD_HEADS // SSD_GROUPS
    x = xs.reshape(bsz, nc, CHUNK, SSD_GROUPS, hg, SSD_P)
    dtc = dt.reshape(bsz, nc, CHUNK, SSD_GROUPS, hg)
    bc = bm.reshape(bsz, nc, CHUNK, SSD_GROUPS, SSD_STATE)
    cc = cm.reshape(bsz, nc, CHUNK, SSD_GROUPS, SSD_STATE)
    a_cs = jnp.cumsum(dtc * a.reshape(SSD_GROUPS, hg), axis=2)
    causal = jnp.tril(jnp.ones((CHUNK, CHUNK), bool))[:, :, None, None]
    seg = a_cs[:, :, :, None] - a_cs[:, :, None, :]
    decay = jnp.exp(jnp.where(causal, seg, NEG_INF))
    cb = jnp.einsum('bcign,bcjgn->bcijg', cc, bc)
    y_diag = jnp.einsum('bcijg,bcijgh,bcjgh,bcjghp->bcighp', cb, decay, dtc, x)
    decay_to_end = jnp.exp(a_cs[:, :, -1:] - a_cs)
    states = jnp.einsum('bcjgn,bcjgh,bcjghp->bcghpn', bc, decay_to_end * dtc, x).astype(jnp.float32)
    chunk_decay = jnp.exp(a_cs[:, :, -1])

    def step(h, inp):
        dec, st = inp
        return dec[..., None, None] * h + st, h

    h_fin, h_prev = lax.scan(step, h0.reshape(bsz, SSD_GROUPS, hg, SSD_P, SSD_STATE),
                             (jnp.moveaxis(chunk_decay, 1, 0), jnp.moveaxis(states, 1, 0)))
    h_prev = jnp.moveaxis(h_prev, 0, 1)
    y_off = jnp.einsum('bcign,bcghpn,bcigh->bcighp', cc, h_prev, jnp.exp(a_cs))
    y = (y_diag + y_off).reshape(bsz, n, SSD_HEADS, SSD_P)
    return y, h_fin.reshape(bsz, SSD_HEADS, SSD_P, SSD_STATE)


def flip_seq(t):
    return jnp.flip(t, axis=1)


def ssd_bidirectional(xs, dt_raw, bm, cm, a_log, dt_bias, init_f, init_b):
    a = -jnp.exp(a_log.astype(jnp.float32))
    dt = jax.nn.softplus(dt_raw.astype(jnp.float32).reshape(dt_raw.shape[0], dt_raw.shape[1], 2, SSD_HEADS)
                         + dt_bias.astype(jnp.float32))
    y_f, h_f = ssd_chunked(xs, dt[:, :, 0], a[0], bm, cm, init_f)
    y_b, h_b = ssd_chunked(flip_seq(xs), flip_seq(dt[:, :, 1]), a[1], flip_seq(bm), flip_seq(cm), init_b)
    return y_f + flip_seq(y_b), h_f, h_b


def ssd_output(ys, xs, z, d_skip, norm_g):
    bsz, n = xs.shape[:2]
    y = (ys + d_skip[:, None] * xs).reshape(bsz, n, SSD_INNER)
    return rms_norm(y * jax.nn.silu(z), norm_g)


def retention_chunked(q, k, v, log_gamma, s0):
    bsz, n = q.shape[:2]
    nc = n // CHUNK
    qc = q.reshape(bsz, nc, CHUNK, RET_HEADS, RET_DK)
    kc = k.reshape(bsz, nc, CHUNK, RET_HEADS, RET_DK)
    vc = v.reshape(bsz, nc, CHUNK, RET_HEADS, RET_DV)
    lg = log_gamma.astype(jnp.float32)[:, None]
    pos = jnp.arange(CHUNK, dtype=jnp.float32)
    diff = pos[:, None] - pos[None, :]
    decay_mask = jnp.where(diff >= 0, jnp.exp(jnp.maximum(diff, 0.0) * lg[:, :, None]), 0.0)
    scores = jnp.einsum('bcihd,bcjhd->bchij', qc, kc) * decay_mask
    intra = jnp.einsum('bchij,bcjhe->bcihe', scores, vc)
    kv = jnp.einsum('bcjhd,bcjhe,hj->bchde', kc, vc, jnp.exp((CHUNK - 1 - pos) * lg)).astype(jnp.float32)
    chunk_decay = jnp.exp(CHUNK * lg)[:, :, None]

    def step(s, kv_c):
        return chunk_decay * s + kv_c, s

    s_fin, s_prev = lax.scan(step, s0, jnp.moveaxis(kv, 1, 0))
    s_prev = jnp.moveaxis(s_prev, 0, 1)
    cross = jnp.einsum('bcihd,bchde,hi->bcihe', qc, s_prev, jnp.exp((pos + 1.0) * lg))
    return (intra + cross).reshape(bsz, n, RET_HEADS, RET_DV), s_fin


def retention_bidirectional(q, k, v, log_decay, init_f, init_b):
    o_f, s_f = retention_chunked(q, k, v, log_decay[0], init_f)
    o_b, s_b = retention_chunked(flip_seq(q), flip_seq(k), flip_seq(v), log_decay[1], init_b)
    return o_f + flip_seq(o_b), s_f, s_b


def retention_output(o, g, norm_g):
    bsz, n = o.shape[:2]
    y = layer_norm(o).reshape(bsz, n, RET_WIDTH) * norm_g
    return y * jax.nn.silu(g)


def merge_branches(attn_o, ssd_o, ret_o, gates, w_branch_attn, w_branch_ssd, w_branch_ret, w_out):
    g = jax.nn.sigmoid(gates.astype(jnp.float32)).astype(gates.dtype)
    g_attn, g_ssd, g_ret = jnp.split(g, N_BRANCH, axis=-1)
    merged = g_attn * (attn_o @ w_branch_attn) + g_ssd * (ssd_o @ w_branch_ssd) + g_ret * (ret_o @ w_branch_ret)
    return merged @ w_out


def deepnorm_update(x, y, gate, g, b):
    return layer_norm(DEEPNORM_ALPHA * x + gate * y) * g + b


def sq_relu_mlp(h, w_up, w_down):
    return jnp.square(jax.nn.relu(h @ w_up)) @ w_down


def hybrid_layer(x_lat, x_ctx, c, c_ctx, rope_cos, rope_sin, ada_w, ada_b, w_in, attn_sink,
                 ssd_conv_w, ssd_conv_b, ssd_a_log, ssd_dt_bias, ssd_d, ssd_norm_g,
                 ret_log_decay, ret_norm_g, w_branch_attn, w_branch_ssd, w_branch_ret, w_out,
                 ln1_g, ln1_b, w_mlp_up, w_mlp_down, ln2_g, ln2_b, update_ctx):
    bsz = x_lat.shape[0]
    m_lat = ada_params(c, ada_w, ada_b)
    m_ctx = ada_params(c_ctx, ada_w, ada_b)
    (aq_l, ak_l, av_l, z_l, xs_l, bm_l, cm_l, dt_l, rq_l, rk_l, rv_l, rg_l, gt_l) = mixer_inputs(
        modulate(x_lat, m_lat[0], m_lat[1]), w_in, ssd_conv_w, ssd_conv_b)
    (aq_c, ak_c, av_c, z_c, xs_c, bm_c, cm_c, dt_c, rq_c, rk_c, rv_c, rg_c, gt_c) = mixer_inputs(
        modulate(x_ctx, m_ctx[0], m_ctx[1]), w_in, ssd_conv_w, ssd_conv_b)

    attn_l = windowed_attention(apply_rope(aq_l, rope_cos, rope_sin), apply_rope(ak_l, rope_cos, rope_sin),
                                av_l, ak_c, av_c, attn_sink)

    zeros_ssd = jnp.zeros((bsz, SSD_HEADS, SSD_P, SSD_STATE), jnp.float32)
    ys_c, hc_f, hc_b = ssd_bidirectional(xs_c, dt_c, bm_c, cm_c, ssd_a_log, ssd_dt_bias, zeros_ssd, zeros_ssd)
    ys_l, _, _ = ssd_bidirectional(xs_l, dt_l, bm_l, cm_l, ssd_a_log, ssd_dt_bias, hc_f, hc_b)
    ssd_l = ssd_output(ys_l, xs_l, z_l, ssd_d, ssd_norm_g)

    q_scale = RET_DK ** -0.5
    zeros_ret = jnp.zeros((bsz, RET_HEADS, RET_DK, RET_DV), jnp.float32)
    ro_c, sc_f, sc_b = retention_bidirectional(rq_c * q_scale, rk_c, rv_c, ret_log_decay, zeros_ret, zeros_ret)
    ro_l, _, _ = retention_bidirectional(apply_rope(rq_l, rope_cos, rope_sin) * q_scale,
                                         apply_rope(rk_l, rope_cos, rope_sin), rv_l, ret_log_decay, sc_f, sc_b)
    ret_l = retention_output(ro_l, rg_l, ret_norm_g)

    mix_l = merge_branches(attn_l, ssd_l, ret_l, gt_l, w_branch_attn, w_branch_ssd, w_branch_ret, w_out)
    x_lat = deepnorm_update(x_lat, mix_l, m_lat[2], ln1_g, ln1_b)
    x_lat = deepnorm_update(x_lat, sq_relu_mlp(modulate(x_lat, m_lat[3], m_lat[4]), w_mlp_up, w_mlp_down),
                            m_lat[5], ln2_g, ln2_b)

    if update_ctx:
        attn_c = context_attention(aq_c, ak_c, av_c, attn_sink)
        ssd_c = ssd_output(ys_c, xs_c, z_c, ssd_d, ssd_norm_g)
        ret_c = retention_output(ro_c, rg_c, ret_norm_g)
        mix_c = merge_branches(attn_c, ssd_c, ret_c, gt_c, w_branch_attn, w_branch_ssd, w_branch_ret, w_out)
        x_ctx = deepnorm_update(x_ctx, mix_c, m_ctx[2], ln1_g, ln1_b)
        x_ctx = deepnorm_update(x_ctx, sq_relu_mlp(modulate(x_ctx, m_ctx[3], m_ctx[4]), w_mlp_up, w_mlp_down),
                                m_ctx[5], ln2_g, ln2_b)
    return x_lat, x_ctx


def setup_inputs(seed: int = 0) -> dict:
    key = jax.random.key(seed)
    ks = jax.random.split(key, 32)
    f32 = jnp.float32

    def nrm(k, shape, scale=1.0):
        return jax.random.normal(k, shape, f32) * scale

    dt0 = jnp.exp(jax.random.uniform(ks[9], (DEPTH, 2, SSD_HEADS), f32, math.log(1e-3), math.log(1e-1)))
    ret_base = jnp.log(1.0 - 2.0 ** (-5.0 - jnp.arange(RET_HEADS, dtype=f32)))
    return {
        'x': nrm(ks[0], (BATCH, SEQ, D_MODEL)),
        'c': nrm(ks[1], (BATCH, D_MODEL)),
        'ctx': nrm(ks[2], (BATCH, CTX_LEN, D_MODEL)),
        'c_ctx': nrm(ks[3], (D_MODEL,)),
        'ada_w': nrm(ks[4], (DEPTH, D_MODEL, 6 * D_MODEL), D_MODEL ** -0.5),
        'ada_b': nrm(ks[5], (DEPTH, 6 * D_MODEL), 0.01),
        'w_in': nrm(ks[6], (DEPTH, D_MODEL, IN_COLS), D_MODEL ** -0.5),
        'attn_sink': nrm(ks[7], (DEPTH, ATTN_HQ), 0.5),
        'ssd_conv_w': nrm(ks[8], (DEPTH, SSD_CONV, SSD_CONV_CH), SSD_CONV ** -0.5),
        'ssd_conv_b': nrm(ks[10], (DEPTH, SSD_CONV_CH), 0.01),
        'ssd_a_log': jnp.log(jax.random.uniform(ks[11], (DEPTH, 2, SSD_HEADS), f32, 1.0, 16.0)),
        'ssd_dt_bias': dt0 + jnp.log(-jnp.expm1(-dt0)),
        'ssd_d': 1.0 + nrm(ks[12], (DEPTH, SSD_HEADS), 0.01),
        'ssd_norm_g': 1.0 + nrm(ks[13], (DEPTH, SSD_INNER), 0.01),
        'ret_log_decay': ret_base * (1.0 + nrm(ks[14], (DEPTH, 2, RET_HEADS), 0.01)),
        'ret_norm_g': 1.0 + nrm(ks[15], (DEPTH, RET_WIDTH), 0.01),
        'w_branch_attn': nrm(ks[16], (DEPTH, ATTN_WIDTH, D_MODEL), ATTN_WIDTH ** -0.5),
        'w_branch_ssd': nrm(ks[17], (DEPTH, SSD_INNER, D_MODEL), SSD_INNER ** -0.5),
        'w_branch_ret': nrm(ks[18], (DEPTH, RET_WIDTH, D_MODEL), RET_WIDTH ** -0.5),
        'w_out': nrm(ks[19], (DEPTH, D_MODEL, D_MODEL), DEEPNORM_BETA * D_MODEL ** -0.5),
        'ln1_g': 1.0 + nrm(ks[20], (DEPTH, D_MODEL), 0.01),
        'ln1_b': nrm(ks[21], (DEPTH, D_MODEL), 0.01),
        'w_mlp_up': nrm(ks[22], (DEPTH, D_MODEL, D_FF), D_MODEL ** -0.5),
        'w_mlp_down': nrm(ks[23], (DEPTH, D_FF, D_MODEL), DEEPNORM_BETA * D_FF ** -0.5),
        'ln2_g': 1.0 + nrm(ks[24], (DEPTH, D_MODEL), 0.01),
        'ln2_b': nrm(ks[25], (DEPTH, D_MODEL), 0.01),
    }


def reference(x, c, ctx, c_ctx, ada_w, ada_b, w_in, attn_sink, ssd_conv_w, ssd_conv_b, ssd_a_log,
              ssd_dt_bias, ssd_d, ssd_norm_g, ret_log_decay, ret_norm_g, w_branch_attn, w_branch_ssd,
              w_branch_ret, w_out, ln1_g, ln1_b, w_mlp_up, w_mlp_down, ln2_g, ln2_b):
    rope_cos, rope_sin = axial_rope_table(x.shape[1])
    x_lat, x_ctx = x, ctx
    for l in range(DEPTH):
        x_lat, x_ctx = hybrid_layer(
            x_lat, x_ctx, c, c_ctx, rope_cos, rope_sin, ada_w[l], ada_b[l], w_in[l], attn_sink[l],
            ssd_conv_w[l], ssd_conv_b[l], ssd_a_log[l], ssd_dt_bias[l], ssd_d[l], ssd_norm_g[l],
            ret_log_decay[l], ret_norm_g[l], w_branch_attn[l], w_branch_ssd[l], w_branch_ret[l], w_out[l],
            ln1_g[l], ln1_b[l], w_mlp_up[l], w_mlp_down[l], ln2_g[l], ln2_b[l],
            update_ctx=(l < DEPTH - 1))
    return x_lat
```

```python
import functools
import math

import jax
import jax.numpy as jnp
from jax import lax
from jax.experimental import pallas as pl
from jax.experimental.pallas import tpu as pltpu

F32 = jnp.float32
BF16 = jnp.bfloat16

D_MODEL = 2048
BATCH = 4
SEQ = 4096
DEPTH = 2
GRID_W = 64
CTX_LEN = 256
BLOCK = 128
HEAD_DIM = 128
ATTN_HQ = 8
ATTN_HKV = 2
ATTN_GROUP = ATTN_HQ // ATTN_HKV
ATTN_WIDTH = ATTN_HQ * HEAD_DIM
SSD_HEADS = 16
SSD_P = 64
SSD_INNER = SSD_HEADS * SSD_P
SSD_GROUPS = 2
SSD_STATE = 128
SSD_CONV = 5
SSD_CONV_CH = SSD_INNER + 2 * SSD_GROUPS * SSD_STATE
CHUNK = 128
RET_HEADS = 8
RET_WIDTH = RET_HEADS * HEAD_DIM
D_FF = 4 * D_MODEL
ROPE_BASE = 10000.0
DEEPNORM_ALPHA = (2 * DEPTH) ** 0.25
LN_EPS = 1e-6
NEG_INF = -1e30
Q_SCALE = HEAD_DIM ** -0.5

T_LAT = BATCH * SEQ
T_CTX = BATCH * CTX_LEN
T_ALL = T_LAT + T_CTX
NCHUNK_LAT = SEQ // CHUNK
NCHUNK_CTX = CTX_LEN // CHUNK
NSTEP = NCHUNK_LAT + NCHUNK_CTX

COL_AQ = 0
COL_RQ = 1024
COL_RK = 2048
COL_RV = 3072
COL_Z = 4096
COL_RG = 5120
COL_AK = 6144
COL_AV = 6400
COL_XBC = 6656
COL_GATES = 8192
P_COLS = 14336
DT_PAD = 128

TM_IN = 1024
TN_IN = 512
N_ROPE_FULL = 6
N_SCALED = 4
J_AKAV = COL_AK // TN_IN
TM_MERGE = 256
TM_OUT = 512
TM_MLP = 512
TF_MLP = 1024
TM_CONV = 1024
HALO = 16

VMEM_LIMIT = 56 * 1024 * 1024


def _cparams(sem):
    return pltpu.CompilerParams(dimension_semantics=sem, vmem_limit_bytes=VMEM_LIMIT)


def _sigmoid(x):
    return 1.0 / (1.0 + jnp.exp(-x))


def _silu(x):
    return x * _sigmoid(x)


def _softplus(x):
    return jnp.maximum(x, 0.0) + jnp.log1p(jnp.exp(-jnp.abs(x)))


def _ln_rows(x):
    mu = jnp.mean(x, axis=-1, keepdims=True)
    xc = x - mu
    var = jnp.mean(xc * xc, axis=-1, keepdims=True)
    return xc * lax.rsqrt(var + LN_EPS)


def _dot(a, b):
    return jnp.dot(a, b, preferred_element_type=F32)


def _dot_nt(a, b):
    return lax.dot_general(a, b, (((1,), (1,)), ((), ())), preferred_element_type=F32)


def _dot_tn(a, b):
    return lax.dot_general(a, b, (((0,), (0,)), ((), ())), preferred_element_type=F32)


def _split3(x):
    x1 = x.astype(BF16)
    r1 = x - x1.astype(F32)
    x2 = r1.astype(BF16)
    r2 = r1 - x2.astype(F32)
    return x1, x2, r2.astype(BF16)


def _batch_of_tile(i, tm):
    return jnp.where(i < T_LAT // tm, i // (SEQ // tm), BATCH)


TN_ADA = 1024


def _ada_kernel(cond_ref, w_ref, b_ref, o_ref):
    c = cond_ref[...]
    s = _silu(c).astype(BF16)
    o_ref[...] = _dot(s, w_ref[...].astype(BF16)) + b_ref[...]


def _ada_call(cond, ada_w, ada_b):
    n = ada_w.shape[-1]
    return pl.pallas_call(
        _ada_kernel,
        grid=(DEPTH, n // TN_ADA),
        in_specs=[
            pl.BlockSpec((8, D_MODEL), lambda l, j: (0, 0)),
            pl.BlockSpec((None, D_MODEL, TN_ADA), lambda l, j: (l, 0, j)),
            pl.BlockSpec((None, 1, TN_ADA), lambda l, j: (l, 0, j)),
        ],
        out_specs=pl.BlockSpec((None, 8, TN_ADA), lambda l, j: (l, 0, j)),
        out_shape=jax.ShapeDtypeStruct((DEPTH, 8, n), F32),
        compiler_params=_cparams(("arbitrary", "arbitrary")),
        name="ada",
    )(cond, ada_w, ada_b.reshape(DEPTH, 1, n))


def _rope_store(acc, cos, sin, o_ref, n_heads, scale):
    for h in range(n_heads):
        xs = acc[:, h * HEAD_DIM:(h + 1) * HEAD_DIM]
        rot = pltpu.roll(xs, HEAD_DIM // 2, axis=1)
        o_ref[:, h * HEAD_DIM:(h + 1) * HEAD_DIM] = ((xs * cos + rot * sin) * scale).astype(BF16)


def _inproj_kernel(x_ref, shift_ref, scale_ref, w_ref, wdt_ref, cos_ref, sin_ref,
                   p_ref, dt_ref, h_scr):
    j = pl.program_id(1)

    @pl.when(j == 0)
    def _():
        rc = 256
        for r in range(0, TM_IN, rc):
            h = _ln_rows(x_ref[r:r + rc, :]) * (1.0 + scale_ref[...]) + shift_ref[...]
            h_scr[r:r + rc, :] = h.astype(BF16)
        dt_ref[...] = _dot(h_scr[...], wdt_ref[...])

    acc = _dot(h_scr[...], w_ref[...])
    heads_per_tile = TN_IN // HEAD_DIM

    @pl.when(j < N_ROPE_FULL)
    def _():
        scale = jnp.where(j < N_SCALED, Q_SCALE, 1.0).astype(F32)
        _rope_store(acc, cos_ref[...], sin_ref[...], p_ref, heads_per_tile, scale)

    @pl.when(j == J_AKAV)
    def _():
        _rope_store(acc, cos_ref[...], sin_ref[...], p_ref, ATTN_HKV, 1.0)
        p_ref[:, ATTN_HKV * HEAD_DIM:] = acc[:, ATTN_HKV * HEAD_DIM:].astype(BF16)

    @pl.when((j >= N_ROPE_FULL) & (j != J_AKAV))
    def _():
        p_ref[...] = acc.astype(BF16)


def _inproj_call(x_all, mod, w_perm, w_dt, cos2, sin2):
    n_tiles = T_ALL // TM_IN
    lat_tiles = T_LAT // TM_IN

    def mod_map(k):
        return lambda i, j: (_batch_of_tile(i, TM_IN) * 6 + k, 0, 0)

    def rope_map(i, j):
        return (jnp.where(i < lat_tiles, i % (SEQ // TM_IN), SEQ // TM_IN), 0)

    return pl.pallas_call(
        _inproj_kernel,
        grid=(n_tiles, P_COLS // TN_IN),
        in_specs=[
            pl.BlockSpec((TM_IN, D_MODEL), lambda i, j: (i, 0)),
            pl.BlockSpec((None, 1, D_MODEL), mod_map(0)),
            pl.BlockSpec((None, 1, D_MODEL), mod_map(1)),
            pl.BlockSpec((D_MODEL, TN_IN), lambda i, j: (0, j)),
            pl.BlockSpec((D_MODEL, DT_PAD), lambda i, j: (0, 0)),
            pl.BlockSpec((TM_IN, HEAD_DIM), rope_map),
            pl.BlockSpec((TM_IN, HEAD_DIM), rope_map),
        ],
        out_specs=[
            pl.BlockSpec((TM_IN, TN_IN), lambda i, j: (i, j)),
            pl.BlockSpec((TM_IN, DT_PAD), lambda i, j: (i, 0)),
        ],
        out_shape=[
            jax.ShapeDtypeStruct((T_ALL, P_COLS), BF16),
            jax.ShapeDtypeStruct((T_ALL, DT_PAD), F32),
        ],
        scratch_shapes=[pltpu.VMEM((TM_IN, D_MODEL), BF16)],
        compiler_params=_cparams(("arbitrary", "arbitrary")),
        name="inproj",
    )(x_all, mod, mod, w_perm, w_dt, cos2, sin2)


TN_CONV = 512


def _conv_kernel(x_ref, prev_ref, next_ref, w_ref, b_ref, o_ref):
    i = pl.program_id(0)
    xe = jnp.concatenate(
        [prev_ref[...].astype(F32), x_ref[...].astype(F32), next_ref[...].astype(F32)], axis=0)
    row = lax.broadcasted_iota(jnp.int32, (TM_CONV, TN_CONV), 0) + i * TM_CONV
    is_lat = row < T_LAT
    pos = jnp.where(is_lat, row & (SEQ - 1), row & (CTX_LEN - 1))
    slen = jnp.where(is_lat, SEQ, CTX_LEN)
    w = w_ref[...]
    acc = jnp.zeros((TM_CONV, TN_CONV), F32) + b_ref[...]
    half = SSD_CONV // 2
    for k in range(SSD_CONV):
        off = HALO - half + k
        xk = xe[off:off + TM_CONV, :]
        if k < half:
            xk = jnp.where(pos + (k - half) >= 0, xk, 0.0)
        elif k > half:
            xk = jnp.where(pos + (k - half) < slen, xk, 0.0)
        acc = acc + xk * w[k:k + 1, :]
    o_ref[...] = _silu(acc).astype(BF16)


def _conv_call(p_all, conv_w, conv_b):
    cb0 = COL_XBC // TN_CONV
    hpt = TM_CONV // HALO
    last_halo = T_ALL // HALO - 1
    return pl.pallas_call(
        _conv_kernel,
        grid=(T_ALL // TM_CONV, SSD_CONV_CH // TN_CONV),
        in_specs=[
            pl.BlockSpec((TM_CONV, TN_CONV), lambda i, j: (i, cb0 + j)),
            pl.BlockSpec((HALO, TN_CONV), lambda i, j: (jnp.maximum(i * hpt - 1, 0), cb0 + j)),
            pl.BlockSpec((HALO, TN_CONV), lambda i, j: (jnp.minimum((i + 1) * hpt, last_halo), cb0 + j)),
            pl.BlockSpec((SSD_CONV, TN_CONV), lambda i, j: (0, j)),
            pl.BlockSpec((1, TN_CONV), lambda i, j: (0, j)),
        ],
        out_specs=pl.BlockSpec((TM_CONV, TN_CONV), lambda i, j: (i, j)),
        out_shape=jax.ShapeDtypeStruct((T_ALL, SSD_CONV_CH), BF16),
        compiler_params=_cparams(("arbitrary", "arbitrary")),
        name="conv",
    )(p_all, p_all, p_all, conv_w, conv_b.reshape(1, SSD_CONV_CH))


def _stack_heads(q_ref):
    return jnp.concatenate(
        [q_ref[:, g * HEAD_DIM:(g + 1) * HEAD_DIM] for g in range(ATTN_GROUP)], axis=0)


def _softmax_pv(scores, values, sink, o_ref, nq):
    m = sink
    for s in scores:
        m = jnp.maximum(m, jnp.max(s, axis=-1, keepdims=True))
    l = jnp.exp(sink - m)
    o = None
    for s, v in zip(scores, values):
        p = jnp.exp(s - m)
        l = l + jnp.sum(p, axis=-1, keepdims=True)
        pv = _dot(p.astype(BF16), v)
        o = pv if o is None else o + pv
    o = o / l
    for g in range(ATTN_GROUP):
        o_ref[:, g * HEAD_DIM:(g + 1) * HEAD_DIM] = o[g * nq:(g + 1) * nq, :].astype(BF16)


def _attn_kernel(q_ref, kp_ref, kc_ref, kn_ref, vp_ref, vc_ref, vn_ref, kx_ref, vx_ref,
                 sink_ref, o_ref):
    n = pl.program_id(1)
    q = _stack_heads(q_ref)
    rows = ATTN_GROUP * BLOCK
    qi = lax.broadcasted_iota(jnp.int32, (rows, BLOCK), 0) & (BLOCK - 1)
    kj = lax.broadcasted_iota(jnp.int32, (rows, BLOCK), 1)
    sp = jnp.where((kj >= qi) & (n > 0), _dot_nt(q, kp_ref[...]), NEG_INF)
    sc = _dot_nt(q, kc_ref[...])
    sn = jnp.where((kj <= qi) & (n < SEQ // BLOCK - 1), _dot_nt(q, kn_ref[...]), NEG_INF)
    sx = _dot_nt(q, kx_ref[...])
    _softmax_pv([sp, sc, sn, sx], [vp_ref[...], vc_ref[...], vn_ref[...], vx_ref[...]],
                sink_ref[...], o_ref, BLOCK)


def _ctx_attn_kernel(q_ref, kx_ref, vx_ref, sink_ref, prev_ref, o_ref):
    del prev_ref
    q = _stack_heads(q_ref)
    _softmax_pv([_dot_nt(q, kx_ref[...])], [vx_ref[...]], sink_ref[...], o_ref, CTX_LEN)


def _attn_calls(p_all, sink_rows, sink_rows_ctx, update_ctx):
    nb = SEQ // BLOCK
    qw = ATTN_GROUP * HEAD_DIM
    ck = COL_AK // HEAD_DIM
    cv = COL_AV // HEAD_DIM
    ctx0 = T_LAT // CTX_LEN

    def kv_spec(col0, dn):
        return pl.BlockSpec(
            (BLOCK, HEAD_DIM),
            lambda b, n, h: (b * nb + jnp.clip(n + dn, 0, nb - 1), col0 + h))

    def ctx_spec(col0):
        return pl.BlockSpec((CTX_LEN, HEAD_DIM), lambda b, n, h: (ctx0 + b, col0 + h))

    attn = pl.pallas_call(
        _attn_kernel,
        grid=(BATCH, nb, ATTN_HKV),
        in_specs=[
            pl.BlockSpec((BLOCK, qw), lambda b, n, h: (b * nb + n, h)),
            kv_spec(ck, -1), kv_spec(ck, 0), kv_spec(ck, 1),
            kv_spec(cv, -1), kv_spec(cv, 0), kv_spec(cv, 1),
            ctx_spec(ck), ctx_spec(cv),
            pl.BlockSpec((None, ATTN_GROUP * BLOCK, 1), lambda b, n, h: (h, 0, 0)),
        ],
        out_specs=pl.BlockSpec((BLOCK, qw), lambda b, n, h: (b * nb + n, h)),
        out_shape=jax.ShapeDtypeStruct((T_ALL, ATTN_WIDTH), BF16),
        compiler_params=_cparams(("arbitrary", "arbitrary", "arbitrary")),
        name="attn",
    )(p_all, p_all, p_all, p_all, p_all, p_all, p_all, p_all, p_all, sink_rows)
    if not update_ctx:
        return attn
    return pl.pallas_call(
        _ctx_attn_kernel,
        grid=(BATCH, ATTN_HKV),
        in_specs=[
            pl.BlockSpec((CTX_LEN, qw), lambda b, h: (ctx0 + b, h)),
            pl.BlockSpec((CTX_LEN, HEAD_DIM), lambda b, h: (ctx0 + b, ck + h)),
            pl.BlockSpec((CTX_LEN, HEAD_DIM), lambda b, h: (ctx0 + b, cv + h)),
            pl.BlockSpec((None, ATTN_GROUP * CTX_LEN, 1), lambda b, h: (h, 0, 0)),
            pl.BlockSpec(memory_space=pl.ANY),
        ],
        out_specs=pl.BlockSpec((CTX_LEN, qw), lambda b, h: (ctx0 + b, h)),
        out_shape=jax.ShapeDtypeStruct((T_ALL, ATTN_WIDTH), BF16),
        input_output_aliases={4: 0},
        compiler_params=_cparams(("arbitrary", "arbitrary")),
        name="ctx_attn",
    )(p_all, p_all, p_all, sink_rows_ctx, attn)


def _fwd_chunk(b, s):
    ctx = T_LAT // CHUNK + b * NCHUNK_CTX + s
    lat = b * NCHUNK_LAT + (s - NCHUNK_CTX)
    return jnp.where(s < NCHUNK_CTX, ctx, lat)


def _bwd_chunk(b, s):
    ctx = T_LAT // CHUNK + b * NCHUNK_CTX + (NCHUNK_CTX - 1 - s)
    lat = b * NCHUNK_LAT + (NSTEP - 1 - s)
    return jnp.where(s < NCHUNK_CTX, ctx, lat)


def _ssd_kernel(xs_f, bm_f, cm_f, dt_f, xs_b, bm_b, cm_b, dt_b,
                bias_row, alog_row, bias_col, alog_col, yf_ref, yb_ref, h_scr):
    s = pl.program_id(1)

    @pl.when(s == 0)
    def _():
        h_scr[...] = jnp.zeros_like(h_scr)

    ii = lax.broadcasted_iota(jnp.int32, (CHUNK, CHUNK), 0)
    jj = lax.broadcasted_iota(jnp.int32, (CHUNK, CHUNK), 1)
    lane_lo = jj < SSD_P
    lower = jj <= ii
    upper = jj >= ii
    hg = SSD_HEADS // SSD_GROUPS
    dirs = ((xs_f, bm_f, cm_f, dt_f, yf_ref), (xs_b, bm_b, cm_b, dt_b, yb_ref))
    for d, (xs_ref, bm_ref, cm_ref, dt_ref, y_ref) in enumerate(dirs):
        causal = lower if d == 0 else upper
        tri = jnp.where(causal, 1.0, 0.0).astype(BF16)
        tri_t = jnp.where(upper if d == 0 else lower, 1.0, 0.0).astype(BF16)
        last = CHUNK - 1 if d == 0 else 0
        r0 = d * SSD_HEADS

        dt_raw = dt_ref[...]
        dt_c = _softplus(dt_raw + bias_row[...])
        adt_c = dt_c * (-jnp.exp(alog_row[...]))
        acs = sum(_dot(tri, p) for p in _split3(adt_c))
        dt_t = _softplus(dt_raw.T[r0:r0 + SSD_HEADS, :] + bias_col[r0:r0 + SSD_HEADS, :])
        adt_t = dt_t * (-jnp.exp(alog_col[r0:r0 + SSD_HEADS, :]))
        acs_t = sum(_dot(p, tri_t) for p in _split3(adt_t))
        dte_t = jnp.exp(acs_t[:, last:last + 1] - acs_t) * dt_t
        tot = acs[last:last + 1, :]

        for g in range(SSD_GROUPS):
            bm = bm_ref[:, g * SSD_STATE:(g + 1) * SSD_STATE]
            cm = cm_ref[:, g * SSD_STATE:(g + 1) * SSD_STATE]
            cb = _dot_nt(cm, bm)
            cm32 = cm.astype(F32)
            bm_t = bm.astype(F32).T
            for v in range(hg // 2):
                pair = g * (hg // 2) + v
                lanes = slice(pair * 2 * SSD_P, (pair + 1) * 2 * SSD_P)
                x_pair = xs_ref[:, lanes]
                h_pair = h_scr[d, :, lanes]
                rhs = jnp.concatenate([x_pair, h_pair.astype(BF16)], axis=0)
                ys, ups, cds = [], [], []
                for e in range(2):
                    h = pair * 2 + e
                    col = acs[:, r0 + h:r0 + h + 1]
                    seg = col - acs_t[h:h + 1, :]
                    dec = jnp.exp(jnp.where(causal, seg, NEG_INF))
                    m_intra = cb * dec * dt_t[h:h + 1, :]
                    m_state = cm32 * jnp.exp(col)
                    lhs = jnp.concatenate([m_intra.astype(BF16), m_state.astype(BF16)], axis=1)
                    ys.append(_dot(lhs, rhs))
                    ups.append(_dot((bm_t * dte_t[h:h + 1, :]).astype(BF16), x_pair))
                    cds.append(jnp.exp(tot[:, r0 + h:r0 + h + 1]))
                y_ref[:, lanes] = jnp.where(lane_lo, ys[0], ys[1])
                h_scr[d, :, lanes] = (jnp.where(lane_lo, cds[0], cds[1]) * h_pair
                                      + jnp.where(lane_lo, ups[0], ups[1]))


def _ssd_call(xbc, dt_all, bias_row, alog_row, bias_col, alog_col):
    bcol = SSD_INNER // (SSD_GROUPS * SSD_STATE)

    def specs(chunk_fn):
        return [
            pl.BlockSpec((CHUNK, SSD_INNER), lambda b, s: (chunk_fn(b, s), 0)),
            pl.BlockSpec((CHUNK, SSD_GROUPS * SSD_STATE), lambda b, s: (chunk_fn(b, s), bcol)),
            pl.BlockSpec((CHUNK, SSD_GROUPS * SSD_STATE), lambda b, s: (chunk_fn(b, s), bcol + 1)),
            pl.BlockSpec((CHUNK, DT_PAD), lambda b, s: (chunk_fn(b, s), 0)),
        ]

    const = lambda shape: pl.BlockSpec(shape, lambda b, s: (0, 0))
    return pl.pallas_call(
        _ssd_kernel,
        grid=(BATCH, NSTEP),
        in_specs=specs(_fwd_chunk) + specs(_bwd_chunk) + [
            const((1, DT_PAD)), const((1, DT_PAD)), const((DT_PAD, 1)), const((DT_PAD, 1))],
        out_specs=[
            pl.BlockSpec((CHUNK, SSD_INNER), lambda b, s: (_fwd_chunk(b, s), 0)),
            pl.BlockSpec((CHUNK, SSD_INNER), lambda b, s: (_bwd_chunk(b, s), 0)),
        ],
        out_shape=[jax.ShapeDtypeStruct((T_ALL, SSD_INNER), F32)] * 2,
        scratch_shapes=[pltpu.VMEM((2, SSD_STATE, SSD_INNER), F32)],
        compiler_params=_cparams(("arbitrary", "arbitrary")),
        name="ssd",
    )(xbc, xbc, xbc, dt_all, xbc, xbc, xbc, dt_all, bias_row, alog_row, bias_col, alog_col)


def _ret_kernel(lg_ref, q_f, k_f, v_f, q_b, k_b, v_b, of_ref, ob_ref, s_scr):
    s = pl.program_id(1)

    @pl.when(s == 0)
    def _():
        s_scr[...] = jnp.zeros_like(s_scr)

    ii = lax.broadcasted_iota(jnp.int32, (CHUNK, CHUNK), 0)
    jj = lax.broadcasted_iota(jnp.int32, (CHUNK, CHUNK), 1)
    dirs = ((q_f, k_f, v_f, of_ref), (q_b, k_b, v_b, ob_ref))
    for d, (q_ref, k_ref, v_ref, o_ref) in enumerate(dirs):
        if d == 0:
            dist = (ii - jj).astype(F32)
            row_pow = (ii + 1).astype(F32)
            key_pow = (CHUNK - 1 - ii).astype(F32)
        else:
            dist = (jj - ii).astype(F32)
            row_pow = (CHUNK - ii).astype(F32)
            key_pow = ii.astype(F32)
        for h in range(RET_HEADS):
            lanes = slice(h * HEAD_DIM, (h + 1) * HEAD_DIM)
            lg = lg_ref[d, h]
            q = q_ref[:, lanes]
            k = k_ref[:, lanes]
            v = v_ref[:, lanes]
            decay = jnp.where(dist >= 0.0, jnp.exp(jnp.maximum(dist, 0.0) * lg), 0.0)
            scores = _dot_nt(q, k) * decay
            q_cross = q.astype(F32) * jnp.exp(row_pow * lg)
            state = s_scr[d, h]
            lhs = jnp.concatenate([scores.astype(BF16), q_cross.astype(BF16)], axis=1)
            rhs = jnp.concatenate([v, state.astype(BF16)], axis=0)
            o_ref[:, lanes] = _dot(lhs, rhs)
            k_dec = (k.astype(F32) * jnp.exp(key_pow * lg)).astype(BF16)
            chunk_decay = jnp.exp(jnp.full((1, HEAD_DIM), float(CHUNK), F32) * lg)
            s_scr[d, h] = chunk_decay * state + _dot_tn(k_dec, v)


def _ret_call(p_all, log_decay):
    def specs(chunk_fn):
        return [
            pl.BlockSpec((CHUNK, RET_WIDTH), lambda b, s: (chunk_fn(b, s), COL_RQ // RET_WIDTH)),
            pl.BlockSpec((CHUNK, RET_WIDTH), lambda b, s: (chunk_fn(b, s), COL_RK // RET_WIDTH)),
            pl.BlockSpec((CHUNK, RET_WIDTH), lambda b, s: (chunk_fn(b, s), COL_RV // RET_WIDTH)),
        ]

    return pl.pallas_call(
        _ret_kernel,
        grid=(BATCH, NSTEP),
        in_specs=[pl.BlockSpec(memory_space=pltpu.SMEM)] + specs(_fwd_chunk) + specs(_bwd_chunk),
        out_specs=[
            pl.BlockSpec((CHUNK, RET_WIDTH), lambda b, s: (_fwd_chunk(b, s), 0)),
            pl.BlockSpec((CHUNK, RET_WIDTH), lambda b, s: (_bwd_chunk(b, s), 0)),
        ],
        out_shape=[jax.ShapeDtypeStruct((T_ALL, RET_WIDTH), F32)] * 2,
        scratch_shapes=[pltpu.VMEM((2, RET_HEADS, HEAD_DIM, HEAD_DIM), F32)],
        compiler_params=_cparams(("arbitrary", "arbitrary")),
        name="retention",
    )(log_decay, p_all, p_all, p_all, p_all, p_all, p_all)


def _merge_kernel(attn_ref, yf_ref, yb_ref, xs_ref, z_ref, of_ref, ob_ref, rg_ref,
                  ga_ref, gs_ref, gr_ref, dskip_ref, ssd_g_ref, ret_g_ref,
                  wa_ref, ws_ref, wr_ref, o_ref):
    y = yf_ref[...] + yb_ref[...] + dskip_ref[...] * xs_ref[...].astype(F32)
    y = y * _silu(z_ref[...].astype(F32))
    ssd_o = y * lax.rsqrt(jnp.mean(y * y, axis=-1, keepdims=True) + LN_EPS) * ssd_g_ref[...]

    o = of_ref[...] + ob_ref[...]
    normed = jnp.concatenate(
        [_ln_rows(o[:, h * HEAD_DIM:(h + 1) * HEAD_DIM]) for h in range(RET_HEADS)], axis=1)
    ret_o = normed * ret_g_ref[...] * _silu(rg_ref[...].astype(F32))

    merged = (_sigmoid(ga_ref[...].astype(F32)) * _dot(attn_ref[...], wa_ref[...])
              + _sigmoid(gs_ref[...].astype(F32)) * _dot(ssd_o.astype(BF16), ws_ref[...])
              + _sigmoid(gr_ref[...].astype(F32)) * _dot(ret_o.astype(BF16), wr_ref[...]))
    o_ref[...] = merged.astype(BF16)


def _merge_call(n_rows, attn, yf, yb, xbc, p_all, of, ob, dskip, ssd_g, ret_g, wa, ws, wr):
    tm = TM_MERGE
    w1024 = lambda c: pl.BlockSpec((tm, 1024), lambda i: (i, c))
    gate = lambda c: pl.BlockSpec((tm, D_MODEL), lambda i: (i, COL_GATES // D_MODEL + c))
    vec = pl.BlockSpec((1, 1024), lambda i: (0, 0))
    wspec = pl.BlockSpec((1024, D_MODEL), lambda i: (0, 0), pipeline_mode=pl.Buffered(1))
    return pl.pallas_call(
        _merge_kernel,
        grid=(n_rows // tm,),
        in_specs=[
            w1024(0), w1024(0), w1024(0), w1024(0), w1024(COL_Z // 1024),
            w1024(0), w1024(0), w1024(COL_RG // 1024),
            gate(0), gate(1), gate(2), vec, vec, vec, wspec, wspec, wspec,
        ],
        out_specs=pl.BlockSpec((tm, D_MODEL), lambda i: (i, 0)),
        out_shape=jax.ShapeDtypeStruct((n_rows, D_MODEL), BF16),
        compiler_params=_cparams(("arbitrary",)),
        name="merge",
    )(attn, yf, yb, xbc, p_all, of, ob, p_all, p_all, p_all, p_all, dskip, ssd_g, ret_g, wa, ws, wr)


def _deepnorm(x, y, gate, g, b):
    return _ln_rows(DEEPNORM_ALPHA * x + gate * y) * g + b


def _outproj_kernel(m_ref, x_ref, w_ref, gate_ref, g_ref, b_ref, o_ref):
    mix = _dot(m_ref[...], w_ref[...])
    o_ref[...] = _deepnorm(x_ref[...], mix, gate_ref[...], g_ref[...], b_ref[...])


def _outproj_call(n_rows, merged, x_all, w_out, mod, ln_g, ln_b):
    tm = TM_OUT
    vec = pl.BlockSpec((1, D_MODEL), lambda i: (0, 0))
    return pl.pallas_call(
        _outproj_kernel,
        grid=(n_rows // tm,),
        in_specs=[
            pl.BlockSpec((tm, D_MODEL), lambda i: (i, 0)),
            pl.BlockSpec((tm, D_MODEL), lambda i: (i, 0)),
            pl.BlockSpec((D_MODEL, D_MODEL), lambda i: (0, 0), pipeline_mode=pl.Buffered(1)),
            pl.BlockSpec((None, 1, D_MODEL), lambda i: (_batch_of_tile(i, tm) * 6 + 2, 0, 0)),
            vec, vec,
        ],
        out_specs=pl.BlockSpec((tm, D_MODEL), lambda i: (i, 0)),
        out_shape=jax.ShapeDtypeStruct((n_rows, D_MODEL), F32),
        compiler_params=_cparams(("arbitrary",)),
        name="outproj",
    )(merged, x_all, w_out, mod, ln_g, ln_b)


def _mlp_kernel(x_ref, shift_ref, scale_ref, gate_ref, wup_ref, wdn_ref, g_ref, b_ref,
                o_ref, h_scr, acc_scr):
    j = pl.program_id(1)

    @pl.when(j == 0)
    def _():
        h = _ln_rows(x_ref[...]) * (1.0 + scale_ref[...]) + shift_ref[...]
        h_scr[...] = h.astype(BF16)

    u = jnp.maximum(_dot(h_scr[...], wup_ref[...]), 0.0)
    part = _dot((u * u).astype(BF16), wdn_ref[...])

    @pl.when(j == 0)
    def _():
        acc_scr[...] = part

    @pl.when(j > 0)
    def _():
        acc_scr[...] += part

    @pl.when(j == pl.num_programs(1) - 1)
    def _():
        o_ref[...] = _deepnorm(x_ref[...], acc_scr[...], gate_ref[...], g_ref[...], b_ref[...])


def _mlp_call(n_rows, x1, mod, w_up, w_down, ln_g, ln_b):
    tm, tf = TM_MLP, TF_MLP

    def mod_map(k):
        return lambda i, j: (_batch_of_tile(i, tm) * 6 + k, 0, 0)

    vec = pl.BlockSpec((1, D_MODEL), lambda i, j: (0, 0))
    return pl.pallas_call(
        _mlp_kernel,
        grid=(n_rows // tm, D_FF // tf),
        in_specs=[
            pl.BlockSpec((tm, D_MODEL), lambda i, j: (i, 0)),
            pl.BlockSpec((None, 1, D_MODEL), mod_map(3)),
            pl.BlockSpec((None, 1, D_MODEL), mod_map(4)),
            pl.BlockSpec((None, 1, D_MODEL), mod_map(5)),
            pl.BlockSpec((D_MODEL, tf), lambda i, j: (0, j)),
            pl.BlockSpec((tf, D_MODEL), lambda i, j: (j, 0)),
            vec, vec,
        ],
        out_specs=pl.BlockSpec((tm, D_MODEL), lambda i, j: (i, 0)),
        out_shape=jax.ShapeDtypeStruct((n_rows, D_MODEL), F32),
        scratch_shapes=[pltpu.VMEM((tm, D_MODEL), BF16), pltpu.VMEM((tm, D_MODEL), F32)],
        compiler_params=_cparams(("arbitrary", "arbitrary")),
        name="mlp",
    )(x1, mod, mod, mod, w_up, w_down, ln_g, ln_b)


def _rope_tables():
    rows = SEQ // GRID_W
    row = jnp.repeat(jnp.arange(rows), GRID_W).astype(F32)
    col = (jnp.arange(rows * GRID_W) % GRID_W).astype(F32)
    n_freq = HEAD_DIM // 4
    inv = ROPE_BASE ** (-jnp.arange(n_freq, dtype=F32) / n_freq)
    ang = jnp.concatenate([row[:, None] * inv, col[:, None] * inv], axis=-1)
    cos, sin = jnp.cos(ang), jnp.sin(ang)
    cos2 = jnp.concatenate([cos, cos], axis=-1)
    sin2 = jnp.concatenate([-sin, sin], axis=-1)
    cos2 = jnp.concatenate([cos2, jnp.ones((TM_IN, HEAD_DIM), F32)], axis=0)
    sin2 = jnp.concatenate([sin2, jnp.zeros((TM_IN, HEAD_DIM), F32)], axis=0)
    return cos2, sin2


def _permute_w_in(w):
    sizes = (ATTN_WIDTH, ATTN_HKV * HEAD_DIM, ATTN_HKV * HEAD_DIM, SSD_INNER, SSD_CONV_CH,
             2 * SSD_HEADS, RET_WIDTH, RET_WIDTH, RET_WIDTH, RET_WIDTH, 3 * D_MODEL)
    offs = [0]
    for sz in sizes:
        offs.append(offs[-1] + sz)
    aq, ak, av, z, xbc, dt, rq, rk, rv, rg, gates = [w[:, offs[i]:offs[i + 1]] for i in range(11)]
    w_perm = jnp.concatenate([aq, rq, rk, rv, z, rg, ak, av, xbc, gates], axis=1).astype(BF16)
    w_dt = jnp.pad(dt, ((0, 0), (0, DT_PAD - 2 * SSD_HEADS))).astype(BF16)
    return w_perm, w_dt


def _pad_heads(v):
    flat = jnp.pad(v.reshape(-1).astype(F32), (0, DT_PAD - 2 * SSD_HEADS))
    return flat.reshape(1, DT_PAD), flat.reshape(DT_PAD, 1)


def kernel(x, c, ctx, c_ctx, ada_w, ada_b, w_in, attn_sink, ssd_conv_w, ssd_conv_b, ssd_a_log,
           ssd_dt_bias, ssd_d, ssd_norm_g, ret_log_decay, ret_norm_g, w_branch_attn, w_branch_ssd,
           w_branch_ret, w_out, ln1_g, ln1_b, w_mlp_up, w_mlp_down, ln2_g, ln2_b):
    x_all = jnp.concatenate([x.reshape(T_LAT, D_MODEL), ctx.reshape(T_CTX, D_MODEL)], axis=0)
    cond = jnp.concatenate([c, c_ctx[None, :], jnp.zeros((8 - BATCH - 1, D_MODEL), F32)], axis=0)
    mod_all = _ada_call(cond, ada_w, ada_b).reshape(DEPTH, 8 * 6, 1, D_MODEL)
    cos2, sin2 = _rope_tables()

    for l in range(DEPTH):
        update_ctx = l < DEPTH - 1
        n_rows = T_ALL if update_ctx else T_LAT
        mod = mod_all[l]
        w_perm, w_dt = _permute_w_in(w_in[l])
        p_all, dt_all = _inproj_call(x_all, mod, w_perm, w_dt, cos2, sin2)
        xbc = _conv_call(p_all, ssd_conv_w[l], ssd_conv_b[l])

        sink = attn_sink[l].astype(F32)
        sink_rows = jnp.repeat(sink, BLOCK).reshape(ATTN_HKV, ATTN_GROUP * BLOCK, 1)
        sink_rows_ctx = jnp.repeat(sink, CTX_LEN).reshape(ATTN_HKV, ATTN_GROUP * CTX_LEN, 1)
        attn = _attn_calls(p_all, sink_rows, sink_rows_ctx, update_ctx)

        bias_row, bias_col = _pad_heads(ssd_dt_bias[l])
        alog_row, alog_col = _pad_heads(ssd_a_log[l])
        yf, yb = _ssd_call(xbc, dt_all, bias_row, alog_row, bias_col, alog_col)
        of, ob = _ret_call(p_all, ret_log_decay[l].astype(F32))

        dskip = jnp.repeat(ssd_d[l].astype(F32), SSD_P).reshape(1, SSD_INNER)
        merged = _merge_call(
            n_rows, attn, yf, yb, xbc, p_all, of, ob, dskip,
            ssd_norm_g[l].reshape(1, SSD_INNER), ret_norm_g[l].reshape(1, RET_WIDTH),
            w_branch_attn[l].astype(BF16), w_branch_ssd[l].astype(BF16), w_branch_ret[l].astype(BF16))
        x1 = _outproj_call(n_rows, merged, x_all, w_out[l].astype(BF16), mod,
                           ln1_g[l].reshape(1, D_MODEL), ln1_b[l].reshape(1, D_MODEL))
        x_all = _mlp_call(n_rows, x1, mod, w_mlp_up[l].astype(BF16), w_mlp_down[l].astype(BF16),
                          ln2_g[l].reshape(1, D_MODEL), ln2_b[l].reshape(1, D_MODEL))
    return x_all.reshape(BATCH, SEQ, D_MODEL)
```

```python
import functools
import math

import jax
import jax.numpy as jnp
from jax import lax
from jax.experimental import pallas as pl
from jax.experimental.pallas import tpu as pltpu

F32 = jnp.float32
BF16 = jnp.bfloat16

D_MODEL = 2048
BATCH = 4
SEQ = 4096
DEPTH = 2
GRID_W = 64
CTX_LEN = 256
BLOCK = 128
HEAD_DIM = 128
ATTN_HQ = 8
ATTN_HKV = 2
ATTN_GROUP = ATTN_HQ // ATTN_HKV
ATTN_WIDTH = ATTN_HQ * HEAD_DIM
SSD_HEADS = 16
SSD_P = 64
SSD_INNER = SSD_HEADS * SSD_P
SSD_GROUPS = 2
SSD_STATE = 128
SSD_CONV = 5
SSD_CONV_CH = SSD_INNER + 2 * SSD_GROUPS * SSD_STATE
CHUNK = 128
RET_HEADS = 8
RET_WIDTH = RET_HEADS * HEAD_DIM
D_FF = 4 * D_MODEL
ROPE_BASE = 10000.0
DEEPNORM_ALPHA = (2 * DEPTH) ** 0.25
LN_EPS = 1e-6
NEG_INF = -1e30
Q_SCALE = HEAD_DIM ** -0.5
LOG2E = 1.4426950408889634

T_LAT = BATCH * SEQ
T_CTX = BATCH * CTX_LEN
T_ALL = T_LAT + T_CTX
NCHUNK_LAT = SEQ // CHUNK
NCHUNK_CTX = CTX_LEN // CHUNK
NSTEP = NCHUNK_LAT + NCHUNK_CTX

COL_AQ = 0
COL_RQ = 1024
COL_RK = 2048
COL_RV = 3072
COL_Z = 4096
COL_RG = 5120
COL_AK = 6144
COL_AV = 6400
COL_XBC = 6656
COL_GATES = 8192
P_COLS = 14336
DT_PAD = 128

TM_IN = 1024
TN_IN = 512
N_ROPE_FULL = 6
N_SCALED = 4
J_AKAV = COL_AK // TN_IN
TM_MERGE = 256
TM_OUT = 512
TM_MLP = 512
TF_MLP = 1024
TM_CONV = 1024
HALO = 16

VMEM_LIMIT = 56 * 1024 * 1024


def _cparams(sem):
    return pltpu.CompilerParams(dimension_semantics=sem, vmem_limit_bytes=VMEM_LIMIT)


def _sigmoid(x):
    return 1.0 / (1.0 + jnp.exp(-x))


def _silu(x):
    return x * _sigmoid(x)


def _softplus(x):
    return jnp.maximum(x, 0.0) + jnp.log1p(jnp.exp(-jnp.abs(x)))


def _ln_rows(x):
    mu = jnp.mean(x, axis=-1, keepdims=True)
    xc = x - mu
    var = jnp.mean(xc * xc, axis=-1, keepdims=True)
    return xc * lax.rsqrt(var + LN_EPS)


def _dot(a, b):
    return jnp.dot(a, b, preferred_element_type=F32)


def _dot_nt(a, b):
    return lax.dot_general(a, b, (((1,), (1,)), ((), ())), preferred_element_type=F32)


def _dot_tn(a, b):
    return lax.dot_general(a, b, (((0,), (0,)), ((), ())), preferred_element_type=F32)


def _split3(x):
    x1 = x.astype(BF16)
    r1 = x - x1.astype(F32)
    x2 = r1.astype(BF16)
    r2 = r1 - x2.astype(F32)
    return x1, x2, r2.astype(BF16)


def _batch_of_tile(i, tm):
    return jnp.where(i < T_LAT // tm, i // (SEQ // tm), BATCH)


TN_ADA = 1024


def _ada_kernel(cond_ref, w_ref, b_ref, o_ref):
    c = cond_ref[...]
    s = _silu(c).astype(BF16)
    o_ref[...] = _dot(s, w_ref[...].astype(BF16)) + b_ref[...]


def _ada_call(cond, ada_w, ada_b):
    n = ada_w.shape[-1]
    return pl.pallas_call(
        _ada_kernel,
        grid=(DEPTH, n // TN_ADA),
        in_specs=[
            pl.BlockSpec((8, D_MODEL), lambda l, j: (0, 0)),
            pl.BlockSpec((None, D_MODEL, TN_ADA), lambda l, j: (l, 0, j)),
            pl.BlockSpec((None, 1, TN_ADA), lambda l, j: (l, 0, j)),
        ],
        out_specs=pl.BlockSpec((None, 8, TN_ADA), lambda l, j: (l, 0, j)),
        out_shape=jax.ShapeDtypeStruct((DEPTH, 8, n), F32),
        compiler_params=_cparams(("arbitrary", "arbitrary")),
        name="ada",
    )(cond, ada_w, ada_b.reshape(DEPTH, 1, n))


def _rope_store(acc, cos, sin, o_ref, n_heads, scale):
    for h in range(n_heads):
        xs = acc[:, h * HEAD_DIM:(h + 1) * HEAD_DIM]
        rot = pltpu.roll(xs, HEAD_DIM // 2, axis=1)
        o_ref[:, h * HEAD_DIM:(h + 1) * HEAD_DIM] = ((xs * cos + rot * sin) * scale).astype(BF16)


def _inproj_kernel(x_ref, shift_ref, scale_ref, w_ref, wdt_ref, cos_ref, sin_ref,
                   p_ref, dt_ref, h_scr):
    j = pl.program_id(1)

    @pl.when(j == 0)
    def _():
        rc = 256
        for r in range(0, TM_IN, rc):
            h = _ln_rows(x_ref[r:r + rc, :]) * (1.0 + scale_ref[...]) + shift_ref[...]
            h_scr[r:r + rc, :] = h.astype(BF16)
        dt_ref[...] = _dot(h_scr[...], wdt_ref[...])

    heads_per_tile = TN_IN // HEAD_DIM

    @pl.when(j < N_ROPE_FULL)
    def _():
        acc = _dot(h_scr[...], w_ref[...])
        scale = jnp.where(j < N_SCALED, Q_SCALE, 1.0).astype(F32)
        _rope_store(acc, cos_ref[...], sin_ref[...], p_ref, heads_per_tile, scale)

    @pl.when(j == J_AKAV)
    def _():
        acc = _dot(h_scr[...], w_ref[...])
        _rope_store(acc, cos_ref[...], sin_ref[...], p_ref, ATTN_HKV, 1.0)
        p_ref[:, ATTN_HKV * HEAD_DIM:] = acc[:, ATTN_HKV * HEAD_DIM:].astype(BF16)

    @pl.when((j >= N_ROPE_FULL) & (j != J_AKAV))
    def _():
        p_ref[...] = _dot(h_scr[...], w_ref[...]).astype(BF16)


def _inproj_call(x_all, mod, w_perm, w_dt, cos2, sin2):
    n_tiles = T_ALL // TM_IN
    lat_tiles = T_LAT // TM_IN

    def mod_map(k):
        return lambda i, j: (_batch_of_tile(i, TM_IN) * 6 + k, 0, 0)

    def rope_map(i, j):
        return (jnp.where(i < lat_tiles, i % (SEQ // TM_IN), SEQ // TM_IN), 0)

    return pl.pallas_call(
        _inproj_kernel,
        grid=(n_tiles, P_COLS // TN_IN),
        in_specs=[
            pl.BlockSpec((TM_IN, D_MODEL), lambda i, j: (i, 0)),
            pl.BlockSpec((None, 1, D_MODEL), mod_map(0)),
            pl.BlockSpec((None, 1, D_MODEL), mod_map(1)),
            pl.BlockSpec((D_MODEL, TN_IN), lambda i, j: (0, j)),
            pl.BlockSpec((D_MODEL, DT_PAD), lambda i, j: (0, 0)),
            pl.BlockSpec((TM_IN, HEAD_DIM), rope_map),
            pl.BlockSpec((TM_IN, HEAD_DIM), rope_map),
        ],
        out_specs=[
            pl.BlockSpec((TM_IN, TN_IN), lambda i, j: (i, j)),
            pl.BlockSpec((TM_IN, DT_PAD), lambda i, j: (i, 0)),
        ],
        out_shape=[
            jax.ShapeDtypeStruct((T_ALL, P_COLS), BF16),
            jax.ShapeDtypeStruct((T_ALL, DT_PAD), F32),
        ],
        scratch_shapes=[pltpu.VMEM((TM_IN, D_MODEL), BF16)],
        compiler_params=_cparams(("arbitrary", "arbitrary")),
        name="inproj",
    )(x_all, mod, mod, w_perm, w_dt, cos2, sin2)


TN_CONV = 512


def _conv_kernel(x_ref, prev_ref, next_ref, w_ref, b_ref, o_ref):
    i = pl.program_id(0)
    xe = jnp.concatenate(
        [prev_ref[...].astype(F32), x_ref[...].astype(F32), next_ref[...].astype(F32)], axis=0)
    row = lax.broadcasted_iota(jnp.int32, (TM_CONV, TN_CONV), 0) + i * TM_CONV
    is_lat = row < T_LAT
    pos = jnp.where(is_lat, row & (SEQ - 1), row & (CTX_LEN - 1))
    slen = jnp.where(is_lat, SEQ, CTX_LEN)
    w = w_ref[...]
    acc = jnp.zeros((TM_CONV, TN_CONV), F32) + b_ref[...]
    half = SSD_CONV // 2
    for k in range(SSD_CONV):
        off = HALO - half + k
        xk = xe[off:off + TM_CONV, :]
        if k < half:
            xk = jnp.where(pos + (k - half) >= 0, xk, 0.0)
        elif k > half:
            xk = jnp.where(pos + (k - half) < slen, xk, 0.0)
        acc = acc + xk * w[k:k + 1, :]
    o_ref[...] = _silu(acc).astype(BF16)


def _conv_call(p_all, conv_w, conv_b):
    cb0 = COL_XBC // TN_CONV
    hpt = TM_CONV // HALO
    last_halo = T_ALL // HALO - 1
    return pl.pallas_call(
        _conv_kernel,
        grid=(T_ALL // TM_CONV, SSD_CONV_CH // TN_CONV),
        in_specs=[
            pl.BlockSpec((TM_CONV, TN_CONV), lambda i, j: (i, cb0 + j)),
            pl.BlockSpec((HALO, TN_CONV), lambda i, j: (jnp.maximum(i * hpt - 1, 0), cb0 + j)),
            pl.BlockSpec((HALO, TN_CONV), lambda i, j: (jnp.minimum((i + 1) * hpt, last_halo), cb0 + j)),
            pl.BlockSpec((SSD_CONV, TN_CONV), lambda i, j: (0, j)),
            pl.BlockSpec((1, TN_CONV), lambda i, j: (0, j)),
        ],
        out_specs=pl.BlockSpec((TM_CONV, TN_CONV), lambda i, j: (i, j)),
        out_shape=jax.ShapeDtypeStruct((T_ALL, SSD_CONV_CH), BF16),
        compiler_params=_cparams(("arbitrary", "arbitrary")),
        name="conv",
    )(p_all, p_all, p_all, conv_w, conv_b.reshape(1, SSD_CONV_CH))


def _softmax_pv(s, vals, sink_col):
    m = jnp.maximum(jnp.max(s, axis=-1, keepdims=True), sink_col)
    p = jnp.exp(s - m)
    l = jnp.sum(p, axis=-1, keepdims=True) + jnp.exp(sink_col - m)
    return _dot(p.astype(BF16), vals) / l


def _group_queries(q_ref, hkv):
    h0 = hkv * ATTN_GROUP
    return jnp.concatenate(
        [q_ref[:, (h0 + g) * HEAD_DIM:(h0 + g + 1) * HEAD_DIM] for g in range(ATTN_GROUP)], axis=0)


def _sink_column(sink_ref, hkv, nq):
    return jnp.concatenate(
        [jnp.full((nq, 1), sink_ref[hkv * ATTN_GROUP + g], F32) for g in range(ATTN_GROUP)], axis=0)


def _attn_kernel(sink_ref, q_ref, kp_ref, kc_ref, kn_ref, vp_ref, vc_ref, vn_ref, kx_ref, vx_ref,
                 o_ref):
    n = pl.program_id(1)
    rows = ATTN_GROUP * BLOCK
    qi = lax.broadcasted_iota(jnp.int32, (rows, BLOCK), 0) & (BLOCK - 1)
    kj = lax.broadcasted_iota(jnp.int32, (rows, BLOCK), 1)
    prev_ok = (kj >= qi) & (n > 0)
    next_ok = (kj <= qi) & (n < SEQ // BLOCK - 1)

    def mask_fn(s):
        return jnp.concatenate([
            jnp.where(prev_ok, s[:, :BLOCK], NEG_INF), s[:, BLOCK:2 * BLOCK],
            jnp.where(next_ok, s[:, 2 * BLOCK:3 * BLOCK], NEG_INF), s[:, 3 * BLOCK:]], axis=1)

    kv_cols = lambda hkv: slice(hkv * HEAD_DIM, (hkv + 1) * HEAD_DIM)
    scores = []
    for hkv in range(ATTN_HKV):
        c = kv_cols(hkv)
        keys = jnp.concatenate([kp_ref[:, c], kc_ref[:, c], kn_ref[:, c], kx_ref[:, c]], axis=0)
        scores.append(mask_fn(_dot_nt(_group_queries(q_ref, hkv), keys)))
    outs = []
    for hkv in range(ATTN_HKV):
        c = kv_cols(hkv)
        vals = jnp.concatenate([vp_ref[:, c], vc_ref[:, c], vn_ref[:, c], vx_ref[:, c]], axis=0)
        o = _softmax_pv(scores[hkv], vals, _sink_column(sink_ref, hkv, BLOCK))
        outs += [o[g * BLOCK:(g + 1) * BLOCK, :] for g in range(ATTN_GROUP)]
    o_ref[...] = jnp.concatenate(outs, axis=1).astype(BF16)


def _ctx_attn_kernel(sink_ref, q_ref, kx_ref, vx_ref, prev_ref, o_ref):
    del prev_ref
    kv_cols = lambda hkv: slice(hkv * HEAD_DIM, (hkv + 1) * HEAD_DIM)
    scores = [_dot_nt(_group_queries(q_ref, hkv), kx_ref[:, kv_cols(hkv)]) for hkv in range(ATTN_HKV)]
    outs = []
    for hkv in range(ATTN_HKV):
        o = _softmax_pv(scores[hkv], vx_ref[:, kv_cols(hkv)], _sink_column(sink_ref, hkv, CTX_LEN))
        outs += [o[g * CTX_LEN:(g + 1) * CTX_LEN, :] for g in range(ATTN_GROUP)]
    o_ref[...] = jnp.concatenate(outs, axis=1).astype(BF16)


def _attn_calls(p_all, sink, update_ctx):
    nb = SEQ // BLOCK
    kvw = ATTN_HKV * HEAD_DIM
    ck = COL_AK // kvw
    cv = COL_AV // kvw
    ctx0 = T_LAT // CTX_LEN
    smem = pl.BlockSpec(memory_space=pltpu.SMEM)

    def kv_spec(col, dn):
        return pl.BlockSpec((BLOCK, kvw), lambda b, n: (b * nb + jnp.clip(n + dn, 0, nb - 1), col))

    def ctx_spec(col):
        return pl.BlockSpec((CTX_LEN, kvw), lambda b, n: (ctx0 + b, col))

    attn = pl.pallas_call(
        _attn_kernel,
        grid=(BATCH, nb),
        in_specs=[
            smem,
            pl.BlockSpec((BLOCK, ATTN_WIDTH), lambda b, n: (b * nb + n, 0)),
            kv_spec(ck, -1), kv_spec(ck, 0), kv_spec(ck, 1),
            kv_spec(cv, -1), kv_spec(cv, 0), kv_spec(cv, 1),
            ctx_spec(ck), ctx_spec(cv),
        ],
        out_specs=pl.BlockSpec((BLOCK, ATTN_WIDTH), lambda b, n: (b * nb + n, 0)),
        out_shape=jax.ShapeDtypeStruct((T_ALL, ATTN_WIDTH), BF16),
        compiler_params=_cparams(("arbitrary", "arbitrary")),
        name="attn",
    )(sink, p_all, p_all, p_all, p_all, p_all, p_all, p_all, p_all, p_all)
    if not update_ctx:
        return attn
    return pl.pallas_call(
        _ctx_attn_kernel,
        grid=(BATCH,),
        in_specs=[
            smem,
            pl.BlockSpec((CTX_LEN, ATTN_WIDTH), lambda b: (ctx0 + b, 0)),
            pl.BlockSpec((CTX_LEN, kvw), lambda b: (ctx0 + b, ck)),
            pl.BlockSpec((CTX_LEN, kvw), lambda b: (ctx0 + b, cv)),
            pl.BlockSpec(memory_space=pl.ANY),
        ],
        out_specs=pl.BlockSpec((CTX_LEN, ATTN_WIDTH), lambda b: (ctx0 + b, 0)),
        out_shape=jax.ShapeDtypeStruct((T_ALL, ATTN_WIDTH), BF16),
        input_output_aliases={4: 0},
        compiler_params=_cparams(("arbitrary",)),
        name="ctx_attn",
    )(sink, p_all, p_all, p_all, attn)


def _fwd_chunk(b, s):
    ctx = T_LAT // CHUNK + b * NCHUNK_CTX + s
    lat = b * NCHUNK_LAT + (s - NCHUNK_CTX)
    return jnp.where(s < NCHUNK_CTX, ctx, lat)


def _bwd_chunk(b, s):
    ctx = T_LAT // CHUNK + b * NCHUNK_CTX + (NCHUNK_CTX - 1 - s)
    lat = b * NCHUNK_LAT + (NSTEP - 1 - s)
    return jnp.where(s < NCHUNK_CTX, ctx, lat)


def _ssd_kernel(xs_f, bm_f, cm_f, dt_f, xs_b, bm_b, cm_b, dt_b,
                bias_row, alog_row, bias_col, alog_col, yf_ref, yb_ref, h_scr):
    s = pl.program_id(1)

    @pl.when(s == 0)
    def _():
        h_scr[...] = jnp.zeros_like(h_scr)

    ii = lax.broadcasted_iota(jnp.int32, (CHUNK, CHUNK), 0)
    jj = lax.broadcasted_iota(jnp.int32, (CHUNK, CHUNK), 1)
    lane_lo = jj < SSD_P
    lower = jj <= ii
    upper = jj >= ii
    hg = SSD_HEADS // SSD_GROUPS
    dirs = ((xs_f, bm_f, cm_f, dt_f, yf_ref), (xs_b, bm_b, cm_b, dt_b, yb_ref))
    causal = (lower, upper)
    grp = lambda g: slice(g * SSD_STATE, (g + 1) * SSD_STATE)
    pair_lanes = lambda pair: slice(pair * 2 * SSD_P, (pair + 1) * 2 * SSD_P)

    acs, acs_t, row_t, dte_t, tot = {}, {}, {}, {}, {}
    for d in range(2):
        dt_ref = dirs[d][3]
        tri = jnp.where(causal[d], 1.0, 0.0).astype(BF16)
        tri_t = jnp.where(causal[1 - d], 1.0, 0.0).astype(BF16)
        last = CHUNK - 1 if d == 0 else 0
        r0 = d * SSD_HEADS
        dt_raw = dt_ref[...]
        dt_c = _softplus(dt_raw + bias_row[...])
        adt_c = dt_c * (-LOG2E * jnp.exp(alog_row[...]))
        acs[d] = sum(_dot(tri, p) for p in _split3(adt_c))
        dt_t = _softplus(dt_raw.T[r0:r0 + SSD_HEADS, :] + bias_col[r0:r0 + SSD_HEADS, :])
        adt_t = dt_t * (-LOG2E * jnp.exp(alog_col[r0:r0 + SSD_HEADS, :]))
        acs_t[d] = sum(_dot(p, tri_t) for p in _split3(adt_t))
        row_t[d] = acs_t[d] - jnp.log2(dt_t)
        dte_t[d] = jnp.exp2(acs_t[d][:, last:last + 1] - acs_t[d]) * dt_t
        tot[d] = acs[d][last:last + 1, :]

    cb, bm_t, cm = {}, {}, {}
    for d in range(2):
        _, bm_ref, cm_ref, _, _ = dirs[d]
        for g in range(SSD_GROUPS):
            cm[d, g] = cm_ref[:, grp(g)]
            cb[d, g] = _dot_nt(cm[d, g], bm_ref[:, grp(g)]).astype(BF16)
            bm_t[d, g] = bm_ref[:, grp(g)].astype(F32).T

    for d in range(2):
        xs_ref, y_ref = dirs[d][0], dirs[d][4]
        r0 = d * SSD_HEADS
        y_pairs, h_pairs = [], []
        for pair in range(SSD_HEADS // 2):
            g = pair // (hg // 2)
            x_pair = xs_ref[:, pair_lanes(pair)]
            h_pair = h_scr[d, :, pair_lanes(pair)]
            rhs = jnp.concatenate([x_pair, h_pair.astype(BF16)], axis=0)
            ys, ups, cds = [], [], []
            for h in (2 * pair, 2 * pair + 1):
                col = jnp.broadcast_to(acs[d][:, r0 + h:r0 + h + 1], (CHUNK, CHUNK))
                dec = jnp.exp2(jnp.where(causal[d], col - row_t[d][h:h + 1, :], NEG_INF))
                m_intra = cb[d, g] * dec.astype(BF16)
                m_state = cm[d, g] * jnp.exp2(col).astype(BF16)
                ys.append(_dot(jnp.concatenate([m_intra, m_state], axis=1), rhs))
                ups.append(_dot((bm_t[d, g] * dte_t[d][h:h + 1, :]).astype(BF16), x_pair))
                cds.append(jnp.exp2(tot[d][:, r0 + h:r0 + h + 1]))
            y_pairs.append(jnp.where(lane_lo, ys[0], ys[1]))
            h_pairs.append(jnp.where(lane_lo, cds[0], cds[1]) * h_pair
                           + jnp.where(lane_lo, ups[0], ups[1]))
        y_ref[...] = jnp.concatenate(y_pairs, axis=1)
        h_scr[d] = jnp.concatenate(h_pairs, axis=1)


def _ssd_call(xbc, dt_all, bias_row, alog_row, bias_col, alog_col):
    bcol = SSD_INNER // (SSD_GROUPS * SSD_STATE)

    def specs(chunk_fn):
        return [
            pl.BlockSpec((CHUNK, SSD_INNER), lambda b, s: (chunk_fn(b, s), 0)),
            pl.BlockSpec((CHUNK, SSD_GROUPS * SSD_STATE), lambda b, s: (chunk_fn(b, s), bcol)),
            pl.BlockSpec((CHUNK, SSD_GROUPS * SSD_STATE), lambda b, s: (chunk_fn(b, s), bcol + 1)),
            pl.BlockSpec((CHUNK, DT_PAD), lambda b, s: (chunk_fn(b, s), 0)),
        ]

    const = lambda shape: pl.BlockSpec(shape, lambda b, s: (0, 0))
    return pl.pallas_call(
        _ssd_kernel,
        grid=(BATCH, NSTEP),
        in_specs=specs(_fwd_chunk) + specs(_bwd_chunk) + [
            const((1, DT_PAD)), const((1, DT_PAD)), const((DT_PAD, 1)), const((DT_PAD, 1))],
        out_specs=[
            pl.BlockSpec((CHUNK, SSD_INNER), lambda b, s: (_fwd_chunk(b, s), 0)),
            pl.BlockSpec((CHUNK, SSD_INNER), lambda b, s: (_bwd_chunk(b, s), 0)),
        ],
        out_shape=[jax.ShapeDtypeStruct((T_ALL, SSD_INNER), F32)] * 2,
        scratch_shapes=[pltpu.VMEM((2, SSD_STATE, SSD_INNER), F32)],
        compiler_params=_cparams(("arbitrary", "arbitrary")),
        name="ssd",
    )(xbc, xbc, xbc, dt_all, xbc, xbc, xbc, dt_all, bias_row, alog_row, bias_col, alog_col)


def _ret_kernel(lg_ref, q_f, k_f, v_f, q_b, k_b, v_b, of_ref, ob_ref, s_scr, tab_scr):
    s = pl.program_id(1)

    @pl.when(s == 0)
    def _():
        s_scr[...] = jnp.zeros_like(s_scr)
        ii = lax.broadcasted_iota(jnp.int32, (CHUNK, CHUNK), 0)
        jj = lax.broadcasted_iota(jnp.int32, (CHUNK, CHUNK), 1)
        for d in range(2):
            if d == 0:
                dist = (ii - jj).astype(F32)
                row_pow = (ii + 1).astype(F32)
                key_pow = (CHUNK - 1 - ii).astype(F32)
            else:
                dist = (jj - ii).astype(F32)
                row_pow = (CHUNK - ii).astype(F32)
                key_pow = ii.astype(F32)
            for h in range(RET_HEADS):
                lg = lg_ref[d, h]
                tab_scr[d, h, 0] = jnp.where(dist >= 0.0, jnp.exp(jnp.maximum(dist, 0.0) * lg), 0.0)
                tab_scr[d, h, 1] = jnp.exp(row_pow * lg)
                tab_scr[d, h, 2] = jnp.exp(key_pow * lg)

    dirs = ((q_f, k_f, v_f, of_ref), (q_b, k_b, v_b, ob_ref))
    heads = [(d, h) for d in range(2) for h in range(RET_HEADS)]
    lanes = lambda h: slice(h * HEAD_DIM, (h + 1) * HEAD_DIM)
    raw = {}
    for d, h in heads:
        q_ref, k_ref, v_ref, _ = dirs[d]
        raw[d, h] = _dot_nt(q_ref[:, lanes(h)], k_ref[:, lanes(h)])
    kv = {}
    for d, h in heads:
        _, k_ref, v_ref, _ = dirs[d]
        k_dec = (k_ref[:, lanes(h)].astype(F32) * tab_scr[d, h, 2]).astype(BF16)
        kv[d, h] = _dot_tn(k_dec, v_ref[:, lanes(h)])
    outs = {}
    for d, h in heads:
        q_ref, _, v_ref, _ = dirs[d]
        scores = raw[d, h] * tab_scr[d, h, 0]
        q_cross = q_ref[:, lanes(h)].astype(F32) * tab_scr[d, h, 1]
        state = s_scr[d, h * HEAD_DIM:(h + 1) * HEAD_DIM, :]
        lhs = jnp.concatenate([scores.astype(BF16), q_cross.astype(BF16)], axis=1)
        rhs = jnp.concatenate([v_ref[:, lanes(h)], state.astype(BF16)], axis=0)
        outs[d, h] = _dot(lhs, rhs)
    for d in range(2):
        states = []
        for h in range(RET_HEADS):
            chunk_decay = jnp.exp(jnp.full((1, HEAD_DIM), float(CHUNK), F32) * lg_ref[d, h])
            states.append(chunk_decay * s_scr[d, h * HEAD_DIM:(h + 1) * HEAD_DIM, :] + kv[d, h])
        dirs[d][3][...] = jnp.concatenate([outs[d, h] for h in range(RET_HEADS)], axis=1)
        s_scr[d] = jnp.concatenate(states, axis=0)


def _ret_call(p_all, log_decay):
    def specs(chunk_fn):
        return [
            pl.BlockSpec((CHUNK, RET_WIDTH), lambda b, s: (chunk_fn(b, s), COL_RQ // RET_WIDTH)),
            pl.BlockSpec((CHUNK, RET_WIDTH), lambda b, s: (chunk_fn(b, s), COL_RK // RET_WIDTH)),
            pl.BlockSpec((CHUNK, RET_WIDTH), lambda b, s: (chunk_fn(b, s), COL_RV // RET_WIDTH)),
        ]

    return pl.pallas_call(
        _ret_kernel,
        grid=(BATCH, NSTEP),
        in_specs=[pl.BlockSpec(memory_space=pltpu.SMEM)] + specs(_fwd_chunk) + specs(_bwd_chunk),
        out_specs=[
            pl.BlockSpec((CHUNK, RET_WIDTH), lambda b, s: (_fwd_chunk(b, s), 0)),
            pl.BlockSpec((CHUNK, RET_WIDTH), lambda b, s: (_bwd_chunk(b, s), 0)),
        ],
        out_shape=[jax.ShapeDtypeStruct((T_ALL, RET_WIDTH), F32)] * 2,
        scratch_shapes=[pltpu.VMEM((2, RET_HEADS * HEAD_DIM, HEAD_DIM), F32),
                        pltpu.VMEM((2, RET_HEADS, 3, CHUNK, CHUNK), F32)],
        compiler_params=_cparams(("arbitrary", "arbitrary")),
        name="retention",
    )(log_decay, p_all, p_all, p_all, p_all, p_all, p_all)


def _merge_kernel(attn_ref, yf_ref, yb_ref, xs_ref, z_ref, of_ref, ob_ref, rg_ref,
                  ga_ref, gs_ref, gr_ref, dskip_ref, ssd_g_ref, ret_g_ref,
                  wa_ref, ws_ref, wr_ref, o_ref):
    y = yf_ref[...] + yb_ref[...] + dskip_ref[...] * xs_ref[...].astype(F32)
    y = y * _silu(z_ref[...].astype(F32))
    ssd_o = y * lax.rsqrt(jnp.mean(y * y, axis=-1, keepdims=True) + LN_EPS) * ssd_g_ref[...]

    o = of_ref[...] + ob_ref[...]
    normed = jnp.concatenate(
        [_ln_rows(o[:, h * HEAD_DIM:(h + 1) * HEAD_DIM]) for h in range(RET_HEADS)], axis=1)
    ret_o = normed * ret_g_ref[...] * _silu(rg_ref[...].astype(F32))

    merged = (_sigmoid(ga_ref[...].astype(F32)) * _dot(attn_ref[...], wa_ref[...])
              + _sigmoid(gs_ref[...].astype(F32)) * _dot(ssd_o.astype(BF16), ws_ref[...])
              + _sigmoid(gr_ref[...].astype(F32)) * _dot(ret_o.astype(BF16), wr_ref[...]))
    o_ref[...] = merged.astype(BF16)


def _merge_call(n_rows, attn, yf, yb, xbc, p_all, of, ob, dskip, ssd_g, ret_g, wa, ws, wr):
    tm = TM_MERGE
    w1024 = lambda c: pl.BlockSpec((tm, 1024), lambda i: (i, c))
    gate = lambda c: pl.BlockSpec((tm, D_MODEL), lambda i: (i, COL_GATES // D_MODEL + c))
    vec = pl.BlockSpec((1, 1024), lambda i: (0, 0))
    wspec = pl.BlockSpec((1024, D_MODEL), lambda i: (0, 0), pipeline_mode=pl.Buffered(1))
    return pl.pallas_call(
        _merge_kernel,
        grid=(n_rows // tm,),
        in_specs=[
            w1024(0), w1024(0), w1024(0), w1024(0), w1024(COL_Z // 1024),
            w1024(0), w1024(0), w1024(COL_RG // 1024),
            gate(0), gate(1), gate(2), vec, vec, vec, wspec, wspec, wspec,
        ],
        out_specs=pl.BlockSpec((tm, D_MODEL), lambda i: (i, 0)),
        out_shape=jax.ShapeDtypeStruct((n_rows, D_MODEL), BF16),
        compiler_params=_cparams(("arbitrary",)),
        name="merge",
    )(attn, yf, yb, xbc, p_all, of, ob, p_all, p_all, p_all, p_all, dskip, ssd_g, ret_g, wa, ws, wr)


def _deepnorm(x, y, gate, g, b):
    return _ln_rows(DEEPNORM_ALPHA * x + gate * y) * g + b


def _outproj_kernel(m_ref, x_ref, w_ref, gate_ref, g_ref, b_ref, o_ref):
    mix = _dot(m_ref[...], w_ref[...])
    o_ref[...] = _deepnorm(x_ref[...], mix, gate_ref[...], g_ref[...], b_ref[...])


def _outproj_call(n_rows, merged, x_all, w_out, mod, ln_g, ln_b):
    tm = TM_OUT
    vec = pl.BlockSpec((1, D_MODEL), lambda i: (0, 0))
    return pl.pallas_call(
        _outproj_kernel,
        grid=(n_rows // tm,),
        in_specs=[
            pl.BlockSpec((tm, D_MODEL), lambda i: (i, 0)),
            pl.BlockSpec((tm, D_MODEL), lambda i: (i, 0)),
            pl.BlockSpec((D_MODEL, D_MODEL), lambda i: (0, 0), pipeline_mode=pl.Buffered(1)),
            pl.BlockSpec((None, 1, D_MODEL), lambda i: (_batch_of_tile(i, tm) * 6 + 2, 0, 0)),
            vec, vec,
        ],
        out_specs=pl.BlockSpec((tm, D_MODEL), lambda i: (i, 0)),
        out_shape=jax.ShapeDtypeStruct((n_rows, D_MODEL), F32),
        compiler_params=_cparams(("arbitrary",)),
        name="outproj",
    )(merged, x_all, w_out, mod, ln_g, ln_b)


def _mlp_kernel(x_ref, shift_ref, scale_ref, gate_ref, wup_ref, wdn_ref, g_ref, b_ref,
                o_ref, h_scr, acc_scr):
    j = pl.program_id(1)

    @pl.when(j == 0)
    def _():
        h = _ln_rows(x_ref[...]) * (1.0 + scale_ref[...]) + shift_ref[...]
        h_scr[...] = h.astype(BF16)
        acc_scr[...] = jnp.zeros_like(acc_scr)

    u = jnp.maximum(_dot(h_scr[...], wup_ref[...]), 0.0)
    acc_scr[...] += _dot((u * u).astype(BF16), wdn_ref[...])

    @pl.when(j == pl.num_programs(1) - 1)
    def _():
        o_ref[...] = _deepnorm(x_ref[...], acc_scr[...], gate_ref[...], g_ref[...], b_ref[...])


def _mlp_call(n_rows, x1, mod, w_up, w_down, ln_g, ln_b):
    tm, tf = TM_MLP, TF_MLP

    def mod_map(k):
        return lambda i, j: (_batch_of_tile(i, tm) * 6 + k, 0, 0)

    vec = pl.BlockSpec((1, D_MODEL), lambda i, j: (0, 0))
    return pl.pallas_call(
        _mlp_kernel,
        grid=(n_rows // tm, D_FF // tf),
        in_specs=[
            pl.BlockSpec((tm, D_MODEL), lambda i, j: (i, 0)),
            pl.BlockSpec((None, 1, D_MODEL), mod_map(3)),
            pl.BlockSpec((None, 1, D_MODEL), mod_map(4)),
            pl.BlockSpec((None, 1, D_MODEL), mod_map(5)),
            pl.BlockSpec((D_MODEL, tf), lambda i, j: (0, j)),
            pl.BlockSpec((tf, D_MODEL), lambda i, j: (j, 0)),
            vec, vec,
        ],
        out_specs=pl.BlockSpec((tm, D_MODEL), lambda i, j: (i, 0)),
        out_shape=jax.ShapeDtypeStruct((n_rows, D_MODEL), F32),
        scratch_shapes=[pltpu.VMEM((tm, D_MODEL), BF16), pltpu.VMEM((tm, D_MODEL), F32)],
        compiler_params=_cparams(("arbitrary", "arbitrary")),
        name="mlp",
    )(x1, mod, mod, mod, w_up, w_down, ln_g, ln_b)


def _rope_tables():
    rows = SEQ // GRID_W
    row = jnp.repeat(jnp.arange(rows), GRID_W).astype(F32)
    col = (jnp.arange(rows * GRID_W) % GRID_W).astype(F32)
    n_freq = HEAD_DIM // 4
    inv = ROPE_BASE ** (-jnp.arange(n_freq, dtype=F32) / n_freq)
    ang = jnp.concatenate([row[:, None] * inv, col[:, None] * inv], axis=-1)
    cos, sin = jnp.cos(ang), jnp.sin(ang)
    cos2 = jnp.concatenate([cos, cos], axis=-1)
    sin2 = jnp.concatenate([-sin, sin], axis=-1)
    cos2 = jnp.concatenate([cos2, jnp.ones((TM_IN, HEAD_DIM), F32)], axis=0)
    sin2 = jnp.concatenate([sin2, jnp.zeros((TM_IN, HEAD_DIM), F32)], axis=0)
    return cos2, sin2


def _permute_w_in(w):
    sizes = (ATTN_WIDTH, ATTN_HKV * HEAD_DIM, ATTN_HKV * HEAD_DIM, SSD_INNER, SSD_CONV_CH,
             2 * SSD_HEADS, RET_WIDTH, RET_WIDTH, RET_WIDTH, RET_WIDTH, 3 * D_MODEL)
    offs = [0]
    for sz in sizes:
        offs.append(offs[-1] + sz)
    aq, ak, av, z, xbc, dt, rq, rk, rv, rg, gates = [w[:, offs[i]:offs[i + 1]] for i in range(11)]
    w_perm = jnp.concatenate([aq, rq, rk, rv, z, rg, ak, av, xbc, gates], axis=1).astype(BF16)
    w_dt = jnp.pad(dt, ((0, 0), (0, DT_PAD - 2 * SSD_HEADS))).astype(BF16)
    return w_perm, w_dt


def _pad_heads(v):
    flat = jnp.pad(v.reshape(-1).astype(F32), (0, DT_PAD - 2 * SSD_HEADS))
    return flat.reshape(1, DT_PAD), flat.reshape(DT_PAD, 1)


def kernel(x, c, ctx, c_ctx, ada_w, ada_b, w_in, attn_sink, ssd_conv_w, ssd_conv_b, ssd_a_log,
           ssd_dt_bias, ssd_d, ssd_norm_g, ret_log_decay, ret_norm_g, w_branch_attn, w_branch_ssd,
           w_branch_ret, w_out, ln1_g, ln1_b, w_mlp_up, w_mlp_down, ln2_g, ln2_b):
    x_all = jnp.concatenate([x.reshape(T_LAT, D_MODEL), ctx.reshape(T_CTX, D_MODEL)], axis=0)
    cond = jnp.concatenate([c, c_ctx[None, :], jnp.zeros((8 - BATCH - 1, D_MODEL), F32)], axis=0)
    mod_all = _ada_call(cond, ada_w, ada_b).reshape(DEPTH, 8 * 6, 1, D_MODEL)
    cos2, sin2 = _rope_tables()

    for l in range(DEPTH):
        update_ctx = l < DEPTH - 1
        n_rows = T_ALL if update_ctx else T_LAT
        mod = mod_all[l]
        w_perm, w_dt = _permute_w_in(w_in[l])
        p_all, dt_all = _inproj_call(x_all, mod, w_perm, w_dt, cos2, sin2)
        xbc = _conv_call(p_all, ssd_conv_w[l], ssd_conv_b[l])

        attn = _attn_calls(p_all, attn_sink[l].astype(F32), update_ctx)

        bias_row, bias_col = _pad_heads(ssd_dt_bias[l])
        alog_row, alog_col = _pad_heads(ssd_a_log[l])
        yf, yb = _ssd_call(xbc, dt_all, bias_row, alog_row, bias_col, alog_col)
        of, ob = _ret_call(p_all, ret_log_decay[l].astype(F32))

        dskip = jnp.repeat(ssd_d[l].astype(F32), SSD_P).reshape(1, SSD_INNER)
        merged = _merge_call(
            n_rows, attn, yf, yb, xbc, p_all, of, ob, dskip,
            ssd_norm_g[l].reshape(1, SSD_INNER), ret_norm_g[l].reshape(1, RET_WIDTH),
            w_branch_attn[l].astype(BF16), w_branch_ssd[l].astype(BF16), w_branch_ret[l].astype(BF16))
        x1 = _outproj_call(n_rows, merged, x_all, w_out[l].astype(BF16), mod,
                           ln1_g[l].reshape(1, D_MODEL), ln1_b[l].reshape(1, D_MODEL))
        x_all = _mlp_call(n_rows, x1, mod, w_mlp_up[l].astype(BF16), w_mlp_down[l].astype(BF16),
                          ln2_g[l].reshape(1, D_MODEL), ln2_b[l].reshape(1, D_MODEL))
    return x_all.reshape(BATCH, SEQ, D_MODEL)
```

```python
import functools

import jax
import jax.numpy as jnp
from jax import lax
from jax.experimental import pallas as pl
from jax.experimental.pallas import tpu as pltpu
import numpy as np

F32 = jnp.float32
BF16 = jnp.bfloat16

D_MODEL = 2048
BATCH = 4
SEQ = 4096
DEPTH = 2
GRID_W = 64
CTX_LEN = 256
BLOCK = 128
HEAD_DIM = 128
ATTN_HQ = 8
ATTN_HKV = 2
ATTN_GROUP = ATTN_HQ // ATTN_HKV
ATTN_WIDTH = ATTN_HQ * HEAD_DIM
SSD_HEADS = 16
SSD_P = 64
SSD_INNER = SSD_HEADS * SSD_P
SSD_GROUPS = 2
SSD_STATE = 128
SSD_CONV = 5
SSD_CONV_CH = SSD_INNER + 2 * SSD_GROUPS * SSD_STATE
CHUNK = 128
RET_HEADS = 8
RET_WIDTH = RET_HEADS * HEAD_DIM
D_FF = 4 * D_MODEL
ROPE_BASE = 10000.0
DEEPNORM_ALPHA = (2 * DEPTH) ** 0.25
LN_EPS = 1e-6
NEG_INF = -1e30
Q_SCALE = HEAD_DIM ** -0.5
LOG2E = 1.4426950408889634

T_LAT = BATCH * SEQ
T_CTX = BATCH * CTX_LEN
T_ALL = T_LAT + T_CTX
NCHUNK_LAT = SEQ // CHUNK
NCHUNK_CTX = CTX_LEN // CHUNK
NSTEP = NCHUNK_LAT + NCHUNK_CTX

COL_AQ = 0
COL_AK = 1024
COL_AV = 1280
COL_Z = 1536
COL_XBC = 2560
COL_DT = 4096
COL_RQ = 4096
COL_RK = 5120
COL_RV = 6144
COL_RG = 7168
COL_GATES = 8192
P_COLS = 14336
DT_COLS = 2 * SSD_HEADS
DT_PAD = 128

TM_IN = 1024
TN_IN = 1024
J_AQ, J_AKAV, J_RQ, J_RK = COL_AQ // TN_IN, COL_AK // TN_IN, COL_RQ // TN_IN, COL_RK // TN_IN
J_SPLIT = COL_DT // TN_IN
J_RG, J_GATES = COL_RG // TN_IN, COL_GATES // TN_IN
TM_MERGE = 256
TM_OUT = 1024
RC_OUT = 256
TM_MLP = 512
TF_MLP = 1024
RC_MLP = 256
TM_CONV = 1024
HALO = 16

VMEM_LIMIT = 56 * 1024 * 1024


def _cparams(sem):
    return pltpu.CompilerParams(dimension_semantics=sem, vmem_limit_bytes=VMEM_LIMIT)


def _sigmoid(x):
    return 1.0 / (1.0 + jnp.exp(-x))


def _silu(x):
    return x * _sigmoid(x)


def _softplus(x):
    return jnp.maximum(x, 0.0) + jnp.log1p(jnp.exp(-jnp.abs(x)))


def _ln_rows(x):
    mu = jnp.mean(x, axis=-1, keepdims=True)
    xc = x - mu
    var = jnp.mean(xc * xc, axis=-1, keepdims=True)
    return xc * lax.rsqrt(var + LN_EPS)


def _dot(a, b):
    return jnp.dot(a, b, preferred_element_type=F32)


def _dot_nt(a, b):
    return lax.dot_general(a, b, (((1,), (1,)), ((), ())), preferred_element_type=F32)


def _dot_tn(a, b):
    return lax.dot_general(a, b, (((0,), (0,)), ((), ())), preferred_element_type=F32)


def _split3(x):
    x1 = x.astype(BF16)
    r1 = x - x1.astype(F32)
    x2 = r1.astype(BF16)
    r2 = r1 - x2.astype(F32)
    return x1, x2, r2.astype(BF16)


def _batch_of_tile(i, tm):
    return jnp.where(i < T_LAT // tm, i // (SEQ // tm), BATCH)


TN_ADA = 2048


def _ada_kernel(cond_ref, w_ref, b_ref, o_ref):
    c = cond_ref[...]
    s = _silu(c).astype(BF16)
    o_ref[...] = _dot(s, w_ref[...].astype(BF16)) + b_ref[...]


def _ada_call(cond, ada_w, ada_b):
    n = ada_w.shape[-1]
    return pl.pallas_call(
        _ada_kernel,
        grid=(DEPTH, n // TN_ADA),
        in_specs=[
            pl.BlockSpec((8, D_MODEL), lambda l, j: (0, 0)),
            pl.BlockSpec((None, D_MODEL, TN_ADA), lambda l, j: (l, 0, j)),
            pl.BlockSpec((None, 1, TN_ADA), lambda l, j: (l, 0, j)),
        ],
        out_specs=pl.BlockSpec((None, 8, TN_ADA), lambda l, j: (l, 0, j)),
        out_shape=jax.ShapeDtypeStruct((DEPTH, 8, n), F32),
        compiler_params=_cparams(("arbitrary", "arbitrary")),
        name="ada",
    )(cond, ada_w, ada_b.reshape(DEPTH, 1, n))


def _rope_store(acc, cos, sin, o_ref, n_heads, scale):
    for h in range(n_heads):
        xs = acc[:, h * HEAD_DIM:(h + 1) * HEAD_DIM]
        rot = pltpu.roll(xs, HEAD_DIM // 2, axis=1)
        o_ref[:, h * HEAD_DIM:(h + 1) * HEAD_DIM] = ((xs * cos + rot * sin) * scale).astype(BF16)


def _inproj_kernel(skip_ctx_cols, x_ref, shift_ref, scale_ref, wa_ref, wb_ref, wdt_ref, cos_ref, sin_ref,
                   p_ref, dt_ref, h_scr):
    i = pl.program_id(0)
    j = pl.program_id(1)

    @pl.when(j == 0)
    def _():
        rc = 256
        for r in range(0, TM_IN, rc):
            h = _ln_rows(x_ref[r:r + rc, :]) * (1.0 + scale_ref[...]) + shift_ref[...]
            h_scr[r:r + rc, :] = h.astype(BF16)
        dt_ref[...] = _dot(h_scr[...], wdt_ref[...])

    heads_per_tile = TN_IN // HEAD_DIM
    if skip_ctx_cols:
        unused = (j == J_AQ) | (j == J_RG) | (j >= J_GATES)
        active = jnp.logical_not((i == T_ALL // TM_IN - 1) & unused)
    else:
        active = True

    def rotated(w_ref, scale):
        acc = _dot(h_scr[...], w_ref[...])
        _rope_store(acc, cos_ref[...], sin_ref[...], p_ref, heads_per_tile, scale)

    def plain(w_ref):
        p_ref[...] = _dot(h_scr[...], w_ref[...]).astype(BF16)

    pl.when((j == J_AQ) & active)(lambda: rotated(wa_ref, Q_SCALE * LOG2E))

    @pl.when(j == J_AKAV)
    def _():
        acc = _dot(h_scr[...], wa_ref[...])
        _rope_store(acc, cos_ref[...], sin_ref[...], p_ref, ATTN_HKV, 1.0)
        p_ref[:, ATTN_HKV * HEAD_DIM:] = acc[:, ATTN_HKV * HEAD_DIM:].astype(BF16)

    pl.when((j > J_AKAV) & (j < J_SPLIT))(lambda: plain(wa_ref))
    pl.when(j == J_RQ)(lambda: rotated(wb_ref, Q_SCALE))
    pl.when(j == J_RK)(lambda: rotated(wb_ref, 1.0))
    pl.when((j > J_RK) & active)(lambda: plain(wb_ref))


def _inproj_call(x_all, mod, w_pre, w_post, w_dt, cos2, sin2, skip_ctx_cols):
    n_tiles = T_ALL // TM_IN
    lat_tiles = T_LAT // TM_IN

    def mod_map(k):
        return lambda i, j: (_batch_of_tile(i, TM_IN) * 6 + k, 0, 0)

    def rope_map(i, j):
        return (jnp.where(i < lat_tiles, i % (SEQ // TM_IN), SEQ // TM_IN), 0)

    return pl.pallas_call(
        functools.partial(_inproj_kernel, skip_ctx_cols),
        grid=(n_tiles, P_COLS // TN_IN),
        in_specs=[
            pl.BlockSpec((TM_IN, D_MODEL), lambda i, j: (i, 0)),
            pl.BlockSpec((None, 1, D_MODEL), mod_map(0)),
            pl.BlockSpec((None, 1, D_MODEL), mod_map(1)),
            pl.BlockSpec((D_MODEL, TN_IN), lambda i, j: (0, jnp.minimum(j, J_SPLIT - 1))),
            pl.BlockSpec((D_MODEL, TN_IN), lambda i, j: (0, jnp.maximum(j, J_SPLIT))),
            pl.BlockSpec((D_MODEL, DT_PAD), lambda i, j: (0, 0)),
            pl.BlockSpec((TM_IN, HEAD_DIM), rope_map),
            pl.BlockSpec((TM_IN, HEAD_DIM), rope_map),
        ],
        out_specs=[
            pl.BlockSpec((TM_IN, TN_IN), lambda i, j: (i, j)),
            pl.BlockSpec((TM_IN, DT_PAD), lambda i, j: (i, 0)),
        ],
        out_shape=[
            jax.ShapeDtypeStruct((T_ALL, P_COLS), BF16),
            jax.ShapeDtypeStruct((T_ALL, DT_PAD), F32),
        ],
        scratch_shapes=[pltpu.VMEM((TM_IN, D_MODEL), BF16)],
        compiler_params=_cparams(("arbitrary", "arbitrary")),
        name="inproj",
    )(x_all, mod, mod, w_pre, w_post, w_dt, cos2, sin2)


TN_CONV = 512


def _conv_taps(prev, x, nxt, w_ref, b_ref, pos, slen):
    xe = jnp.concatenate([prev, x, nxt], axis=0)
    w = w_ref[...]
    acc = jnp.zeros((TM_CONV, TN_CONV), F32) + b_ref[...]
    half = SSD_CONV // 2
    for k in range(SSD_CONV):
        off = HALO - half + k
        xk = xe[off:off + TM_CONV, :]
        if pos is not None and k < half:
            xk = jnp.where(pos + (k - half) >= 0, xk, 0.0)
        elif pos is not None and k > half:
            xk = jnp.where(pos + (k - half) < slen, xk, 0.0)
        acc = acc + xk * w[k:k + 1, :]
    return _silu(acc).astype(BF16)


def _conv_kernel(x_ref, prev_ref, next_ref, w_ref, b_ref, o_ref):
    i = pl.program_id(0)
    tiles_per_seq = SEQ // TM_CONV
    x = x_ref[...].astype(F32)
    prev = prev_ref[...].astype(F32)
    nxt = next_ref[...].astype(F32)

    @pl.when(i < T_LAT // TM_CONV)
    def _():
        t = i & (tiles_per_seq - 1)
        o_ref[...] = _conv_taps(jnp.where(t == 0, 0.0, prev), x,
                                jnp.where(t == tiles_per_seq - 1, 0.0, nxt), w_ref, b_ref, None, None)

    @pl.when(i >= T_LAT // TM_CONV)
    def _():
        row = lax.broadcasted_iota(jnp.int32, (TM_CONV, TN_CONV), 0)
        o_ref[...] = _conv_taps(prev, x, nxt, w_ref, b_ref, row & (CTX_LEN - 1), CTX_LEN)


def _conv_call(p_all, conv_w, conv_b):
    cb0 = COL_XBC // TN_CONV
    hpt = TM_CONV // HALO
    last_halo = T_ALL // HALO - 1
    return pl.pallas_call(
        _conv_kernel,
        grid=(T_ALL // TM_CONV, SSD_CONV_CH // TN_CONV),
        in_specs=[
            pl.BlockSpec((TM_CONV, TN_CONV), lambda i, j: (i, cb0 + j)),
            pl.BlockSpec((HALO, TN_CONV), lambda i, j: (jnp.maximum(i * hpt - 1, 0), cb0 + j)),
            pl.BlockSpec((HALO, TN_CONV), lambda i, j: (jnp.minimum((i + 1) * hpt, last_halo), cb0 + j)),
            pl.BlockSpec((SSD_CONV, TN_CONV), lambda i, j: (0, j)),
            pl.BlockSpec((1, TN_CONV), lambda i, j: (0, j)),
        ],
        out_specs=pl.BlockSpec((TM_CONV, TN_CONV), lambda i, j: (i, j)),
        out_shape=jax.ShapeDtypeStruct((T_ALL, SSD_CONV_CH), BF16),
        compiler_params=_cparams(("arbitrary", "arbitrary")),
        name="conv",
    )(p_all, p_all, p_all, conv_w, conv_b.reshape(1, SSD_CONV_CH))


def _softmax_pv(s, vals, sink_col):
    m = jnp.maximum(jnp.max(s, axis=-1, keepdims=True), sink_col)
    p = jnp.exp2(s - m)
    l = jnp.sum(p, axis=-1, keepdims=True) + jnp.exp2(sink_col - m)
    return _dot(p.astype(BF16), vals) / l


def _group_queries(q_ref, hkv):
    h0 = hkv * ATTN_GROUP
    return jnp.concatenate(
        [q_ref[:, (h0 + g) * HEAD_DIM:(h0 + g + 1) * HEAD_DIM] for g in range(ATTN_GROUP)], axis=0)


def _sink_column(sink_ref, hkv, nq):
    return jnp.concatenate(
        [jnp.full((nq, 1), sink_ref[hkv * ATTN_GROUP + g] * LOG2E, F32) for g in range(ATTN_GROUP)], axis=0)


def _attn_kernel(sink_ref, q_ref, kp_ref, kc_ref, kn_ref, vp_ref, vc_ref, vn_ref, kx_ref, vx_ref,
                 o_ref):
    n = pl.program_id(1)
    rows = ATTN_GROUP * BLOCK
    qi = lax.broadcasted_iota(jnp.int32, (rows, BLOCK), 0) & (BLOCK - 1)
    kj = lax.broadcasted_iota(jnp.int32, (rows, BLOCK), 1)
    prev_ok = (kj >= qi) & (n > 0)
    next_ok = (kj <= qi) & (n < SEQ // BLOCK - 1)

    def mask_fn(s):
        return jnp.concatenate([
            jnp.where(prev_ok, s[:, :BLOCK], NEG_INF), s[:, BLOCK:2 * BLOCK],
            jnp.where(next_ok, s[:, 2 * BLOCK:3 * BLOCK], NEG_INF), s[:, 3 * BLOCK:]], axis=1)

    kv_cols = lambda hkv: slice(hkv * HEAD_DIM, (hkv + 1) * HEAD_DIM)
    scores = []
    for hkv in range(ATTN_HKV):
        c = kv_cols(hkv)
        keys = jnp.concatenate([kp_ref[:, c], kc_ref[:, c], kn_ref[:, c], kx_ref[:, c]], axis=0)
        scores.append(mask_fn(_dot_nt(_group_queries(q_ref, hkv), keys)))
    outs = []
    for hkv in range(ATTN_HKV):
        c = kv_cols(hkv)
        vals = jnp.concatenate([vp_ref[:, c], vc_ref[:, c], vn_ref[:, c], vx_ref[:, c]], axis=0)
        o = _softmax_pv(scores[hkv], vals, _sink_column(sink_ref, hkv, BLOCK))
        outs += [o[g * BLOCK:(g + 1) * BLOCK, :] for g in range(ATTN_GROUP)]
    o_ref[...] = jnp.concatenate(outs, axis=1).astype(BF16)


def _ctx_attn_kernel(sink_ref, q_ref, kx_ref, vx_ref, prev_ref, o_ref):
    del prev_ref
    kv_cols = lambda hkv: slice(hkv * HEAD_DIM, (hkv + 1) * HEAD_DIM)
    scores = [_dot_nt(_group_queries(q_ref, hkv), kx_ref[:, kv_cols(hkv)]) for hkv in range(ATTN_HKV)]
    outs = []
    for hkv in range(ATTN_HKV):
        o = _softmax_pv(scores[hkv], vx_ref[:, kv_cols(hkv)], _sink_column(sink_ref, hkv, CTX_LEN))
        outs += [o[g * CTX_LEN:(g + 1) * CTX_LEN, :] for g in range(ATTN_GROUP)]
    o_ref[...] = jnp.concatenate(outs, axis=1).astype(BF16)


def _attn_calls(p_all, sink, update_ctx):
    nb = SEQ // BLOCK
    kvw = ATTN_HKV * HEAD_DIM
    ck = COL_AK // kvw
    cv = COL_AV // kvw
    ctx0 = T_LAT // CTX_LEN
    smem = pl.BlockSpec(memory_space=pltpu.SMEM)

    def kv_spec(col, dn):
        return pl.BlockSpec((BLOCK, kvw), lambda b, n: (b * nb + jnp.clip(n + dn, 0, nb - 1), col))

    def ctx_spec(col):
        return pl.BlockSpec((CTX_LEN, kvw), lambda b, n: (ctx0 + b, col))

    attn = pl.pallas_call(
        _attn_kernel,
        grid=(BATCH, nb),
        in_specs=[
            smem,
            pl.BlockSpec((BLOCK, ATTN_WIDTH), lambda b, n: (b * nb + n, 0)),
            kv_spec(ck, -1), kv_spec(ck, 0), kv_spec(ck, 1),
            kv_spec(cv, -1), kv_spec(cv, 0), kv_spec(cv, 1),
            ctx_spec(ck), ctx_spec(cv),
        ],
        out_specs=pl.BlockSpec((BLOCK, ATTN_WIDTH), lambda b, n: (b * nb + n, 0)),
        out_shape=jax.ShapeDtypeStruct((T_ALL, ATTN_WIDTH), BF16),
        compiler_params=_cparams(("arbitrary", "arbitrary")),
        name="attn",
    )(sink, p_all, p_all, p_all, p_all, p_all, p_all, p_all, p_all, p_all)
    if not update_ctx:
        return attn
    return pl.pallas_call(
        _ctx_attn_kernel,
        grid=(BATCH,),
        in_specs=[
            smem,
            pl.BlockSpec((CTX_LEN, ATTN_WIDTH), lambda b: (ctx0 + b, 0)),
            pl.BlockSpec((CTX_LEN, kvw), lambda b: (ctx0 + b, ck)),
            pl.BlockSpec((CTX_LEN, kvw), lambda b: (ctx0 + b, cv)),
            pl.BlockSpec(memory_space=pl.ANY),
        ],
        out_specs=pl.BlockSpec((CTX_LEN, ATTN_WIDTH), lambda b: (ctx0 + b, 0)),
        out_shape=jax.ShapeDtypeStruct((T_ALL, ATTN_WIDTH), BF16),
        input_output_aliases={4: 0},
        compiler_params=_cparams(("arbitrary",)),
        name="ctx_attn",
    )(sink, p_all, p_all, p_all, attn)


def _fwd_chunk(b, s):
    ctx = T_LAT // CHUNK + b * NCHUNK_CTX + s
    lat = b * NCHUNK_LAT + (s - NCHUNK_CTX)
    return jnp.where(s < NCHUNK_CTX, ctx, lat)


def _bwd_chunk(b, s):
    ctx = T_LAT // CHUNK + b * NCHUNK_CTX + (NCHUNK_CTX - 1 - s)
    lat = b * NCHUNK_LAT + (NSTEP - 1 - s)
    return jnp.where(s < NCHUNK_CTX, ctx, lat)


def _ssd_kernel(xs_f, bm_f, cm_f, dt_f, xs_b, bm_b, cm_b, dt_b,
                bias_row, alog_row, bias_col, alog_col, yf_ref, yb_ref, h_scr):
    s = pl.program_id(1)

    @pl.when(s == 0)
    def _():
        h_scr[...] = jnp.zeros_like(h_scr)

    ii = lax.broadcasted_iota(jnp.int32, (CHUNK, CHUNK), 0)
    jj = lax.broadcasted_iota(jnp.int32, (CHUNK, CHUNK), 1)
    lane_lo = jj < SSD_P
    lower = jj <= ii
    upper = jj >= ii
    hg = SSD_HEADS // SSD_GROUPS
    dirs = ((xs_f, bm_f, cm_f, dt_f, yf_ref), (xs_b, bm_b, cm_b, dt_b, yb_ref))
    causal = (lower, upper)
    grp = lambda g: slice(g * SSD_STATE, (g + 1) * SSD_STATE)
    pair_lanes = lambda pair: slice(pair * 2 * SSD_P, (pair + 1) * 2 * SSD_P)

    acs, acs_t, row_t, dte_t, tot = {}, {}, {}, {}, {}
    for d in range(2):
        dt_ref = dirs[d][3]
        tri = jnp.where(causal[d], 1.0, 0.0).astype(BF16)
        tri_t = jnp.where(causal[1 - d], 1.0, 0.0).astype(BF16)
        last = CHUNK - 1 if d == 0 else 0
        r0 = d * SSD_HEADS
        dt_raw = dt_ref[...]
        dt_c = _softplus(dt_raw + bias_row[...])
        adt_c = dt_c * (-LOG2E * jnp.exp(alog_row[...]))
        acs[d] = sum(_dot(tri, p) for p in _split3(adt_c))
        dt_t = _softplus(dt_raw.T[r0:r0 + SSD_HEADS, :] + bias_col[r0:r0 + SSD_HEADS, :])
        adt_t = dt_t * (-LOG2E * jnp.exp(alog_col[r0:r0 + SSD_HEADS, :]))
        acs_t[d] = sum(_dot(p, tri_t) for p in _split3(adt_t))
        row_t[d] = acs_t[d] - jnp.log2(dt_t)
        dte_t[d] = jnp.exp2(acs_t[d][:, last:last + 1] - acs_t[d]) * dt_t
        tot[d] = acs[d][last:last + 1, :]

    cb, bm_t, cm = {}, {}, {}
    for d in range(2):
        _, bm_ref, cm_ref, _, _ = dirs[d]
        for g in range(SSD_GROUPS):
            cm[d, g] = cm_ref[:, grp(g)]
            cb[d, g] = _dot_nt(cm[d, g], bm_ref[:, grp(g)]).astype(BF16)
            bm_t[d, g] = bm_ref[:, grp(g)].astype(F32).T

    for d in range(2):
        xs_ref, y_ref = dirs[d][0], dirs[d][4]
        r0 = d * SSD_HEADS
        y_pairs, h_pairs = [], []
        for pair in range(SSD_HEADS // 2):
            g = pair // (hg // 2)
            x_pair = xs_ref[:, pair_lanes(pair)]
            h_pair = h_scr[d, :, pair_lanes(pair)]
            rhs = jnp.concatenate([x_pair, h_pair.astype(BF16)], axis=0)
            ys, ups, cds = [], [], []
            for h in (2 * pair, 2 * pair + 1):
                col = jnp.broadcast_to(acs[d][:, r0 + h:r0 + h + 1], (CHUNK, CHUNK))
                dec = jnp.exp2(jnp.where(causal[d], col - row_t[d][h:h + 1, :], NEG_INF))
                m_intra = cb[d, g] * dec.astype(BF16)
                m_state = cm[d, g] * jnp.exp2(col).astype(BF16)
                ys.append(_dot(jnp.concatenate([m_intra, m_state], axis=1), rhs))
                ups.append(_dot((bm_t[d, g] * dte_t[d][h:h + 1, :]).astype(BF16), x_pair))
                cds.append(jnp.exp2(tot[d][:, r0 + h:r0 + h + 1]))
            y_pairs.append(jnp.where(lane_lo, ys[0], ys[1]))
            h_pairs.append(jnp.where(lane_lo, cds[0], cds[1]) * h_pair
                           + jnp.where(lane_lo, ups[0], ups[1]))
        y_ref[...] = jnp.concatenate(y_pairs, axis=1)
        h_scr[d] = jnp.concatenate(h_pairs, axis=1)


def _ssd_call(xbc, dt_all, bias_row, alog_row, bias_col, alog_col):
    bcol = SSD_INNER // (SSD_GROUPS * SSD_STATE)

    def specs(chunk_fn):
        return [
            pl.BlockSpec((CHUNK, SSD_INNER), lambda b, s: (chunk_fn(b, s), 0)),
            pl.BlockSpec((CHUNK, SSD_GROUPS * SSD_STATE), lambda b, s: (chunk_fn(b, s), bcol)),
            pl.BlockSpec((CHUNK, SSD_GROUPS * SSD_STATE), lambda b, s: (chunk_fn(b, s), bcol + 1)),
            pl.BlockSpec((CHUNK, DT_PAD), lambda b, s: (chunk_fn(b, s), 0)),
        ]

    const = lambda shape: pl.BlockSpec(shape, lambda b, s: (0, 0))
    return pl.pallas_call(
        _ssd_kernel,
        grid=(BATCH, NSTEP),
        in_specs=specs(_fwd_chunk) + specs(_bwd_chunk) + [
            const((1, DT_PAD)), const((1, DT_PAD)), const((DT_PAD, 1)), const((DT_PAD, 1))],
        out_specs=[
            pl.BlockSpec((CHUNK, SSD_INNER), lambda b, s: (_fwd_chunk(b, s), 0)),
            pl.BlockSpec((CHUNK, SSD_INNER), lambda b, s: (_bwd_chunk(b, s), 0)),
        ],
        out_shape=[jax.ShapeDtypeStruct((T_ALL, SSD_INNER), F32)] * 2,
        scratch_shapes=[pltpu.VMEM((2, SSD_STATE, SSD_INNER), F32)],
        compiler_params=_cparams(("arbitrary", "arbitrary")),
        name="ssd",
    )(xbc, xbc, xbc, dt_all, xbc, xbc, xbc, dt_all, bias_row, alog_row, bias_col, alog_col)


def _ret_kernel(lg_ref, q_f, k_f, v_f, q_b, k_b, v_b, of_ref, ob_ref, s_scr, tab_scr):
    s = pl.program_id(1)

    @pl.when(s == 0)
    def _():
        s_scr[...] = jnp.zeros_like(s_scr)
        ii = lax.broadcasted_iota(jnp.int32, (CHUNK, CHUNK), 0)
        jj = lax.broadcasted_iota(jnp.int32, (CHUNK, CHUNK), 1)
        for d in range(2):
            if d == 0:
                dist = (ii - jj).astype(F32)
                row_pow = (ii + 1).astype(F32)
                key_pow = (CHUNK - 1 - ii).astype(F32)
            else:
                dist = (jj - ii).astype(F32)
                row_pow = (CHUNK - ii).astype(F32)
                key_pow = ii.astype(F32)
            for h in range(RET_HEADS):
                lg = lg_ref[d, h]
                tab_scr[d, h, 0] = jnp.where(dist >= 0.0, jnp.exp(jnp.maximum(dist, 0.0) * lg), 0.0)
                tab_scr[d, h, 1] = jnp.exp(row_pow * lg)
                tab_scr[d, h, 2] = jnp.exp(key_pow * lg)

    dirs = ((q_f, k_f, v_f, of_ref), (q_b, k_b, v_b, ob_ref))
    heads = [(d, h) for d in range(2) for h in range(RET_HEADS)]
    lanes = lambda h: slice(h * HEAD_DIM, (h + 1) * HEAD_DIM)
    raw = {}
    for d, h in heads:
        q_ref, k_ref, v_ref, _ = dirs[d]
        raw[d, h] = _dot_nt(q_ref[:, lanes(h)], k_ref[:, lanes(h)])
    kv = {}
    for d, h in heads:
        _, k_ref, v_ref, _ = dirs[d]
        k_dec = (k_ref[:, lanes(h)].astype(F32) * tab_scr[d, h, 2]).astype(BF16)
        kv[d, h] = _dot_tn(k_dec, v_ref[:, lanes(h)])
    outs = {}
    for d, h in heads:
        q_ref, _, v_ref, _ = dirs[d]
        scores = raw[d, h] * tab_scr[d, h, 0]
        q_cross = q_ref[:, lanes(h)].astype(F32) * tab_scr[d, h, 1]
        state = s_scr[d, h * HEAD_DIM:(h + 1) * HEAD_DIM, :]
        lhs = jnp.concatenate([scores.astype(BF16), q_cross.astype(BF16)], axis=1)
        rhs = jnp.concatenate([v_ref[:, lanes(h)], state.astype(BF16)], axis=0)
        outs[d, h] = _dot(lhs, rhs)
    for d in range(2):
        states = []
        for h in range(RET_HEADS):
            chunk_decay = jnp.exp(jnp.full((1, HEAD_DIM), float(CHUNK), F32) * lg_ref[d, h])
            states.append(chunk_decay * s_scr[d, h * HEAD_DIM:(h + 1) * HEAD_DIM, :] + kv[d, h])
        dirs[d][3][...] = jnp.concatenate([outs[d, h] for h in range(RET_HEADS)], axis=1)
        s_scr[d] = jnp.concatenate(states, axis=0)


def _ret_call(p_all, log_decay):
    def specs(chunk_fn):
        return [
            pl.BlockSpec((CHUNK, RET_WIDTH), lambda b, s: (chunk_fn(b, s), COL_RQ // RET_WIDTH)),
            pl.BlockSpec((CHUNK, RET_WIDTH), lambda b, s: (chunk_fn(b, s), COL_RK // RET_WIDTH)),
            pl.BlockSpec((CHUNK, RET_WIDTH), lambda b, s: (chunk_fn(b, s), COL_RV // RET_WIDTH)),
        ]

    return pl.pallas_call(
        _ret_kernel,
        grid=(BATCH, NSTEP),
        in_specs=[pl.BlockSpec(memory_space=pltpu.SMEM)] + specs(_fwd_chunk) + specs(_bwd_chunk),
        out_specs=[
            pl.BlockSpec((CHUNK, RET_WIDTH), lambda b, s: (_fwd_chunk(b, s), 0)),
            pl.BlockSpec((CHUNK, RET_WIDTH), lambda b, s: (_bwd_chunk(b, s), 0)),
        ],
        out_shape=[jax.ShapeDtypeStruct((T_ALL, RET_WIDTH), F32)] * 2,
        scratch_shapes=[pltpu.VMEM((2, RET_HEADS * HEAD_DIM, HEAD_DIM), F32),
                        pltpu.VMEM((2, RET_HEADS, 3, CHUNK, CHUNK), F32)],
        compiler_params=_cparams(("arbitrary", "arbitrary")),
        name="retention",
    )(log_decay, p_all, p_all, p_all, p_all, p_all, p_all)


def _merge_kernel(attn_ref, yf_ref, yb_ref, xs_ref, z0_ref, z1_ref, of_ref, ob_ref, rg_ref,
                  ga_ref, gs_ref, gr_ref, dskip_ref, ssd_g_ref, ret_g_ref,
                  wa_ref, ws_ref, wr_ref, o_ref):
    z = jnp.concatenate([z0_ref[...], z1_ref[...]], axis=1).astype(F32)
    y = yf_ref[...] + yb_ref[...] + dskip_ref[...] * xs_ref[...].astype(F32)
    y = y * _silu(z)
    ssd_o = y * lax.rsqrt(jnp.mean(y * y, axis=-1, keepdims=True) + LN_EPS) * ssd_g_ref[...]

    o = of_ref[...] + ob_ref[...]
    normed = jnp.concatenate(
        [_ln_rows(o[:, h * HEAD_DIM:(h + 1) * HEAD_DIM]) for h in range(RET_HEADS)], axis=1)
    ret_o = normed * ret_g_ref[...] * _silu(rg_ref[...].astype(F32))

    merged = (_sigmoid(ga_ref[...].astype(F32)) * _dot(attn_ref[...], wa_ref[...])
              + _sigmoid(gs_ref[...].astype(F32)) * _dot(ssd_o.astype(BF16), ws_ref[...])
              + _sigmoid(gr_ref[...].astype(F32)) * _dot(ret_o.astype(BF16), wr_ref[...]))
    o_ref[...] = merged.astype(BF16)


def _merge_call(n_rows, attn, yf, yb, xbc, p_all, of, ob, dskip, ssd_g, ret_g, wa, ws, wr):
    tm = TM_MERGE
    zw = SSD_INNER // 2
    w1024 = lambda c: pl.BlockSpec((tm, 1024), lambda i: (i, c))
    zspec = lambda c: pl.BlockSpec((tm, zw), lambda i: (i, COL_Z // zw + c))
    gate = lambda c: pl.BlockSpec((tm, D_MODEL), lambda i: (i, COL_GATES // D_MODEL + c))
    vec = pl.BlockSpec((1, 1024), lambda i: (0, 0))
    wspec = pl.BlockSpec((1024, D_MODEL), lambda i: (0, 0), pipeline_mode=pl.Buffered(1))
    return pl.pallas_call(
        _merge_kernel,
        grid=(n_rows // tm,),
        in_specs=[
            w1024(0), w1024(0), w1024(0), w1024(0), zspec(0), zspec(1),
            w1024(0), w1024(0), w1024(COL_RG // 1024),
            gate(0), gate(1), gate(2), vec, vec, vec, wspec, wspec, wspec,
        ],
        out_specs=pl.BlockSpec((tm, D_MODEL), lambda i: (i, 0)),
        out_shape=jax.ShapeDtypeStruct((n_rows, D_MODEL), BF16),
        compiler_params=_cparams(("arbitrary",)),
        name="merge",
    )(attn, yf, yb, xbc, p_all, p_all, of, ob, p_all, p_all, p_all, p_all, dskip, ssd_g, ret_g,
      wa, ws, wr)


def _deepnorm(x, y, gate, g, b):
    return _ln_rows(DEEPNORM_ALPHA * x + gate * y) * g + b


def _outproj_kernel(m_ref, x_ref, w_ref, gate_ref, g_ref, b_ref, o_ref):
    for r in range(0, TM_OUT, RC_OUT):
        rows = slice(r, r + RC_OUT)
        mix = _dot(m_ref[rows, :], w_ref[...])
        o_ref[rows, :] = _deepnorm(x_ref[rows, :], mix, gate_ref[...], g_ref[...], b_ref[...])


def _outproj_call(n_rows, merged, x_all, w_out, mod, ln_g, ln_b):
    tm = TM_OUT
    vec = pl.BlockSpec((1, D_MODEL), lambda i: (0, 0))
    return pl.pallas_call(
        _outproj_kernel,
        grid=(n_rows // tm,),
        in_specs=[
            pl.BlockSpec((tm, D_MODEL), lambda i: (i, 0)),
            pl.BlockSpec((tm, D_MODEL), lambda i: (i, 0)),
            pl.BlockSpec((D_MODEL, D_MODEL), lambda i: (0, 0), pipeline_mode=pl.Buffered(1)),
            pl.BlockSpec((None, 1, D_MODEL), lambda i: (_batch_of_tile(i, tm) * 6 + 2, 0, 0)),
            vec, vec,
        ],
        out_specs=pl.BlockSpec((tm, D_MODEL), lambda i: (i, 0)),
        out_shape=jax.ShapeDtypeStruct((n_rows, D_MODEL), F32),
        compiler_params=_cparams(("arbitrary",)),
        name="outproj",
    )(merged, x_all, w_out, mod, ln_g, ln_b)


def _mlp_kernel(x_ref, shift_ref, scale_ref, gate_ref, wup_ref, wdn_ref, g_ref, b_ref,
                o_ref, h_scr, acc_scr):
    j = pl.program_id(1)
    last = pl.num_programs(1) - 1
    chunks = [slice(r, r + RC_MLP) for r in range(0, TM_MLP, RC_MLP)]

    def up_down(rows):
        u = jnp.maximum(_dot(h_scr[rows, :], wup_ref[...]), 0.0)
        return _dot((u * u).astype(BF16), wdn_ref[...])

    @pl.when(j == 0)
    def _():
        for rows in chunks:
            h = _ln_rows(x_ref[rows, :]) * (1.0 + scale_ref[...]) + shift_ref[...]
            h_scr[rows, :] = h.astype(BF16)
        for rows in chunks:
            acc_scr[rows, :] = up_down(rows)

    @pl.when((j > 0) & (j < last))
    def _():
        acc_scr[...] += up_down(slice(None))

    @pl.when(j == last)
    def _():
        for rows in chunks:
            y = acc_scr[rows, :] + up_down(rows)
            o_ref[rows, :] = _deepnorm(x_ref[rows, :], y, gate_ref[...], g_ref[...], b_ref[...])


def _mlp_call(n_rows, x1, mod, w_up, w_down, ln_g, ln_b):
    tm, tf = TM_MLP, TF_MLP

    def mod_map(k):
        return lambda i, j: (_batch_of_tile(i, tm) * 6 + k, 0, 0)

    vec = pl.BlockSpec((1, D_MODEL), lambda i, j: (0, 0))
    return pl.pallas_call(
        _mlp_kernel,
        grid=(n_rows // tm, D_FF // tf),
        in_specs=[
            pl.BlockSpec((tm, D_MODEL), lambda i, j: (i, 0)),
            pl.BlockSpec((None, 1, D_MODEL), mod_map(3)),
            pl.BlockSpec((None, 1, D_MODEL), mod_map(4)),
            pl.BlockSpec((None, 1, D_MODEL), mod_map(5)),
            pl.BlockSpec((D_MODEL, tf), lambda i, j: (0, j)),
            pl.BlockSpec((tf, D_MODEL), lambda i, j: (j, 0)),
            vec, vec,
        ],
        out_specs=pl.BlockSpec((tm, D_MODEL), lambda i, j: (i, 0)),
        out_shape=jax.ShapeDtypeStruct((n_rows, D_MODEL), F32),
        scratch_shapes=[pltpu.VMEM((tm, D_MODEL), BF16), pltpu.VMEM((tm, D_MODEL), F32)],
        compiler_params=_cparams(("arbitrary", "arbitrary")),
        name="mlp",
    )(x1, mod, mod, mod, w_up, w_down, ln_g, ln_b)


def _rope_tables():
    f32 = np.float32
    rows = SEQ // GRID_W
    row = np.repeat(np.arange(rows), GRID_W).astype(f32)
    col = (np.arange(rows * GRID_W) % GRID_W).astype(f32)
    n_freq = HEAD_DIM // 4
    inv = (f32(ROPE_BASE) ** (-np.arange(n_freq, dtype=f32) / f32(n_freq))).astype(f32)
    ang = np.concatenate([row[:, None] * inv, col[:, None] * inv], axis=-1).astype(f32)
    cos, sin = np.cos(ang).astype(f32), np.sin(ang).astype(f32)
    cos2 = np.concatenate([cos, cos], axis=-1)
    sin2 = np.concatenate([-sin, sin], axis=-1)
    cos2 = np.concatenate([cos2, np.ones((TM_IN, HEAD_DIM), f32)], axis=0)
    sin2 = np.concatenate([sin2, np.zeros((TM_IN, HEAD_DIM), f32)], axis=0)
    return jnp.asarray(cos2), jnp.asarray(sin2)


def _split_w_in(w):
    w_pre = w[:, :COL_DT].astype(BF16)
    w_post = w[:, DT_COLS:].astype(BF16)
    w_dt = jnp.pad(w[:, COL_DT:COL_DT + DT_COLS], ((0, 0), (0, DT_PAD - DT_COLS))).astype(BF16)
    return w_pre, w_post, w_dt


def _pad_heads(v):
    flat = jnp.pad(v.reshape(-1).astype(F32), (0, DT_PAD - DT_COLS))
    return flat.reshape(1, DT_PAD), flat.reshape(DT_PAD, 1)


def kernel(x, c, ctx, c_ctx, ada_w, ada_b, w_in, attn_sink, ssd_conv_w, ssd_conv_b, ssd_a_log,
           ssd_dt_bias, ssd_d, ssd_norm_g, ret_log_decay, ret_norm_g, w_branch_attn, w_branch_ssd,
           w_branch_ret, w_out, ln1_g, ln1_b, w_mlp_up, w_mlp_down, ln2_g, ln2_b):
    x_all = jnp.concatenate([x.reshape(T_LAT, D_MODEL), ctx.reshape(T_CTX, D_MODEL)], axis=0)
    cond = jnp.concatenate([c, c_ctx[None, :], jnp.zeros((8 - BATCH - 1, D_MODEL), F32)], axis=0)
    mod_all = _ada_call(cond, ada_w, ada_b).reshape(DEPTH, 8 * 6, 1, D_MODEL)
    cos2, sin2 = _rope_tables()

    for l in range(DEPTH):
        update_ctx = l < DEPTH - 1
        n_rows = T_ALL if update_ctx else T_LAT
        mod = mod_all[l]
        w_pre, w_post, w_dt = _split_w_in(w_in[l])
        p_all, dt_all = _inproj_call(x_all, mod, w_pre, w_post, w_dt, cos2, sin2, not update_ctx)
        xbc = _conv_call(p_all, ssd_conv_w[l], ssd_conv_b[l])

        attn = _attn_calls(p_all, attn_sink[l].astype(F32), update_ctx)

        bias_row, bias_col = _pad_heads(ssd_dt_bias[l])
        alog_row, alog_col = _pad_heads(ssd_a_log[l])
        yf, yb = _ssd_call(xbc, dt_all, bias_row, alog_row, bias_col, alog_col)
        of, ob = _ret_call(p_all, ret_log_decay[l].astype(F32))

        dskip = jnp.repeat(ssd_d[l].astype(F32), SSD_P).reshape(1, SSD_INNER)
        merged = _merge_call(
            n_rows, attn, yf, yb, xbc, p_all, of, ob, dskip,
            ssd_norm_g[l].reshape(1, SSD_INNER), ret_norm_g[l].reshape(1, RET_WIDTH),
            w_branch_attn[l].astype(BF16), w_branch_ssd[l].astype(BF16), w_branch_ret[l].astype(BF16))
        x1 = _outproj_call(n_rows, merged, x_all, w_out[l].astype(BF16), mod,
                           ln1_g[l].reshape(1, D_MODEL), ln1_b[l].reshape(1, D_MODEL))
        x_all = _mlp_call(n_rows, x1, mod, w_mlp_up[l].astype(BF16), w_mlp_down[l].astype(BF16),
                          ln2_g[l].reshape(1, D_MODEL), ln2_b[l].reshape(1, D_MODEL))
    return x_all.reshape(BATCH, SEQ, D_MODEL)
```

```python
import functools

import jax
import jax.numpy as jnp
from jax import lax
from jax.experimental import pallas as pl
from jax.experimental.pallas import tpu as pltpu
import numpy as np

F32 = jnp.float32
BF16 = jnp.bfloat16

D_MODEL = 2048
BATCH = 4
SEQ = 4096
DEPTH = 2
GRID_W = 64
CTX_LEN = 256
BLOCK = 128
HEAD_DIM = 128
ATTN_HQ = 8
ATTN_HKV = 2
ATTN_GROUP = ATTN_HQ // ATTN_HKV
ATTN_WIDTH = ATTN_HQ * HEAD_DIM
SSD_HEADS = 16
SSD_P = 64
SSD_INNER = SSD_HEADS * SSD_P
SSD_GROUPS = 2
SSD_STATE = 128
SSD_CONV = 5
SSD_CONV_CH = SSD_INNER + 2 * SSD_GROUPS * SSD_STATE
CHUNK = 128
RET_HEADS = 8
RET_WIDTH = RET_HEADS * HEAD_DIM
D_FF = 4 * D_MODEL
ROPE_BASE = 10000.0
DEEPNORM_ALPHA = (2 * DEPTH) ** 0.25
LN_EPS = 1e-6
NEG_INF = -1e30
Q_SCALE = HEAD_DIM ** -0.5
LOG2E = 1.4426950408889634

T_LAT = BATCH * SEQ
T_CTX = BATCH * CTX_LEN
T_ALL = T_LAT + T_CTX
NCHUNK_LAT = SEQ // CHUNK
NCHUNK_CTX = CTX_LEN // CHUNK
NSTEP = NCHUNK_LAT + NCHUNK_CTX

COL_AQ = 0
COL_AK = 1024
COL_AV = 1280
COL_Z = 1536
COL_XBC = 2560
COL_DT = 4096
COL_RQ = 4096
COL_RK = 5120
COL_RV = 6144
COL_RG = 7168
COL_GATES = 8192
P_COLS = 14336
DT_COLS = 2 * SSD_HEADS
DT_PAD = 128

TM_IN = 1024
TN_IN = 1024
J_AQ, J_AKAV, J_RQ, J_RK = COL_AQ // TN_IN, COL_AK // TN_IN, COL_RQ // TN_IN, COL_RK // TN_IN
J_SPLIT = COL_DT // TN_IN
J_RG, J_GATES = COL_RG // TN_IN, COL_GATES // TN_IN
TM_MERGE = 256
TM_OUT = 1024
RC_OUT = 256
TM_MLP = 1024
TF_MLP = 512
RC_MLP = 256
TM_CONV = 1024
HALO = 16

VMEM_LIMIT = 56 * 1024 * 1024


def _cparams(sem):
    return pltpu.CompilerParams(dimension_semantics=sem, vmem_limit_bytes=VMEM_LIMIT)


def _sigmoid(x):
    return 0.5 * jnp.tanh(0.5 * x) + 0.5


def _silu(x):
    return x * _sigmoid(x)


def _softplus(x):
    return jnp.maximum(x, 0.0) + jnp.log1p(jnp.exp(-jnp.abs(x)))


def _ln_rows(x):
    mu = jnp.mean(x, axis=-1, keepdims=True)
    xc = x - mu
    var = jnp.mean(xc * xc, axis=-1, keepdims=True)
    return xc * lax.rsqrt(var + LN_EPS)


def _dot(a, b):
    return jnp.dot(a, b, preferred_element_type=F32)


def _dot_nt(a, b):
    return lax.dot_general(a, b, (((1,), (1,)), ((), ())), preferred_element_type=F32)


def _dot_tn(a, b):
    return lax.dot_general(a, b, (((0,), (0,)), ((), ())), preferred_element_type=F32)


def _split3(x):
    x1 = x.astype(BF16)
    r1 = x - x1.astype(F32)
    x2 = r1.astype(BF16)
    r2 = r1 - x2.astype(F32)
    return x1, x2, r2.astype(BF16)


def _batch_of_tile(i, tm):
    return jnp.where(i < T_LAT // tm, i // (SEQ // tm), BATCH)


TN_ADA = 2048


def _ada_kernel(cond_ref, w0_ref, w1_ref, b_ref, o_ref):
    s = _silu(cond_ref[...]).astype(BF16)
    kh = D_MODEL // 2
    o_ref[...] = (_dot(s[:, :kh], w0_ref[...].astype(BF16))
                  + _dot(s[:, kh:], w1_ref[...].astype(BF16)) + b_ref[...])


def _ada_call(cond, ada_w, ada_b):
    n = ada_w.shape[-1]
    return pl.pallas_call(
        _ada_kernel,
        grid=(DEPTH, n // TN_ADA),
        in_specs=[
            pl.BlockSpec((8, D_MODEL), lambda l, j: (0, 0)),
            pl.BlockSpec((None, D_MODEL // 2, TN_ADA), lambda l, j: (l, 0, j)),
            pl.BlockSpec((None, D_MODEL // 2, TN_ADA), lambda l, j: (l, 1, j)),
            pl.BlockSpec((None, 1, TN_ADA), lambda l, j: (l, 0, j)),
        ],
        out_specs=pl.BlockSpec((None, 8, TN_ADA), lambda l, j: (l, 0, j)),
        out_shape=jax.ShapeDtypeStruct((DEPTH, 8, n), F32),
        compiler_params=_cparams(("arbitrary", "arbitrary")),
        name="ada",
    )(cond, ada_w, ada_w, ada_b.reshape(DEPTH, 1, n))


def _rope_store(acc, cos, sin, o_ref, n_heads, scale):
    for h in range(n_heads):
        xs = acc[:, h * HEAD_DIM:(h + 1) * HEAD_DIM]
        rot = pltpu.roll(xs, HEAD_DIM // 2, axis=1)
        o_ref[:, h * HEAD_DIM:(h + 1) * HEAD_DIM] = ((xs * cos + rot * sin) * scale).astype(BF16)


def _inproj_kernel(skip_ctx_cols, x_ref, shift_ref, scale_ref, wa_ref, wb_ref, wdt_ref, cos_ref, sin_ref,
                   p_ref, dt_ref, h_scr):
    i = pl.program_id(0)
    j = pl.program_id(1)

    @pl.when(j == 0)
    def _():
        rc = 256
        for r in range(0, TM_IN, rc):
            h = _ln_rows(x_ref[r:r + rc, :]) * (1.0 + scale_ref[...]) + shift_ref[...]
            h_scr[r:r + rc, :] = h.astype(BF16)
        dt_ref[...] = _dot(h_scr[...], wdt_ref[...])

    heads_per_tile = TN_IN // HEAD_DIM
    if skip_ctx_cols:
        unused = (j == J_AQ) | (j == J_RG) | (j >= J_GATES)
        active = jnp.logical_not((i == T_ALL // TM_IN - 1) & unused)
    else:
        active = True

    def rotated(w_ref, scale):
        acc = _dot(h_scr[...], w_ref[...])
        _rope_store(acc, cos_ref[...], sin_ref[...], p_ref, heads_per_tile, scale)

    def plain(w_ref):
        p_ref[...] = _dot(h_scr[...], w_ref[...]).astype(BF16)

    pl.when((j == J_AQ) & active)(lambda: rotated(wa_ref, Q_SCALE * LOG2E))

    @pl.when(j == J_AKAV)
    def _():
        acc = _dot(h_scr[...], wa_ref[...])
        _rope_store(acc, cos_ref[...], sin_ref[...], p_ref, ATTN_HKV, 1.0)
        p_ref[:, ATTN_HKV * HEAD_DIM:] = acc[:, ATTN_HKV * HEAD_DIM:].astype(BF16)

    pl.when((j > J_AKAV) & (j < J_SPLIT))(lambda: plain(wa_ref))
    pl.when(j == J_RQ)(lambda: rotated(wb_ref, Q_SCALE))
    pl.when(j == J_RK)(lambda: rotated(wb_ref, 1.0))
    pl.when((j > J_RK) & active)(lambda: plain(wb_ref))


def _inproj_call(x_all, mod, w_pre, w_post, w_dt, cos2, sin2, skip_ctx_cols):
    n_tiles = T_ALL // TM_IN
    lat_tiles = T_LAT // TM_IN

    def mod_map(k):
        return lambda i, j: (_batch_of_tile(i, TM_IN) * 6 + k, 0, 0)

    def rope_map(i, j):
        return (jnp.where(i < lat_tiles, i % (SEQ // TM_IN), SEQ // TM_IN), 0)

    return pl.pallas_call(
        functools.partial(_inproj_kernel, skip_ctx_cols),
        grid=(n_tiles, P_COLS // TN_IN),
        in_specs=[
            pl.BlockSpec((TM_IN, D_MODEL), lambda i, j: (i, 0)),
            pl.BlockSpec((None, 1, D_MODEL), mod_map(0)),
            pl.BlockSpec((None, 1, D_MODEL), mod_map(1)),
            pl.BlockSpec((D_MODEL, TN_IN), lambda i, j: (0, jnp.minimum(j, J_SPLIT - 1))),
            pl.BlockSpec((D_MODEL, TN_IN), lambda i, j: (0, jnp.maximum(j, J_SPLIT))),
            pl.BlockSpec((D_MODEL, DT_PAD), lambda i, j: (0, 0)),
            pl.BlockSpec((TM_IN, HEAD_DIM), rope_map),
            pl.BlockSpec((TM_IN, HEAD_DIM), rope_map),
        ],
        out_specs=[
            pl.BlockSpec((TM_IN, TN_IN), lambda i, j: (i, j)),
            pl.BlockSpec((TM_IN, DT_PAD), lambda i, j: (i, 0)),
        ],
        out_shape=[
            jax.ShapeDtypeStruct((T_ALL, P_COLS), BF16),
            jax.ShapeDtypeStruct((T_ALL, DT_PAD), F32),
        ],
        scratch_shapes=[pltpu.VMEM((TM_IN, D_MODEL), BF16)],
        compiler_params=_cparams(("arbitrary", "arbitrary")),
        name="inproj",
    )(x_all, mod, mod, w_pre, w_post, w_dt, cos2, sin2)


TN_CONV = 512


def _conv_taps(prev, x, nxt, w_ref, b_ref, pos, slen):
    xe = jnp.concatenate([prev, x, nxt], axis=0)
    w = w_ref[...]
    acc = jnp.zeros((TM_CONV, TN_CONV), F32) + b_ref[...]
    half = SSD_CONV // 2
    for k in range(SSD_CONV):
        off = HALO - half + k
        xk = xe[off:off + TM_CONV, :]
        if pos is not None and k < half:
            xk = jnp.where(pos + (k - half) >= 0, xk, 0.0)
        elif pos is not None and k > half:
            xk = jnp.where(pos + (k - half) < slen, xk, 0.0)
        acc = acc + xk * w[k:k + 1, :]
    return _silu(acc).astype(BF16)


def _conv_kernel(x_ref, prev_ref, next_ref, w_ref, b_ref, o_ref):
    i = pl.program_id(0)
    tiles_per_seq = SEQ // TM_CONV
    x = x_ref[...].astype(F32)
    prev = prev_ref[...].astype(F32)
    nxt = next_ref[...].astype(F32)

    @pl.when(i < T_LAT // TM_CONV)
    def _():
        t = i & (tiles_per_seq - 1)
        o_ref[...] = _conv_taps(jnp.where(t == 0, 0.0, prev), x,
                                jnp.where(t == tiles_per_seq - 1, 0.0, nxt), w_ref, b_ref, None, None)

    @pl.when(i >= T_LAT // TM_CONV)
    def _():
        row = lax.broadcasted_iota(jnp.int32, (TM_CONV, TN_CONV), 0)
        o_ref[...] = _conv_taps(prev, x, nxt, w_ref, b_ref, row & (CTX_LEN - 1), CTX_LEN)


def _conv_call(p_all, conv_w, conv_b):
    cb0 = COL_XBC // TN_CONV
    hpt = TM_CONV // HALO
    last_halo = T_ALL // HALO - 1
    return pl.pallas_call(
        _conv_kernel,
        grid=(T_ALL // TM_CONV, SSD_CONV_CH // TN_CONV),
        in_specs=[
            pl.BlockSpec((TM_CONV, TN_CONV), lambda i, j: (i, cb0 + j)),
            pl.BlockSpec((HALO, TN_CONV), lambda i, j: (jnp.maximum(i * hpt - 1, 0), cb0 + j)),
            pl.BlockSpec((HALO, TN_CONV), lambda i, j: (jnp.minimum((i + 1) * hpt, last_halo), cb0 + j)),
            pl.BlockSpec((SSD_CONV, TN_CONV), lambda i, j: (0, j)),
            pl.BlockSpec((1, TN_CONV), lambda i, j: (0, j)),
        ],
        out_specs=pl.BlockSpec((TM_CONV, TN_CONV), lambda i, j: (i, j)),
        out_shape=jax.ShapeDtypeStruct((T_ALL, SSD_CONV_CH), BF16),
        compiler_params=_cparams(("arbitrary", "arbitrary")),
        name="conv",
    )(p_all, p_all, p_all, conv_w, conv_b.reshape(1, SSD_CONV_CH))


def _softmax_pv(s, vals, sink_col):
    m = jnp.maximum(jnp.max(s, axis=-1, keepdims=True), sink_col)
    p = jnp.exp2(s - m)
    l = jnp.sum(p, axis=-1, keepdims=True) + jnp.exp2(sink_col - m)
    return _dot(p.astype(BF16), vals) / l


def _group_queries(q_ref, hkv):
    h0 = hkv * ATTN_GROUP
    return jnp.concatenate(
        [q_ref[:, (h0 + g) * HEAD_DIM:(h0 + g + 1) * HEAD_DIM] for g in range(ATTN_GROUP)], axis=0)


def _sink_column(sink_ref, hkv, nq):
    return jnp.concatenate(
        [jnp.full((nq, 1), sink_ref[hkv * ATTN_GROUP + g] * LOG2E, F32) for g in range(ATTN_GROUP)], axis=0)


def _attn_kernel(sink_ref, q_ref, kp_ref, kc_ref, kn_ref, vp_ref, vc_ref, vn_ref, kx_ref, vx_ref,
                 o_ref):
    n = pl.program_id(1)
    rows = ATTN_GROUP * BLOCK
    qi = lax.broadcasted_iota(jnp.int32, (rows, BLOCK), 0) & (BLOCK - 1)
    kj = lax.broadcasted_iota(jnp.int32, (rows, BLOCK), 1)
    prev_ok = (kj >= qi) & (n > 0)
    next_ok = (kj <= qi) & (n < SEQ // BLOCK - 1)

    def mask_fn(s):
        return jnp.concatenate([
            jnp.where(prev_ok, s[:, :BLOCK], NEG_INF), s[:, BLOCK:2 * BLOCK],
            jnp.where(next_ok, s[:, 2 * BLOCK:3 * BLOCK], NEG_INF), s[:, 3 * BLOCK:]], axis=1)

    kv_cols = lambda hkv: slice(hkv * HEAD_DIM, (hkv + 1) * HEAD_DIM)
    scores = []
    for hkv in range(ATTN_HKV):
        c = kv_cols(hkv)
        keys = jnp.concatenate([kp_ref[:, c], kc_ref[:, c], kn_ref[:, c], kx_ref[:, c]], axis=0)
        scores.append(mask_fn(_dot_nt(_group_queries(q_ref, hkv), keys)))
    outs = []
    for hkv in range(ATTN_HKV):
        c = kv_cols(hkv)
        vals = jnp.concatenate([vp_ref[:, c], vc_ref[:, c], vn_ref[:, c], vx_ref[:, c]], axis=0)
        o = _softmax_pv(scores[hkv], vals, _sink_column(sink_ref, hkv, BLOCK))
        outs += [o[g * BLOCK:(g + 1) * BLOCK, :] for g in range(ATTN_GROUP)]
    o_ref[...] = jnp.concatenate(outs, axis=1).astype(BF16)


def _ctx_attn_kernel(sink_ref, q_ref, kx_ref, vx_ref, prev_ref, o_ref):
    del prev_ref
    kv_cols = lambda hkv: slice(hkv * HEAD_DIM, (hkv + 1) * HEAD_DIM)
    scores = [_dot_nt(_group_queries(q_ref, hkv), kx_ref[:, kv_cols(hkv)]) for hkv in range(ATTN_HKV)]
    outs = []
    for hkv in range(ATTN_HKV):
        o = _softmax_pv(scores[hkv], vx_ref[:, kv_cols(hkv)], _sink_column(sink_ref, hkv, CTX_LEN))
        outs += [o[g * CTX_LEN:(g + 1) * CTX_LEN, :] for g in range(ATTN_GROUP)]
    o_ref[...] = jnp.concatenate(outs, axis=1).astype(BF16)


def _attn_calls(p_all, sink, update_ctx):
    nb = SEQ // BLOCK
    kvw = ATTN_HKV * HEAD_DIM
    ck = COL_AK // kvw
    cv = COL_AV // kvw
    ctx0 = T_LAT // CTX_LEN
    smem = pl.BlockSpec(memory_space=pltpu.SMEM)

    def kv_spec(col, dn):
        return pl.BlockSpec((BLOCK, kvw), lambda b, n: (b * nb + jnp.clip(n + dn, 0, nb - 1), col))

    def ctx_spec(col):
        return pl.BlockSpec((CTX_LEN, kvw), lambda b, n: (ctx0 + b, col))

    attn = pl.pallas_call(
        _attn_kernel,
        grid=(BATCH, nb),
        in_specs=[
            smem,
            pl.BlockSpec((BLOCK, ATTN_WIDTH), lambda b, n: (b * nb + n, 0)),
            kv_spec(ck, -1), kv_spec(ck, 0), kv_spec(ck, 1),
            kv_spec(cv, -1), kv_spec(cv, 0), kv_spec(cv, 1),
            ctx_spec(ck), ctx_spec(cv),
        ],
        out_specs=pl.BlockSpec((BLOCK, ATTN_WIDTH), lambda b, n: (b * nb + n, 0)),
        out_shape=jax.ShapeDtypeStruct((T_ALL, ATTN_WIDTH), BF16),
        compiler_params=_cparams(("arbitrary", "arbitrary")),
        name="attn",
    )(sink, p_all, p_all, p_all, p_all, p_all, p_all, p_all, p_all, p_all)
    if not update_ctx:
        return attn
    return pl.pallas_call(
        _ctx_attn_kernel,
        grid=(BATCH,),
        in_specs=[
            smem,
            pl.BlockSpec((CTX_LEN, ATTN_WIDTH), lambda b: (ctx0 + b, 0)),
            pl.BlockSpec((CTX_LEN, kvw), lambda b: (ctx0 + b, ck)),
            pl.BlockSpec((CTX_LEN, kvw), lambda b: (ctx0 + b, cv)),
            pl.BlockSpec(memory_space=pl.ANY),
        ],
        out_specs=pl.BlockSpec((CTX_LEN, ATTN_WIDTH), lambda b: (ctx0 + b, 0)),
        out_shape=jax.ShapeDtypeStruct((T_ALL, ATTN_WIDTH), BF16),
        input_output_aliases={4: 0},
        compiler_params=_cparams(("arbitrary",)),
        name="ctx_attn",
    )(sink, p_all, p_all, p_all, attn)


def _fwd_chunk(b, s):
    ctx = T_LAT // CHUNK + b * NCHUNK_CTX + s
    lat = b * NCHUNK_LAT + (s - NCHUNK_CTX)
    return jnp.where(s < NCHUNK_CTX, ctx, lat)


def _bwd_chunk(b, s):
    ctx = T_LAT // CHUNK + b * NCHUNK_CTX + (NCHUNK_CTX - 1 - s)
    lat = b * NCHUNK_LAT + (NSTEP - 1 - s)
    return jnp.where(s < NCHUNK_CTX, ctx, lat)


def _ssd_kernel(xs_f, bm_f, cm_f, dt_f, xs_b, bm_b, cm_b, dt_b,
                bias_row, alog_row, bias_col, alog_col, yf_ref, yb_ref, h_scr):
    s = pl.program_id(1)

    @pl.when(s == 0)
    def _():
        h_scr[...] = jnp.zeros_like(h_scr)

    ii = lax.broadcasted_iota(jnp.int32, (CHUNK, CHUNK), 0)
    jj = lax.broadcasted_iota(jnp.int32, (CHUNK, CHUNK), 1)
    lane_lo = jj < SSD_P
    lower = jj <= ii
    upper = jj >= ii
    hg = SSD_HEADS // SSD_GROUPS
    dirs = ((xs_f, bm_f, cm_f, dt_f, yf_ref), (xs_b, bm_b, cm_b, dt_b, yb_ref))
    causal = (lower, upper)
    grp = lambda g: slice(g * SSD_STATE, (g + 1) * SSD_STATE)
    pair_lanes = lambda pair: slice(pair * 2 * SSD_P, (pair + 1) * 2 * SSD_P)

    acs, acs_t, row_t, dte_t, tot = {}, {}, {}, {}, {}
    for d in range(2):
        dt_ref = dirs[d][3]
        tri = jnp.where(causal[d], 1.0, 0.0).astype(BF16)
        tri_t = jnp.where(causal[1 - d], 1.0, 0.0).astype(BF16)
        last = CHUNK - 1 if d == 0 else 0
        r0 = d * SSD_HEADS
        dt_raw = dt_ref[...]
        dt_c = _softplus(dt_raw + bias_row[...])
        adt_c = dt_c * (-LOG2E * jnp.exp(alog_row[...]))
        acs[d] = sum(_dot(tri, p) for p in _split3(adt_c))
        dt_t = _softplus(dt_raw.T[r0:r0 + SSD_HEADS, :] + bias_col[r0:r0 + SSD_HEADS, :])
        adt_t = dt_t * (-LOG2E * jnp.exp(alog_col[r0:r0 + SSD_HEADS, :]))
        acs_t[d] = sum(_dot(p, tri_t) for p in _split3(adt_t))
        row_t[d] = acs_t[d] - jnp.log2(dt_t)
        dte_t[d] = jnp.exp2(acs_t[d][:, last:last + 1] - acs_t[d]) * dt_t
        tot[d] = acs[d][last:last + 1, :]

    cb, bm_t, cm = {}, {}, {}
    for d in range(2):
        _, bm_ref, cm_ref, _, _ = dirs[d]
        for g in range(SSD_GROUPS):
            cm[d, g] = cm_ref[:, grp(g)]
            cb[d, g] = _dot_nt(cm[d, g], bm_ref[:, grp(g)]).astype(BF16)
            bm_t[d, g] = bm_ref[:, grp(g)].astype(F32).T

    for d in range(2):
        xs_ref, y_ref = dirs[d][0], dirs[d][4]
        r0 = d * SSD_HEADS
        y_pairs, h_pairs = [], []
        for pair in range(SSD_HEADS // 2):
            g = pair // (hg // 2)
            x_pair = xs_ref[:, pair_lanes(pair)]
            h_pair = h_scr[d, :, pair_lanes(pair)]
            rhs = jnp.concatenate([x_pair, h_pair.astype(BF16)], axis=0)
            ys, ups, cds = [], [], []
            for h in (2 * pair, 2 * pair + 1):
                col = jnp.broadcast_to(acs[d][:, r0 + h:r0 + h + 1], (CHUNK, CHUNK))
                dec = jnp.exp2(jnp.where(causal[d], col - row_t[d][h:h + 1, :], NEG_INF))
                m_intra = cb[d, g] * dec.astype(BF16)
                m_state = cm[d, g] * jnp.exp2(col).astype(BF16)
                ys.append(_dot(jnp.concatenate([m_intra, m_state], axis=1), rhs))
                ups.append(_dot((bm_t[d, g] * dte_t[d][h:h + 1, :]).astype(BF16), x_pair))
                cds.append(jnp.exp2(tot[d][:, r0 + h:r0 + h + 1]))
            y_pairs.append(jnp.where(lane_lo, ys[0], ys[1]))
            h_pairs.append(jnp.where(lane_lo, cds[0], cds[1]) * h_pair
                           + jnp.where(lane_lo, ups[0], ups[1]))
        y_ref[...] = jnp.concatenate(y_pairs, axis=1).astype(BF16)
        h_scr[d] = jnp.concatenate(h_pairs, axis=1)


def _ssd_call(xbc, dt_all, bias_row, alog_row, bias_col, alog_col):
    bcol = SSD_INNER // (SSD_GROUPS * SSD_STATE)

    def specs(chunk_fn):
        return [
            pl.BlockSpec((CHUNK, SSD_INNER), lambda b, s: (chunk_fn(b, s), 0)),
            pl.BlockSpec((CHUNK, SSD_GROUPS * SSD_STATE), lambda b, s: (chunk_fn(b, s), bcol)),
            pl.BlockSpec((CHUNK, SSD_GROUPS * SSD_STATE), lambda b, s: (chunk_fn(b, s), bcol + 1)),
            pl.BlockSpec((CHUNK, DT_PAD), lambda b, s: (chunk_fn(b, s), 0)),
        ]

    const = lambda shape: pl.BlockSpec(shape, lambda b, s: (0, 0))
    return pl.pallas_call(
        _ssd_kernel,
        grid=(BATCH, NSTEP),
        in_specs=specs(_fwd_chunk) + specs(_bwd_chunk) + [
            const((1, DT_PAD)), const((1, DT_PAD)), const((DT_PAD, 1)), const((DT_PAD, 1))],
        out_specs=[
            pl.BlockSpec((CHUNK, SSD_INNER), lambda b, s: (_fwd_chunk(b, s), 0)),
            pl.BlockSpec((CHUNK, SSD_INNER), lambda b, s: (_bwd_chunk(b, s), 0)),
        ],
        out_shape=[jax.ShapeDtypeStruct((T_ALL, SSD_INNER), BF16)] * 2,
        scratch_shapes=[pltpu.VMEM((2, SSD_STATE, SSD_INNER), F32)],
        compiler_params=_cparams(("arbitrary", "arbitrary")),
        name="ssd",
    )(xbc, xbc, xbc, dt_all, xbc, xbc, xbc, dt_all, bias_row, alog_row, bias_col, alog_col)


def _ret_kernel(lg_ref, q_f, k_f, v_f, q_b, k_b, v_b, of_ref, ob_ref, s_scr, tab_scr):
    s = pl.program_id(1)

    @pl.when(s == 0)
    def _():
        s_scr[...] = jnp.zeros_like(s_scr)
        ii = lax.broadcasted_iota(jnp.int32, (CHUNK, CHUNK), 0)
        jj = lax.broadcasted_iota(jnp.int32, (CHUNK, CHUNK), 1)
        for d in range(2):
            if d == 0:
                dist = (ii - jj).astype(F32)
                row_pow = (ii + 1).astype(F32)
                key_pow = (CHUNK - 1 - ii).astype(F32)
            else:
                dist = (jj - ii).astype(F32)
                row_pow = (CHUNK - ii).astype(F32)
                key_pow = ii.astype(F32)
            for h in range(RET_HEADS):
                lg = lg_ref[d, h]
                tab_scr[d, h, 0] = jnp.where(dist >= 0.0, jnp.exp(jnp.maximum(dist, 0.0) * lg), 0.0)
                tab_scr[d, h, 1] = jnp.exp(row_pow * lg)
                tab_scr[d, h, 2] = jnp.exp(key_pow * lg)

    dirs = ((q_f, k_f, v_f, of_ref), (q_b, k_b, v_b, ob_ref))
    heads = [(d, h) for d in range(2) for h in range(RET_HEADS)]
    lanes = lambda h: slice(h * HEAD_DIM, (h + 1) * HEAD_DIM)
    raw = {}
    for d, h in heads:
        q_ref, k_ref, v_ref, _ = dirs[d]
        raw[d, h] = _dot_nt(q_ref[:, lanes(h)], k_ref[:, lanes(h)])
    kv = {}
    for d, h in heads:
        _, k_ref, v_ref, _ = dirs[d]
        k_dec = (k_ref[:, lanes(h)].astype(F32) * tab_scr[d, h, 2]).astype(BF16)
        kv[d, h] = _dot_tn(k_dec, v_ref[:, lanes(h)])
    outs = {}
    for d, h in heads:
        q_ref, _, v_ref, _ = dirs[d]
        scores = raw[d, h] * tab_scr[d, h, 0]
        q_cross = q_ref[:, lanes(h)].astype(F32) * tab_scr[d, h, 1]
        state = s_scr[d, h * HEAD_DIM:(h + 1) * HEAD_DIM, :]
        lhs = jnp.concatenate([scores.astype(BF16), q_cross.astype(BF16)], axis=1)
        rhs = jnp.concatenate([v_ref[:, lanes(h)], state.astype(BF16)], axis=0)
        outs[d, h] = _dot(lhs, rhs)
    for d in range(2):
        states = []
        for h in range(RET_HEADS):
            chunk_decay = jnp.exp(jnp.full((1, HEAD_DIM), float(CHUNK), F32) * lg_ref[d, h])
            states.append(chunk_decay * s_scr[d, h * HEAD_DIM:(h + 1) * HEAD_DIM, :] + kv[d, h])
        dirs[d][3][...] = jnp.concatenate([outs[d, h] for h in range(RET_HEADS)], axis=1).astype(BF16)
        s_scr[d] = jnp.concatenate(states, axis=0)


def _ret_call(p_all, log_decay):
    def specs(chunk_fn):
        return [
            pl.BlockSpec((CHUNK, RET_WIDTH), lambda b, s: (chunk_fn(b, s), COL_RQ // RET_WIDTH)),
            pl.BlockSpec((CHUNK, RET_WIDTH), lambda b, s: (chunk_fn(b, s), COL_RK // RET_WIDTH)),
            pl.BlockSpec((CHUNK, RET_WIDTH), lambda b, s: (chunk_fn(b, s), COL_RV // RET_WIDTH)),
        ]

    return pl.pallas_call(
        _ret_kernel,
        grid=(BATCH, NSTEP),
        in_specs=[pl.BlockSpec(memory_space=pltpu.SMEM)] + specs(_fwd_chunk) + specs(_bwd_chunk),
        out_specs=[
            pl.BlockSpec((CHUNK, RET_WIDTH), lambda b, s: (_fwd_chunk(b, s), 0)),
            pl.BlockSpec((CHUNK, RET_WIDTH), lambda b, s: (_bwd_chunk(b, s), 0)),
        ],
        out_shape=[jax.ShapeDtypeStruct((T_ALL, RET_WIDTH), BF16)] * 2,
        scratch_shapes=[pltpu.VMEM((2, RET_HEADS * HEAD_DIM, HEAD_DIM), F32),
                        pltpu.VMEM((2, RET_HEADS, 3, CHUNK, CHUNK), F32)],
        compiler_params=_cparams(("arbitrary", "arbitrary")),
        name="retention",
    )(log_decay, p_all, p_all, p_all, p_all, p_all, p_all)


def _merge_kernel(attn_ref, yf_ref, yb_ref, xs_ref, z0_ref, z1_ref, of_ref, ob_ref, rg_ref,
                  ga_ref, gs_ref, gr_ref, dskip_ref, ssd_g_ref, ret_g_ref,
                  wa_ref, ws_ref, wr_ref, o_ref):
    z = jnp.concatenate([z0_ref[...], z1_ref[...]], axis=1).astype(F32)
    y = (yf_ref[...].astype(F32) + yb_ref[...].astype(F32)
         + dskip_ref[...] * xs_ref[...].astype(F32))
    y = y * _silu(z)
    ssd_o = y * lax.rsqrt(jnp.mean(y * y, axis=-1, keepdims=True) + LN_EPS) * ssd_g_ref[...]

    o = of_ref[...].astype(F32) + ob_ref[...].astype(F32)
    normed = jnp.concatenate(
        [_ln_rows(o[:, h * HEAD_DIM:(h + 1) * HEAD_DIM]) for h in range(RET_HEADS)], axis=1)
    ret_o = normed * ret_g_ref[...] * _silu(rg_ref[...].astype(F32))

    merged = (_sigmoid(ga_ref[...].astype(F32)) * _dot(attn_ref[...], wa_ref[...])
              + _sigmoid(gs_ref[...].astype(F32)) * _dot(ssd_o.astype(BF16), ws_ref[...])
              + _sigmoid(gr_ref[...].astype(F32)) * _dot(ret_o.astype(BF16), wr_ref[...]))
    o_ref[...] = merged.astype(BF16)


def _merge_call(n_rows, attn, yf, yb, xbc, p_all, of, ob, dskip, ssd_g, ret_g, wa, ws, wr):
    tm = TM_MERGE
    zw = SSD_INNER // 2
    w1024 = lambda c: pl.BlockSpec((tm, 1024), lambda i: (i, c))
    zspec = lambda c: pl.BlockSpec((tm, zw), lambda i: (i, COL_Z // zw + c))
    gate = lambda c: pl.BlockSpec((tm, D_MODEL), lambda i: (i, COL_GATES // D_MODEL + c))
    vec = pl.BlockSpec((1, 1024), lambda i: (0, 0))
    wspec = pl.BlockSpec((1024, D_MODEL), lambda i: (0, 0), pipeline_mode=pl.Buffered(1))
    return pl.pallas_call(
        _merge_kernel,
        grid=(n_rows // tm,),
        in_specs=[
            w1024(0), w1024(0), w1024(0), w1024(0), zspec(0), zspec(1),
            w1024(0), w1024(0), w1024(COL_RG // 1024),
            gate(0), gate(1), gate(2), vec, vec, vec, wspec, wspec, wspec,
        ],
        out_specs=pl.BlockSpec((tm, D_MODEL), lambda i: (i, 0)),
        out_shape=jax.ShapeDtypeStruct((n_rows, D_MODEL), BF16),
        compiler_params=_cparams(("arbitrary",)),
        name="merge",
    )(attn, yf, yb, xbc, p_all, p_all, of, ob, p_all, p_all, p_all, p_all, dskip, ssd_g, ret_g,
      wa, ws, wr)


def _deepnorm(x, y, gate, g, b):
    return _ln_rows(DEEPNORM_ALPHA * x + gate * y) * g + b


def _outproj_kernel(m_ref, x_ref, w_ref, gate_ref, g_ref, b_ref, o_ref):
    for r in range(0, TM_OUT, RC_OUT):
        rows = slice(r, r + RC_OUT)
        mix = _dot(m_ref[rows, :], w_ref[...])
        o_ref[rows, :] = _deepnorm(x_ref[rows, :], mix, gate_ref[...], g_ref[...], b_ref[...])


def _outproj_call(n_rows, merged, x_all, w_out, mod, ln_g, ln_b):
    tm = TM_OUT
    vec = pl.BlockSpec((1, D_MODEL), lambda i: (0, 0))
    return pl.pallas_call(
        _outproj_kernel,
        grid=(n_rows // tm,),
        in_specs=[
            pl.BlockSpec((tm, D_MODEL), lambda i: (i, 0)),
            pl.BlockSpec((tm, D_MODEL), lambda i: (i, 0)),
            pl.BlockSpec((D_MODEL, D_MODEL), lambda i: (0, 0), pipeline_mode=pl.Buffered(1)),
            pl.BlockSpec((None, 1, D_MODEL), lambda i: (_batch_of_tile(i, tm) * 6 + 2, 0, 0)),
            vec, vec,
        ],
        out_specs=pl.BlockSpec((tm, D_MODEL), lambda i: (i, 0)),
        out_shape=jax.ShapeDtypeStruct((n_rows, D_MODEL), F32),
        compiler_params=_cparams(("arbitrary",)),
        name="outproj",
    )(merged, x_all, w_out, mod, ln_g, ln_b)


def _mlp_kernel(x_ref, shift_ref, scale_ref, gate_ref, wup_ref, wdn_ref, g_ref, b_ref,
                o_ref, h_scr):
    j = pl.program_id(1)
    last = pl.num_programs(1) - 1
    chunks = [slice(r, r + RC_MLP) for r in range(0, TM_MLP, RC_MLP)]

    def up_down(rows):
        u = jnp.maximum(_dot(h_scr[rows, :], wup_ref[...]), 0.0)
        return _dot((u * u).astype(BF16), wdn_ref[...])

    @pl.when(j == 0)
    def _():
        for rows in chunks:
            h = _ln_rows(x_ref[rows, :]) * (1.0 + scale_ref[...]) + shift_ref[...]
            h_scr[rows, :] = h.astype(BF16)
        for rows in chunks:
            o_ref[rows, :] = up_down(rows)

    @pl.when((j > 0) & (j < last))
    def _():
        o_ref[...] += up_down(slice(None))

    @pl.when(j == last)
    def _():
        for rows in chunks:
            y = o_ref[rows, :] + up_down(rows)
            o_ref[rows, :] = _deepnorm(x_ref[rows, :], y, gate_ref[...], g_ref[...], b_ref[...])


def _mlp_call(n_rows, x1, mod, w_up, w_down, ln_g, ln_b):
    tm, tf = TM_MLP, TF_MLP

    def mod_map(k):
        return lambda i, j: (_batch_of_tile(i, tm) * 6 + k, 0, 0)

    vec = pl.BlockSpec((1, D_MODEL), lambda i, j: (0, 0))
    return pl.pallas_call(
        _mlp_kernel,
        grid=(n_rows // tm, D_FF // tf),
        in_specs=[
            pl.BlockSpec((tm, D_MODEL), lambda i, j: (i, 0)),
            pl.BlockSpec((None, 1, D_MODEL), mod_map(3)),
            pl.BlockSpec((None, 1, D_MODEL), mod_map(4)),
            pl.BlockSpec((None, 1, D_MODEL), mod_map(5)),
            pl.BlockSpec((D_MODEL, tf), lambda i, j: (0, j)),
            pl.BlockSpec((tf, D_MODEL), lambda i, j: (j, 0)),
            vec, vec,
        ],
        out_specs=pl.BlockSpec((tm, D_MODEL), lambda i, j: (i, 0)),
        out_shape=jax.ShapeDtypeStruct((n_rows, D_MODEL), F32),
        scratch_shapes=[pltpu.VMEM((tm, D_MODEL), BF16)],
        compiler_params=_cparams(("arbitrary", "arbitrary")),
        name="mlp",
    )(x1, mod, mod, mod, w_up, w_down, ln_g, ln_b)


def _rope_tables():
    f32 = np.float32
    rows = SEQ // GRID_W
    row = np.repeat(np.arange(rows), GRID_W).astype(f32)
    col = (np.arange(rows * GRID_W) % GRID_W).astype(f32)
    n_freq = HEAD_DIM // 4
    inv = (f32(ROPE_BASE) ** (-np.arange(n_freq, dtype=f32) / f32(n_freq))).astype(f32)
    ang = np.concatenate([row[:, None] * inv, col[:, None] * inv], axis=-1).astype(f32)
    cos, sin = np.cos(ang).astype(f32), np.sin(ang).astype(f32)
    cos2 = np.concatenate([cos, cos], axis=-1)
    sin2 = np.concatenate([-sin, sin], axis=-1)
    cos2 = np.concatenate([cos2, np.ones((TM_IN, HEAD_DIM), f32)], axis=0)
    sin2 = np.concatenate([sin2, np.zeros((TM_IN, HEAD_DIM), f32)], axis=0)
    return jnp.asarray(cos2), jnp.asarray(sin2)


def _split_w_in(w):
    w_pre = w[:, :COL_DT].astype(BF16)
    w_post = w[:, DT_COLS:].astype(BF16)
    w_dt = jnp.pad(w[:, COL_DT:COL_DT + DT_COLS], ((0, 0), (0, DT_PAD - DT_COLS))).astype(BF16)
    return w_pre, w_post, w_dt


def _pad_heads(v):
    flat = jnp.pad(v.reshape(-1).astype(F32), (0, DT_PAD - DT_COLS))
    return flat.reshape(1, DT_PAD), flat.reshape(DT_PAD, 1)


def kernel(x, c, ctx, c_ctx, ada_w, ada_b, w_in, attn_sink, ssd_conv_w, ssd_conv_b, ssd_a_log,
           ssd_dt_bias, ssd_d, ssd_norm_g, ret_log_decay, ret_norm_g, w_branch_attn, w_branch_ssd,
           w_branch_ret, w_out, ln1_g, ln1_b, w_mlp_up, w_mlp_down, ln2_g, ln2_b):
    x_all = jnp.concatenate([x.reshape(T_LAT, D_MODEL), ctx.reshape(T_CTX, D_MODEL)], axis=0)
    cond = jnp.concatenate([c, c_ctx[None, :], jnp.zeros((8 - BATCH - 1, D_MODEL), F32)], axis=0)
    mod_all = _ada_call(cond, ada_w, ada_b).reshape(DEPTH, 8 * 6, 1, D_MODEL)
    cos2, sin2 = _rope_tables()

    for l in range(DEPTH):
        update_ctx = l < DEPTH - 1
        n_rows = T_ALL if update_ctx else T_LAT
        mod = mod_all[l]
        w_pre, w_post, w_dt = _split_w_in(w_in[l])
        p_all, dt_all = _inproj_call(x_all, mod, w_pre, w_post, w_dt, cos2, sin2, not update_ctx)
        xbc = _conv_call(p_all, ssd_conv_w[l], ssd_conv_b[l])

        attn = _attn_calls(p_all, attn_sink[l].astype(F32), update_ctx)

        bias_row, bias_col = _pad_heads(ssd_dt_bias[l])
        alog_row, alog_col = _pad_heads(ssd_a_log[l])
        yf, yb = _ssd_call(xbc, dt_all, bias_row, alog_row, bias_col, alog_col)
        of, ob = _ret_call(p_all, ret_log_decay[l].astype(F32))

        dskip = jnp.repeat(ssd_d[l].astype(F32), SSD_P).reshape(1, SSD_INNER)
        merged = _merge_call(
            n_rows, attn, yf, yb, xbc, p_all, of, ob, dskip,
            ssd_norm_g[l].reshape(1, SSD_INNER), ret_norm_g[l].reshape(1, RET_WIDTH),
            w_branch_attn[l].astype(BF16), w_branch_ssd[l].astype(BF16), w_branch_ret[l].astype(BF16))
        x1 = _outproj_call(n_rows, merged, x_all, w_out[l].astype(BF16), mod,
                           ln1_g[l].reshape(1, D_MODEL), ln1_b[l].reshape(1, D_MODEL))
        x_all = _mlp_call(n_rows, x1, mod, w_mlp_up[l].astype(BF16), w_mlp_down[l].astype(BF16),
                          ln2_g[l].reshape(1, D_MODEL), ln2_b[l].reshape(1, D_MODEL))
    return x_all.reshape(BATCH, SEQ, D_MODEL)
```

```python
import functools

import jax
import jax.numpy as jnp
from jax import lax
from jax.experimental import pallas as pl
from jax.experimental.pallas import tpu as pltpu
import numpy as np

F32 = jnp.float32
BF16 = jnp.bfloat16

D_MODEL = 2048
BATCH = 4
SEQ = 4096
DEPTH = 2
GRID_W = 64
CTX_LEN = 256
BLOCK = 128
HEAD_DIM = 128
ATTN_HQ = 8
ATTN_HKV = 2
ATTN_GROUP = ATTN_HQ // ATTN_HKV
ATTN_WIDTH = ATTN_HQ * HEAD_DIM
SSD_HEADS = 16
SSD_P = 64
SSD_INNER = SSD_HEADS * SSD_P
SSD_GROUPS = 2
SSD_STATE = 128
SSD_CONV = 5
SSD_CONV_CH = SSD_INNER + 2 * SSD_GROUPS * SSD_STATE
CHUNK = 128
RET_HEADS = 8
RET_WIDTH = RET_HEADS * HEAD_DIM
D_FF = 4 * D_MODEL
ROPE_BASE = 10000.0
DEEPNORM_ALPHA = (2 * DEPTH) ** 0.25
LN_EPS = 1e-6
NEG_INF = -1e30
Q_SCALE = HEAD_DIM ** -0.5
LOG2E = 1.4426950408889634

T_LAT = BATCH * SEQ
T_CTX = BATCH * CTX_LEN
T_ALL = T_LAT + T_CTX
NCHUNK_LAT = SEQ // CHUNK
NCHUNK_CTX = CTX_LEN // CHUNK
NSTEP = NCHUNK_LAT + NCHUNK_CTX

COL_AQ = 0
COL_AK = 1024
COL_AV = 1280
COL_Z = 1536
COL_XBC = 2560
COL_DT = 4096
COL_RQ = 4096
COL_RK = 5120
COL_RV = 6144
COL_RG = 7168
COL_GATES = 8192
P_COLS = 14336
DT_COLS = 2 * SSD_HEADS
DT_PAD = 128

TM_IN = 1024
RC_IN = 256
TN_IN = 1024
J_AQ, J_AKAV, J_RQ, J_RK = COL_AQ // TN_IN, COL_AK // TN_IN, COL_RQ // TN_IN, COL_RK // TN_IN
J_SPLIT = COL_DT // TN_IN
J_RG, J_GATES = COL_RG // TN_IN, COL_GATES // TN_IN
TM_MERGE = 256
TM_OUT = 1024
RC_OUT = 256
TM_MLP = 512
TF_MLP = 1024
RC_MLP = 256
TM_CONV = 1024
HALO = 16

VMEM_LIMIT = 56 * 1024 * 1024


def _cparams(sem):
    return pltpu.CompilerParams(dimension_semantics=sem, vmem_limit_bytes=VMEM_LIMIT)


def _sigmoid(x):
    return 0.5 * jnp.tanh(0.5 * x) + 0.5


def _silu(x):
    return x * _sigmoid(x)


def _softplus(x):
    return jnp.maximum(x, 0.0) + jnp.log1p(jnp.exp(-jnp.abs(x)))


def _ln_rows(x):
    mu = jnp.mean(x, axis=-1, keepdims=True)
    xc = x - mu
    var = jnp.mean(xc * xc, axis=-1, keepdims=True)
    return xc * lax.rsqrt(var + LN_EPS)


def _dot(a, b):
    return jnp.dot(a, b, preferred_element_type=F32)


def _dot_nt(a, b):
    return lax.dot_general(a, b, (((1,), (1,)), ((), ())), preferred_element_type=F32)


def _dot_tn(a, b):
    return lax.dot_general(a, b, (((0,), (0,)), ((), ())), preferred_element_type=F32)


def _split3(x):
    x1 = x.astype(BF16)
    r1 = x - x1.astype(F32)
    x2 = r1.astype(BF16)
    r2 = r1 - x2.astype(F32)
    return x1, x2, r2.astype(BF16)


def _batch_of_tile(i, tm):
    return jnp.where(i < T_LAT // tm, i // (SEQ // tm), BATCH)


TN_ADA = 2048


def _ada_kernel(cond_ref, w0_ref, w1_ref, b_ref, o_ref):
    s = _silu(cond_ref[...]).astype(BF16)
    kh = D_MODEL // 2
    o_ref[...] = (_dot(s[:, :kh], w0_ref[...].astype(BF16))
                  + _dot(s[:, kh:], w1_ref[...].astype(BF16)) + b_ref[...])


def _ada_call(cond, ada_w, ada_b):
    n = ada_w.shape[-1]
    return pl.pallas_call(
        _ada_kernel,
        grid=(DEPTH, n // TN_ADA),
        in_specs=[
            pl.BlockSpec((8, D_MODEL), lambda l, j: (0, 0)),
            pl.BlockSpec((None, D_MODEL // 2, TN_ADA), lambda l, j: (l, 0, j)),
            pl.BlockSpec((None, D_MODEL // 2, TN_ADA), lambda l, j: (l, 1, j)),
            pl.BlockSpec((None, 1, TN_ADA), lambda l, j: (l, 0, j)),
        ],
        out_specs=pl.BlockSpec((None, 8, TN_ADA), lambda l, j: (l, 0, j)),
        out_shape=jax.ShapeDtypeStruct((DEPTH, 8, n), F32),
        compiler_params=_cparams(("arbitrary", "arbitrary")),
        name="ada",
    )(cond, ada_w, ada_w, ada_b.reshape(DEPTH, 1, n))


def _rope_store(acc, cos, sin, o_ref, n_heads, scale, rows=slice(None)):
    for h in range(n_heads):
        xs = acc[:, h * HEAD_DIM:(h + 1) * HEAD_DIM]
        rot = pltpu.roll(xs, HEAD_DIM // 2, axis=1)
        o_ref[rows, h * HEAD_DIM:(h + 1) * HEAD_DIM] = ((xs * cos + rot * sin) * scale).astype(BF16)


def _inproj_kernel(skip_ctx_cols, tile0, has_prev, x_ref, shift_ref, scale_ref, wa_ref, wb_ref, wdt_ref,
                   cos_ref, sin_ref, *rest):
    p_ref, dt_ref, h_scr = rest[2:] if has_prev else rest
    i = pl.program_id(0) + tile0
    j = pl.program_id(1)
    chunks = [slice(r, r + RC_IN) for r in range(0, TM_IN, RC_IN)]
    heads_per_tile = TN_IN // HEAD_DIM
    if skip_ctx_cols:
        unused = (j == J_AQ) | (j == J_RG) | (j >= J_GATES)
        active = jnp.logical_not((i == T_ALL // TM_IN - 1) & unused)
    else:
        active = True

    def prologue(rows):
        h = _ln_rows(x_ref[rows, :]) * (1.0 + scale_ref[...]) + shift_ref[...]
        h_scr[rows, :] = h.astype(BF16)
        dt_ref[rows, :] = _dot(h_scr[rows, :], wdt_ref[...])

    def rotated(w_ref, scale):
        acc = _dot(h_scr[...], w_ref[...])
        _rope_store(acc, cos_ref[...], sin_ref[...], p_ref, heads_per_tile, scale)

    def plain(w_ref):
        p_ref[...] = _dot(h_scr[...], w_ref[...]).astype(BF16)

    @pl.when((j == J_AQ) & active)
    def _():
        for rows in chunks:
            prologue(rows)
            acc = _dot(h_scr[rows, :], wa_ref[...])
            _rope_store(acc, cos_ref[rows, :], sin_ref[rows, :], p_ref, heads_per_tile,
                        Q_SCALE * LOG2E, rows)

    if skip_ctx_cols:
        @pl.when((j == J_AQ) & jnp.logical_not(active))
        def _():
            for rows in chunks:
                prologue(rows)

    @pl.when(j == J_AKAV)
    def _():
        acc = _dot(h_scr[...], wa_ref[...])
        _rope_store(acc, cos_ref[...], sin_ref[...], p_ref, ATTN_HKV, 1.0)
        p_ref[:, ATTN_HKV * HEAD_DIM:] = acc[:, ATTN_HKV * HEAD_DIM:].astype(BF16)

    pl.when((j > J_AKAV) & (j < J_SPLIT))(lambda: plain(wa_ref))
    pl.when(j == J_RQ)(lambda: rotated(wb_ref, Q_SCALE))
    pl.when(j == J_RK)(lambda: rotated(wb_ref, 1.0))
    pl.when((j > J_RK) & active)(lambda: plain(wb_ref))


def _inproj_call(x_src, tile0, mod, w_pre, w_post, w_dt, cos2, sin2, skip_ctx_cols, prev=None):
    lat_tiles = T_LAT // TM_IN

    def mod_map(k):
        return lambda i, j: (_batch_of_tile(i + tile0, TM_IN) * 6 + k, 0, 0)

    def rope_map(i, j):
        g = i + tile0
        return (jnp.where(g < lat_tiles, g % (SEQ // TM_IN), SEQ // TM_IN), 0)

    in_specs = [
        pl.BlockSpec((TM_IN, D_MODEL), lambda i, j: (i, 0)),
        pl.BlockSpec((None, 1, D_MODEL), mod_map(0)),
        pl.BlockSpec((None, 1, D_MODEL), mod_map(1)),
        pl.BlockSpec((D_MODEL, TN_IN), lambda i, j: (0, jnp.minimum(j, J_SPLIT - 1))),
        pl.BlockSpec((D_MODEL, TN_IN), lambda i, j: (0, jnp.maximum(j, J_SPLIT))),
        pl.BlockSpec((D_MODEL, DT_PAD), lambda i, j: (0, 0)),
        pl.BlockSpec((TM_IN, HEAD_DIM), rope_map),
        pl.BlockSpec((TM_IN, HEAD_DIM), rope_map),
    ]
    args = [x_src, mod, mod, w_pre, w_post, w_dt, cos2, sin2]
    aliases = {}
    if prev is not None:
        aliases = {len(args): 0, len(args) + 1: 1}
        in_specs += [pl.BlockSpec(memory_space=pl.ANY)] * 2
        args += list(prev)
    return pl.pallas_call(
        functools.partial(_inproj_kernel, skip_ctx_cols, tile0, prev is not None),
        grid=(x_src.shape[0] // TM_IN, P_COLS // TN_IN),
        in_specs=in_specs,
        out_specs=[
            pl.BlockSpec((TM_IN, TN_IN), lambda i, j: (i + tile0, j)),
            pl.BlockSpec((TM_IN, DT_PAD), lambda i, j: (i + tile0, 0)),
        ],
        out_shape=[
            jax.ShapeDtypeStruct((T_ALL, P_COLS), BF16),
            jax.ShapeDtypeStruct((T_ALL, DT_PAD), F32),
        ],
        scratch_shapes=[pltpu.VMEM((TM_IN, D_MODEL), BF16)],
        input_output_aliases=aliases,
        compiler_params=_cparams(("arbitrary", "arbitrary")),
        name="inproj",
    )(*args)


TN_CONV = 512


def _conv_taps(prev, x, nxt, w_ref, b_ref, pos, slen):
    xe = jnp.concatenate([prev, x, nxt], axis=0)
    w = w_ref[...]
    acc = jnp.zeros((TM_CONV, TN_CONV), F32) + b_ref[...]
    half = SSD_CONV // 2
    for k in range(SSD_CONV):
        off = HALO - half + k
        xk = xe[off:off + TM_CONV, :]
        if pos is not None and k < half:
            xk = jnp.where(pos + (k - half) >= 0, xk, 0.0)
        elif pos is not None and k > half:
            xk = jnp.where(pos + (k - half) < slen, xk, 0.0)
        acc = acc + xk * w[k:k + 1, :]
    return _silu(acc).astype(BF16)


def _conv_kernel(x_ref, prev_ref, next_ref, w_ref, b_ref, o_ref):
    i = pl.program_id(0)
    tiles_per_seq = SEQ // TM_CONV
    x = x_ref[...].astype(F32)
    prev = prev_ref[...].astype(F32)
    nxt = next_ref[...].astype(F32)

    @pl.when(i < T_LAT // TM_CONV)
    def _():
        t = i & (tiles_per_seq - 1)
        o_ref[...] = _conv_taps(jnp.where(t == 0, 0.0, prev), x,
                                jnp.where(t == tiles_per_seq - 1, 0.0, nxt), w_ref, b_ref, None, None)

    @pl.when(i >= T_LAT // TM_CONV)
    def _():
        row = lax.broadcasted_iota(jnp.int32, (TM_CONV, TN_CONV), 0)
        o_ref[...] = _conv_taps(prev, x, nxt, w_ref, b_ref, row & (CTX_LEN - 1), CTX_LEN)


def _conv_call(p_all, conv_w, conv_b):
    cb0 = COL_XBC // TN_CONV
    hpt = TM_CONV // HALO
    last_halo = T_ALL // HALO - 1
    return pl.pallas_call(
        _conv_kernel,
        grid=(T_ALL // TM_CONV, SSD_CONV_CH // TN_CONV),
        in_specs=[
            pl.BlockSpec((TM_CONV, TN_CONV), lambda i, j: (i, cb0 + j)),
            pl.BlockSpec((HALO, TN_CONV), lambda i, j: (jnp.maximum(i * hpt - 1, 0), cb0 + j)),
            pl.BlockSpec((HALO, TN_CONV), lambda i, j: (jnp.minimum((i + 1) * hpt, last_halo), cb0 + j)),
            pl.BlockSpec((SSD_CONV, TN_CONV), lambda i, j: (0, j)),
            pl.BlockSpec((1, TN_CONV), lambda i, j: (0, j)),
        ],
        out_specs=pl.BlockSpec((TM_CONV, TN_CONV), lambda i, j: (i, j)),
        out_shape=jax.ShapeDtypeStruct((T_ALL, SSD_CONV_CH), BF16),
        compiler_params=_cparams(("arbitrary", "arbitrary")),
        name="conv",
    )(p_all, p_all, p_all, conv_w, conv_b.reshape(1, SSD_CONV_CH))


def _softmax_pv(s, vals, sink_col):
    m = jnp.maximum(jnp.max(s, axis=-1, keepdims=True), sink_col)
    p = jnp.exp2(s - m)
    l = jnp.sum(p, axis=-1, keepdims=True) + jnp.exp2(sink_col - m)
    return _dot(p.astype(BF16), vals) / l


def _group_queries(q_ref, hkv):
    h0 = hkv * ATTN_GROUP
    return jnp.concatenate(
        [q_ref[:, (h0 + g) * HEAD_DIM:(h0 + g + 1) * HEAD_DIM] for g in range(ATTN_GROUP)], axis=0)


def _sink_column(sink_ref, hkv, nq):
    return jnp.concatenate(
        [jnp.full((nq, 1), sink_ref[hkv * ATTN_GROUP + g] * LOG2E, F32) for g in range(ATTN_GROUP)], axis=0)


def _attn_kernel(sink_ref, q_ref, kp_ref, kc_ref, kn_ref, vp_ref, vc_ref, vn_ref, kx_ref, vx_ref,
                 o_ref):
    n = pl.program_id(1)
    rows = ATTN_GROUP * BLOCK
    qi = lax.broadcasted_iota(jnp.int32, (rows, BLOCK), 0) & (BLOCK - 1)
    kj = lax.broadcasted_iota(jnp.int32, (rows, BLOCK), 1)
    prev_ok = (kj >= qi) & (n > 0)
    next_ok = (kj <= qi) & (n < SEQ // BLOCK - 1)

    def mask_fn(s):
        return jnp.concatenate([
            jnp.where(prev_ok, s[:, :BLOCK], NEG_INF), s[:, BLOCK:2 * BLOCK],
            jnp.where(next_ok, s[:, 2 * BLOCK:3 * BLOCK], NEG_INF), s[:, 3 * BLOCK:]], axis=1)

    kv_cols = lambda hkv: slice(hkv * HEAD_DIM, (hkv + 1) * HEAD_DIM)
    scores = []
    for hkv in range(ATTN_HKV):
        c = kv_cols(hkv)
        keys = jnp.concatenate([kp_ref[:, c], kc_ref[:, c], kn_ref[:, c], kx_ref[:, c]], axis=0)
        scores.append(mask_fn(_dot_nt(_group_queries(q_ref, hkv), keys)))
    outs = []
    for hkv in range(ATTN_HKV):
        c = kv_cols(hkv)
        vals = jnp.concatenate([vp_ref[:, c], vc_ref[:, c], vn_ref[:, c], vx_ref[:, c]], axis=0)
        o = _softmax_pv(scores[hkv], vals, _sink_column(sink_ref, hkv, BLOCK))
        outs += [o[g * BLOCK:(g + 1) * BLOCK, :] for g in range(ATTN_GROUP)]
    o_ref[...] = jnp.concatenate(outs, axis=1).astype(BF16)


def _ctx_attn_kernel(sink_ref, q_ref, kx_ref, vx_ref, prev_ref, o_ref):
    del prev_ref
    kv_cols = lambda hkv: slice(hkv * HEAD_DIM, (hkv + 1) * HEAD_DIM)
    scores = [_dot_nt(_group_queries(q_ref, hkv), kx_ref[:, kv_cols(hkv)]) for hkv in range(ATTN_HKV)]
    outs = []
    for hkv in range(ATTN_HKV):
        o = _softmax_pv(scores[hkv], vx_ref[:, kv_cols(hkv)], _sink_column(sink_ref, hkv, CTX_LEN))
        outs += [o[g * CTX_LEN:(g + 1) * CTX_LEN, :] for g in range(ATTN_GROUP)]
    o_ref[...] = jnp.concatenate(outs, axis=1).astype(BF16)


def _attn_calls(p_all, sink, update_ctx):
    nb = SEQ // BLOCK
    kvw = ATTN_HKV * HEAD_DIM
    ck = COL_AK // kvw
    cv = COL_AV // kvw
    ctx0 = T_LAT // CTX_LEN
    smem = pl.BlockSpec(memory_space=pltpu.SMEM)

    def kv_spec(col, dn):
        return pl.BlockSpec((BLOCK, kvw), lambda b, n: (b * nb + jnp.clip(n + dn, 0, nb - 1), col))

    def ctx_spec(col):
        return pl.BlockSpec((CTX_LEN, kvw), lambda b, n: (ctx0 + b, col))

    attn = pl.pallas_call(
        _attn_kernel,
        grid=(BATCH, nb),
        in_specs=[
            smem,
            pl.BlockSpec((BLOCK, ATTN_WIDTH), lambda b, n: (b * nb + n, 0)),
            kv_spec(ck, -1), kv_spec(ck, 0), kv_spec(ck, 1),
            kv_spec(cv, -1), kv_spec(cv, 0), kv_spec(cv, 1),
            ctx_spec(ck), ctx_spec(cv),
        ],
        out_specs=pl.BlockSpec((BLOCK, ATTN_WIDTH), lambda b, n: (b * nb + n, 0)),
        out_shape=jax.ShapeDtypeStruct((T_ALL, ATTN_WIDTH), BF16),
        compiler_params=_cparams(("arbitrary", "arbitrary")),
        name="attn",
    )(sink, p_all, p_all, p_all, p_all, p_all, p_all, p_all, p_all, p_all)
    if not update_ctx:
        return attn
    return pl.pallas_call(
        _ctx_attn_kernel,
        grid=(BATCH,),
        in_specs=[
            smem,
            pl.BlockSpec((CTX_LEN, ATTN_WIDTH), lambda b: (ctx0 + b, 0)),
            pl.BlockSpec((CTX_LEN, kvw), lambda b: (ctx0 + b, ck)),
            pl.BlockSpec((CTX_LEN, kvw), lambda b: (ctx0 + b, cv)),
            pl.BlockSpec(memory_space=pl.ANY),
        ],
        out_specs=pl.BlockSpec((CTX_LEN, ATTN_WIDTH), lambda b: (ctx0 + b, 0)),
        out_shape=jax.ShapeDtypeStruct((T_ALL, ATTN_WIDTH), BF16),
        input_output_aliases={4: 0},
        compiler_params=_cparams(("arbitrary",)),
        name="ctx_attn",
    )(sink, p_all, p_all, p_all, attn)


def _fwd_chunk(b, s):
    ctx = T_LAT // CHUNK + b * NCHUNK_CTX + s
    lat = b * NCHUNK_LAT + (s - NCHUNK_CTX)
    return jnp.where(s < NCHUNK_CTX, ctx, lat)


def _bwd_chunk(b, s):
    ctx = T_LAT // CHUNK + b * NCHUNK_CTX + (NCHUNK_CTX - 1 - s)
    lat = b * NCHUNK_LAT + (NSTEP - 1 - s)
    return jnp.where(s < NCHUNK_CTX, ctx, lat)


def _ssd_kernel(xs_f, bm_f, cm_f, dt_f, xs_b, bm_b, cm_b, dt_b,
                bias_row, alog_row, bias_col, alog_col, yf_ref, yb_ref, h_scr):
    s = pl.program_id(1)

    @pl.when(s == 0)
    def _():
        h_scr[...] = jnp.zeros_like(h_scr)

    ii = lax.broadcasted_iota(jnp.int32, (CHUNK, CHUNK), 0)
    jj = lax.broadcasted_iota(jnp.int32, (CHUNK, CHUNK), 1)
    lane_lo = jj < SSD_P
    lower = jj <= ii
    upper = jj >= ii
    hg = SSD_HEADS // SSD_GROUPS
    dirs = ((xs_f, bm_f, cm_f, dt_f, yf_ref), (xs_b, bm_b, cm_b, dt_b, yb_ref))
    causal = (lower, upper)
    grp = lambda g: slice(g * SSD_STATE, (g + 1) * SSD_STATE)
    pair_lanes = lambda pair: slice(pair * 2 * SSD_P, (pair + 1) * 2 * SSD_P)

    acs, acs_t, row_t, dte_t, tot = {}, {}, {}, {}, {}
    for d in range(2):
        dt_ref = dirs[d][3]
        tri = jnp.where(causal[d], 1.0, 0.0).astype(BF16)
        tri_t = jnp.where(causal[1 - d], 1.0, 0.0).astype(BF16)
        last = CHUNK - 1 if d == 0 else 0
        r0 = d * SSD_HEADS
        dt_raw = dt_ref[...]
        dt_c = _softplus(dt_raw + bias_row[...])
        adt_c = dt_c * (-LOG2E * jnp.exp(alog_row[...]))
        acs[d] = sum(_dot(tri, p) for p in _split3(adt_c))
        dt_t = _softplus(dt_raw.T[r0:r0 + SSD_HEADS, :] + bias_col[r0:r0 + SSD_HEADS, :])
        adt_t = dt_t * (-LOG2E * jnp.exp(alog_col[r0:r0 + SSD_HEADS, :]))
        acs_t[d] = sum(_dot(p, tri_t) for p in _split3(adt_t))
        row_t[d] = acs_t[d] - jnp.log2(dt_t)
        dte_t[d] = jnp.exp2(acs_t[d][:, last:last + 1] - acs_t[d]) * dt_t
        tot[d] = acs[d][last:last + 1, :]

    cb, bm_t, cm = {}, {}, {}
    for d in range(2):
        _, bm_ref, cm_ref, _, _ = dirs[d]
        for g in range(SSD_GROUPS):
            cm[d, g] = cm_ref[:, grp(g)]
            cb[d, g] = _dot_nt(cm[d, g], bm_ref[:, grp(g)]).astype(BF16)
            bm_t[d, g] = bm_ref[:, grp(g)].astype(F32).T

    for d in range(2):
        xs_ref, y_ref = dirs[d][0], dirs[d][4]
        r0 = d * SSD_HEADS
        y_pairs, h_pairs = [], []
        for pair in range(SSD_HEADS // 2):
            g = pair // (hg // 2)
            x_pair = xs_ref[:, pair_lanes(pair)]
            h_pair = h_scr[d, :, pair_lanes(pair)]
            rhs = jnp.concatenate([x_pair, h_pair.astype(BF16)], axis=0)
            ys, ups, cds = [], [], []
            for h in (2 * pair, 2 * pair + 1):
                col = jnp.broadcast_to(acs[d][:, r0 + h:r0 + h + 1], (CHUNK, CHUNK))
                dec = jnp.exp2(jnp.where(causal[d], col - row_t[d][h:h + 1, :], NEG_INF))
                m_intra = cb[d, g] * dec.astype(BF16)
                m_state = cm[d, g] * jnp.exp2(col).astype(BF16)
                ys.append(_dot(jnp.concatenate([m_intra, m_state], axis=1), rhs))
                ups.append(_dot((bm_t[d, g] * dte_t[d][h:h + 1, :]).astype(BF16), x_pair))
                cds.append(jnp.exp2(tot[d][:, r0 + h:r0 + h + 1]))
            y_pairs.append(jnp.where(lane_lo, ys[0], ys[1]))
            h_pairs.append(jnp.where(lane_lo, cds[0], cds[1]) * h_pair
                           + jnp.where(lane_lo, ups[0], ups[1]))
        y_ref[...] = jnp.concatenate(y_pairs, axis=1).astype(BF16)
        h_scr[d] = jnp.concatenate(h_pairs, axis=1)


def _ssd_call(xbc, dt_all, bias_row, alog_row, bias_col, alog_col):
    bcol = SSD_INNER // (SSD_GROUPS * SSD_STATE)

    def specs(chunk_fn):
        return [
            pl.BlockSpec((CHUNK, SSD_INNER), lambda b, s: (chunk_fn(b, s), 0)),
            pl.BlockSpec((CHUNK, SSD_GROUPS * SSD_STATE), lambda b, s: (chunk_fn(b, s), bcol)),
            pl.BlockSpec((CHUNK, SSD_GROUPS * SSD_STATE), lambda b, s: (chunk_fn(b, s), bcol + 1)),
            pl.BlockSpec((CHUNK, DT_PAD), lambda b, s: (chunk_fn(b, s), 0)),
        ]

    const = lambda shape: pl.BlockSpec(shape, lambda b, s: (0, 0))
    return pl.pallas_call(
        _ssd_kernel,
        grid=(BATCH, NSTEP),
        in_specs=specs(_fwd_chunk) + specs(_bwd_chunk) + [
            const((1, DT_PAD)), const((1, DT_PAD)), const((DT_PAD, 1)), const((DT_PAD, 1))],
        out_specs=[
            pl.BlockSpec((CHUNK, SSD_INNER), lambda b, s: (_fwd_chunk(b, s), 0)),
            pl.BlockSpec((CHUNK, SSD_INNER), lambda b, s: (_bwd_chunk(b, s), 0)),
        ],
        out_shape=[jax.ShapeDtypeStruct((T_ALL, SSD_INNER), BF16)] * 2,
        scratch_shapes=[pltpu.VMEM((2, SSD_STATE, SSD_INNER), F32)],
        compiler_params=_cparams(("arbitrary", "arbitrary")),
        name="ssd",
    )(xbc, xbc, xbc, dt_all, xbc, xbc, xbc, dt_all, bias_row, alog_row, bias_col, alog_col)


def _ret_kernel(lg_ref, q_f, k_f, v_f, q_b, k_b, v_b, of_ref, ob_ref, s_scr, tab_scr):
    s = pl.program_id(1)

    @pl.when(s == 0)
    def _():
        s_scr[...] = jnp.zeros_like(s_scr)
        ii = lax.broadcasted_iota(jnp.int32, (CHUNK, CHUNK), 0)
        jj = lax.broadcasted_iota(jnp.int32, (CHUNK, CHUNK), 1)
        for d in range(2):
            if d == 0:
                dist = (ii - jj).astype(F32)
                row_pow = (ii + 1).astype(F32)
                key_pow = (CHUNK - 1 - ii).astype(F32)
            else:
                dist = (jj - ii).astype(F32)
                row_pow = (CHUNK - ii).astype(F32)
                key_pow = ii.astype(F32)
            for h in range(RET_HEADS):
                lg = lg_ref[d, h]
                tab_scr[d, h, 0] = jnp.where(dist >= 0.0, jnp.exp(jnp.maximum(dist, 0.0) * lg), 0.0)
                tab_scr[d, h, 1] = jnp.exp(row_pow * lg)
                tab_scr[d, h, 2] = jnp.exp(key_pow * lg)

    dirs = ((q_f, k_f, v_f, of_ref), (q_b, k_b, v_b, ob_ref))
    heads = [(d, h) for d in range(2) for h in range(RET_HEADS)]
    lanes = lambda h: slice(h * HEAD_DIM, (h + 1) * HEAD_DIM)
    raw = {}
    for d, h in heads:
        q_ref, k_ref, v_ref, _ = dirs[d]
        raw[d, h] = _dot_nt(q_ref[:, lanes(h)], k_ref[:, lanes(h)])
    kv = {}
    for d, h in heads:
        _, k_ref, v_ref, _ = dirs[d]
        k_dec = (k_ref[:, lanes(h)].astype(F32) * tab_scr[d, h, 2]).astype(BF16)
        kv[d, h] = _dot_tn(k_dec, v_ref[:, lanes(h)])
    outs = {}
    for d, h in heads:
        q_ref, _, v_ref, _ = dirs[d]
        scores = raw[d, h] * tab_scr[d, h, 0]
        q_cross = q_ref[:, lanes(h)].astype(F32) * tab_scr[d, h, 1]
        state = s_scr[d, h * HEAD_DIM:(h + 1) * HEAD_DIM, :]
        lhs = jnp.concatenate([scores.astype(BF16), q_cross.astype(BF16)], axis=1)
        rhs = jnp.concatenate([v_ref[:, lanes(h)], state.astype(BF16)], axis=0)
        outs[d, h] = _dot(lhs, rhs)
    for d in range(2):
        states = []
        for h in range(RET_HEADS):
            chunk_decay = jnp.exp(jnp.full((1, HEAD_DIM), float(CHUNK), F32) * lg_ref[d, h])
            states.append(chunk_decay * s_scr[d, h * HEAD_DIM:(h + 1) * HEAD_DIM, :] + kv[d, h])
        dirs[d][3][...] = jnp.concatenate([outs[d, h] for h in range(RET_HEADS)], axis=1).astype(BF16)
        s_scr[d] = jnp.concatenate(states, axis=0)


def _ret_call(p_all, log_decay):
    def specs(chunk_fn):
        return [
            pl.BlockSpec((CHUNK, RET_WIDTH), lambda b, s: (chunk_fn(b, s), COL_RQ // RET_WIDTH)),
            pl.BlockSpec((CHUNK, RET_WIDTH), lambda b, s: (chunk_fn(b, s), COL_RK // RET_WIDTH)),
            pl.BlockSpec((CHUNK, RET_WIDTH), lambda b, s: (chunk_fn(b, s), COL_RV // RET_WIDTH)),
        ]

    return pl.pallas_call(
        _ret_kernel,
        grid=(BATCH, NSTEP),
        in_specs=[pl.BlockSpec(memory_space=pltpu.SMEM)] + specs(_fwd_chunk) + specs(_bwd_chunk),
        out_specs=[
            pl.BlockSpec((CHUNK, RET_WIDTH), lambda b, s: (_fwd_chunk(b, s), 0)),
            pl.BlockSpec((CHUNK, RET_WIDTH), lambda b, s: (_bwd_chunk(b, s), 0)),
        ],
        out_shape=[jax.ShapeDtypeStruct((T_ALL, RET_WIDTH), BF16)] * 2,
        scratch_shapes=[pltpu.VMEM((2, RET_HEADS * HEAD_DIM, HEAD_DIM), F32),
                        pltpu.VMEM((2, RET_HEADS, 3, CHUNK, CHUNK), F32)],
        compiler_params=_cparams(("arbitrary", "arbitrary")),
        name="retention",
    )(log_decay, p_all, p_all, p_all, p_all, p_all, p_all)


def _merge_kernel(attn_ref, yf_ref, yb_ref, xs_ref, z0_ref, z1_ref, of_ref, ob_ref, rg_ref,
                  ga_ref, gs_ref, gr_ref, dskip_ref, ssd_g_ref, ret_g_ref,
                  wa_ref, ws_ref, wr_ref, o_ref):
    z = jnp.concatenate([z0_ref[...], z1_ref[...]], axis=1).astype(F32)
    y = (yf_ref[...].astype(F32) + yb_ref[...].astype(F32)
         + dskip_ref[...] * xs_ref[...].astype(F32))
    y = y * _silu(z)
    ssd_o = y * lax.rsqrt(jnp.mean(y * y, axis=-1, keepdims=True) + LN_EPS) * ssd_g_ref[...]

    o = of_ref[...].astype(F32) + ob_ref[...].astype(F32)
    normed = jnp.concatenate(
        [_ln_rows(o[:, h * HEAD_DIM:(h + 1) * HEAD_DIM]) for h in range(RET_HEADS)], axis=1)
    ret_o = normed * ret_g_ref[...] * _silu(rg_ref[...].astype(F32))

    merged = (_sigmoid(ga_ref[...].astype(F32)) * _dot(attn_ref[...], wa_ref[...])
              + _sigmoid(gs_ref[...].astype(F32)) * _dot(ssd_o.astype(BF16), ws_ref[...])
              + _sigmoid(gr_ref[...].astype(F32)) * _dot(ret_o.astype(BF16), wr_ref[...]))
    o_ref[...] = merged.astype(BF16)


def _merge_call(n_rows, attn, yf, yb, xbc, p_all, of, ob, dskip, ssd_g, ret_g, wa, ws, wr):
    tm = TM_MERGE
    zw = SSD_INNER // 2
    w1024 = lambda c: pl.BlockSpec((tm, 1024), lambda i: (i, c))
    zspec = lambda c: pl.BlockSpec((tm, zw), lambda i: (i, COL_Z // zw + c))
    gate = lambda c: pl.BlockSpec((tm, D_MODEL), lambda i: (i, COL_GATES // D_MODEL + c))
    vec = pl.BlockSpec((1, 1024), lambda i: (0, 0))
    wspec = pl.BlockSpec((1024, D_MODEL), lambda i: (0, 0), pipeline_mode=pl.Buffered(1))
    return pl.pallas_call(
        _merge_kernel,
        grid=(n_rows // tm,),
        in_specs=[
            w1024(0), w1024(0), w1024(0), w1024(0), zspec(0), zspec(1),
            w1024(0), w1024(0), w1024(COL_RG // 1024),
            gate(0), gate(1), gate(2), vec, vec, vec, wspec, wspec, wspec,
        ],
        out_specs=pl.BlockSpec((tm, D_MODEL), lambda i: (i, 0)),
        out_shape=jax.ShapeDtypeStruct((n_rows, D_MODEL), BF16),
        compiler_params=_cparams(("arbitrary",)),
        name="merge",
    )(attn, yf, yb, xbc, p_all, p_all, of, ob, p_all, p_all, p_all, p_all, dskip, ssd_g, ret_g,
      wa, ws, wr)


def _deepnorm(x, y, gate, g, b):
    return _ln_rows(DEEPNORM_ALPHA * x + gate * y) * g + b


def _outproj_kernel(m_ref, x_ref, w_ref, gate_ref, g_ref, b_ref, *rest):
    o_ref = rest[-1]
    for r in range(0, TM_OUT, RC_OUT):
        rows = slice(r, r + RC_OUT)
        mix = _dot(m_ref[rows, :], w_ref[...])
        o_ref[rows, :] = _deepnorm(x_ref[rows, :], mix, gate_ref[...], g_ref[...], b_ref[...])


def _outproj_call(n_rows, n_tiles, merged, x_src, tile0, w_out, mod, ln_g, ln_b, prev=None):
    tm = TM_OUT
    vec = pl.BlockSpec((1, D_MODEL), lambda i: (0, 0))
    in_specs = [
        pl.BlockSpec((tm, D_MODEL), lambda i: (i + tile0, 0)),
        pl.BlockSpec((tm, D_MODEL), lambda i: (i, 0)),
        pl.BlockSpec((D_MODEL, D_MODEL), lambda i: (0, 0), pipeline_mode=pl.Buffered(1)),
        pl.BlockSpec((None, 1, D_MODEL), lambda i: (_batch_of_tile(i + tile0, tm) * 6 + 2, 0, 0)),
        vec, vec,
    ]
    args = [merged, x_src, w_out, mod, ln_g, ln_b]
    aliases = {}
    if prev is not None:
        aliases = {len(args): 0}
        in_specs.append(pl.BlockSpec(memory_space=pl.ANY))
        args.append(prev)
    return pl.pallas_call(
        _outproj_kernel,
        grid=(n_tiles,),
        in_specs=in_specs,
        out_specs=pl.BlockSpec((tm, D_MODEL), lambda i: (i + tile0, 0)),
        out_shape=jax.ShapeDtypeStruct((n_rows, D_MODEL), F32),
        input_output_aliases=aliases,
        compiler_params=_cparams(("arbitrary",)),
        name="outproj",
    )(*args)


def _mlp_kernel(x_ref, shift_ref, scale_ref, gate_ref, wup_ref, wdn_ref, g_ref, b_ref,
                o_ref, h_scr, acc_scr):
    j = pl.program_id(1)
    last = pl.num_programs(1) - 1
    chunks = [slice(r, r + RC_MLP) for r in range(0, TM_MLP, RC_MLP)]

    def up_down(rows):
        u = jnp.maximum(_dot(h_scr[rows, :], wup_ref[...]), 0.0)
        return _dot((u * u).astype(BF16), wdn_ref[...])

    @pl.when(j == 0)
    def _():
        for rows in chunks:
            h = _ln_rows(x_ref[rows, :]) * (1.0 + scale_ref[...]) + shift_ref[...]
            h_scr[rows, :] = h.astype(BF16)
        for rows in chunks:
            acc_scr[rows, :] = up_down(rows)

    @pl.when((j > 0) & (j < last))
    def _():
        acc_scr[...] += up_down(slice(None))

    @pl.when(j == last)
    def _():
        for rows in chunks:
            y = acc_scr[rows, :] + up_down(rows)
            o_ref[rows, :] = _deepnorm(x_ref[rows, :], y, gate_ref[...], g_ref[...], b_ref[...])


def _mlp_call(n_rows, x1, mod, w_up, w_down, ln_g, ln_b):
    tm, tf = TM_MLP, TF_MLP

    def mod_map(k):
        return lambda i, j: (_batch_of_tile(i, tm) * 6 + k, 0, 0)

    vec = pl.BlockSpec((1, D_MODEL), lambda i, j: (0, 0))
    return pl.pallas_call(
        _mlp_kernel,
        grid=(n_rows // tm, D_FF // tf),
        in_specs=[
            pl.BlockSpec((tm, D_MODEL), lambda i, j: (i, 0)),
            pl.BlockSpec((None, 1, D_MODEL), mod_map(3)),
            pl.BlockSpec((None, 1, D_MODEL), mod_map(4)),
            pl.BlockSpec((None, 1, D_MODEL), mod_map(5)),
            pl.BlockSpec((D_MODEL, tf), lambda i, j: (0, j)),
            pl.BlockSpec((tf, D_MODEL), lambda i, j: (j, 0)),
            vec, vec,
        ],
        out_specs=pl.BlockSpec((tm, D_MODEL), lambda i, j: (i, 0)),
        out_shape=jax.ShapeDtypeStruct((n_rows, D_MODEL), F32),
        scratch_shapes=[pltpu.VMEM((tm, D_MODEL), BF16), pltpu.VMEM((tm, D_MODEL), F32)],
        compiler_params=_cparams(("arbitrary", "arbitrary")),
        name="mlp",
    )(x1, mod, mod, mod, w_up, w_down, ln_g, ln_b)


def _rope_tables():
    f32 = np.float32
    rows = SEQ // GRID_W
    row = np.repeat(np.arange(rows), GRID_W).astype(f32)
    col = (np.arange(rows * GRID_W) % GRID_W).astype(f32)
    n_freq = HEAD_DIM // 4
    inv = (f32(ROPE_BASE) ** (-np.arange(n_freq, dtype=f32) / f32(n_freq))).astype(f32)
    ang = np.concatenate([row[:, None] * inv, col[:, None] * inv], axis=-1).astype(f32)
    cos, sin = np.cos(ang).astype(f32), np.sin(ang).astype(f32)
    cos2 = np.concatenate([cos, cos], axis=-1)
    sin2 = np.concatenate([-sin, sin], axis=-1)
    cos2 = np.concatenate([cos2, np.ones((TM_IN, HEAD_DIM), f32)], axis=0)
    sin2 = np.concatenate([sin2, np.zeros((TM_IN, HEAD_DIM), f32)], axis=0)
    return jnp.asarray(cos2), jnp.asarray(sin2)


def _split_w_in(w):
    w_pre = w[:, :COL_DT].astype(BF16)
    w_post = w[:, DT_COLS:].astype(BF16)
    w_dt = jnp.pad(w[:, COL_DT:COL_DT + DT_COLS], ((0, 0), (0, DT_PAD - DT_COLS))).astype(BF16)
    return w_pre, w_post, w_dt


def _pad_heads(v):
    flat = jnp.pad(v.reshape(-1).astype(F32), (0, DT_PAD - DT_COLS))
    return flat.reshape(1, DT_PAD), flat.reshape(DT_PAD, 1)


def kernel(x, c, ctx, c_ctx, ada_w, ada_b, w_in, attn_sink, ssd_conv_w, ssd_conv_b, ssd_a_log,
           ssd_dt_bias, ssd_d, ssd_norm_g, ret_log_decay, ret_norm_g, w_branch_attn, w_branch_ssd,
           w_branch_ret, w_out, ln1_g, ln1_b, w_mlp_up, w_mlp_down, ln2_g, ln2_b):
    lat_tiles_in, lat_tiles_out = T_LAT // TM_IN, T_LAT // TM_OUT
    sources = [(x.reshape(T_LAT, D_MODEL), 0, 0), (ctx.reshape(T_CTX, D_MODEL), lat_tiles_in, lat_tiles_out)]
    cond = jnp.concatenate([c, c_ctx[None, :], jnp.zeros((8 - BATCH - 1, D_MODEL), F32)], axis=0)
    mod_all = _ada_call(cond, ada_w, ada_b).reshape(DEPTH, 8 * 6, 1, D_MODEL)
    cos2, sin2 = _rope_tables()

    for l in range(DEPTH):
        update_ctx = l < DEPTH - 1
        n_rows = T_ALL if update_ctx else T_LAT
        mod = mod_all[l]
        w_pre, w_post, w_dt = _split_w_in(w_in[l])
        proj = None
        for x_src, tile_in, _ in sources:
            proj = _inproj_call(x_src, tile_in, mod, w_pre, w_post, w_dt, cos2, sin2, not update_ctx, proj)
        p_all, dt_all = proj
        xbc = _conv_call(p_all, ssd_conv_w[l], ssd_conv_b[l])

        attn = _attn_calls(p_all, attn_sink[l].astype(F32), update_ctx)

        bias_row, bias_col = _pad_heads(ssd_dt_bias[l])
        alog_row, alog_col = _pad_heads(ssd_a_log[l])
        yf, yb = _ssd_call(xbc, dt_all, bias_row, alog_row, bias_col, alog_col)
        of, ob = _ret_call(p_all, ret_log_decay[l].astype(F32))

        dskip = jnp.repeat(ssd_d[l].astype(F32), SSD_P).reshape(1, SSD_INNER)
        merged = _merge_call(
            n_rows, attn, yf, yb, xbc, p_all, of, ob, dskip,
            ssd_norm_g[l].reshape(1, SSD_INNER), ret_norm_g[l].reshape(1, RET_WIDTH),
            w_branch_attn[l].astype(BF16), w_branch_ssd[l].astype(BF16), w_branch_ret[l].astype(BF16))
        w_out_b = w_out[l].astype(BF16)
        x1 = None
        for x_src, _, tile_out in sources:
            n_tiles = min(x_src.shape[0], n_rows - tile_out * TM_OUT) // TM_OUT
            x1 = _outproj_call(n_rows, n_tiles, merged, x_src, tile_out, w_out_b, mod,
                               ln1_g[l].reshape(1, D_MODEL), ln1_b[l].reshape(1, D_MODEL), x1)
        x_all = _mlp_call(n_rows, x1, mod, w_mlp_up[l].astype(BF16), w_mlp_down[l].astype(BF16),
                          ln2_g[l].reshape(1, D_MODEL), ln2_b[l].reshape(1, D_MODEL))
        sources = [(x_all, 0, 0)]
    return x_all.reshape(BATCH, SEQ, D_MODEL)
```

```python
import functools

import jax
import jax.numpy as jnp
from jax import lax
from jax.experimental import pallas as pl
from jax.experimental.pallas import tpu as pltpu
import numpy as np

F32 = jnp.float32
BF16 = jnp.bfloat16

D_MODEL = 2048
BATCH = 4
SEQ = 4096
DEPTH = 2
GRID_W = 64
CTX_LEN = 256
BLOCK = 128
HEAD_DIM = 128
ATTN_HQ = 8
ATTN_HKV = 2
ATTN_GROUP = ATTN_HQ // ATTN_HKV
ATTN_WIDTH = ATTN_HQ * HEAD_DIM
SSD_HEADS = 16
SSD_P = 64
SSD_INNER = SSD_HEADS * SSD_P
SSD_GROUPS = 2
SSD_STATE = 128
SSD_CONV = 5
SSD_CONV_CH = SSD_INNER + 2 * SSD_GROUPS * SSD_STATE
CHUNK = 128
RET_HEADS = 8
RET_WIDTH = RET_HEADS * HEAD_DIM
D_FF = 4 * D_MODEL
ROPE_BASE = 10000.0
DEEPNORM_ALPHA = (2 * DEPTH) ** 0.25
LN_EPS = 1e-6
NEG_INF = -1e30
Q_SCALE = HEAD_DIM ** -0.5
LOG2E = 1.4426950408889634

T_LAT = BATCH * SEQ
T_CTX = BATCH * CTX_LEN
T_ALL = T_LAT + T_CTX
NCHUNK_LAT = SEQ // CHUNK
NCHUNK_CTX = CTX_LEN // CHUNK
NSTEP = NCHUNK_LAT + NCHUNK_CTX

COL_AQ = 0
COL_AK = 1024
COL_AV = 1280
COL_Z = 1536
COL_XBC = 2560
COL_DT = 4096
COL_RQ = 4096
COL_RK = 5120
COL_RV = 6144
COL_RG = 7168
COL_GATES = 8192
P_COLS = 14336
DT_COLS = 2 * SSD_HEADS
DT_PAD = 128

TM_IN = 1024
RC_IN = 256
TN_IN = 1024
J_AQ, J_AKAV, J_RQ, J_RK = COL_AQ // TN_IN, COL_AK // TN_IN, COL_RQ // TN_IN, COL_RK // TN_IN
J_SPLIT = COL_DT // TN_IN
J_RG, J_GATES = COL_RG // TN_IN, COL_GATES // TN_IN
TM_MERGE = 256
TM_OUT = 1024
RC_OUT = 256
TM_MLP = 512
TF_MLP = 1024
RC_MLP = 256
TM_CONV = 1024
HALO = 16

VMEM_LIMIT = 56 * 1024 * 1024


def _cparams(sem):
    return pltpu.CompilerParams(dimension_semantics=sem, vmem_limit_bytes=VMEM_LIMIT)


def _sigmoid(x):
    return 0.5 * jnp.tanh(0.5 * x) + 0.5


def _silu(x):
    return x * _sigmoid(x)


def _softplus(x):
    return jnp.maximum(x, 0.0) + jnp.log1p(jnp.exp(-jnp.abs(x)))


def _ln_rows(x):
    mu = jnp.mean(x, axis=-1, keepdims=True)
    xc = x - mu
    var = jnp.mean(xc * xc, axis=-1, keepdims=True)
    return xc * lax.rsqrt(var + LN_EPS)


def _dot(a, b):
    return jnp.dot(a, b, preferred_element_type=F32)


def _dot_nt(a, b):
    return lax.dot_general(a, b, (((1,), (1,)), ((), ())), preferred_element_type=F32)


def _dot_tn(a, b):
    return lax.dot_general(a, b, (((0,), (0,)), ((), ())), preferred_element_type=F32)


def _split3(x):
    x1 = x.astype(BF16)
    r1 = x - x1.astype(F32)
    x2 = r1.astype(BF16)
    r2 = r1 - x2.astype(F32)
    return x1, x2, r2.astype(BF16)


def _batch_of_tile(i, tm):
    return jnp.where(i < T_LAT // tm, i // (SEQ // tm), BATCH)


CAST_BLOCK_BYTES = 8 * 1024 * 1024


def _cast_kernel(w_ref, o_ref):
    o_ref[...] = w_ref[...].astype(BF16)


def _cast_stacked(w):
    depth, rows, cols = w.shape
    rb = min(rows, CAST_BLOCK_BYTES // (cols * 4))
    spec = pl.BlockSpec((None, rb, cols), lambda l, i: (l, i, 0))
    return pl.pallas_call(
        _cast_kernel,
        grid=(depth, rows // rb),
        in_specs=[spec],
        out_specs=spec,
        out_shape=jax.ShapeDtypeStruct(w.shape, BF16),
        compiler_params=_cparams(("arbitrary", "arbitrary")),
        name="cast",
    )(w)


def _w_in_kernel(w_ref, pre_ref, post_ref, dt_ref):
    pre_ref[...] = w_ref[:, :COL_DT].astype(BF16)
    post_ref[...] = w_ref[:, COL_DT + DT_COLS:].astype(BF16)
    dt_cols = w_ref[:, COL_DT:COL_DT + DT_COLS]
    dt_ref[...] = jnp.concatenate(
        [dt_cols, jnp.zeros((dt_cols.shape[0], DT_PAD - DT_COLS), F32)], axis=1).astype(BF16)


def _cast_w_in(w_in):
    depth, rows, cols = w_in.shape
    rb = 128
    n_post = cols - COL_DT - DT_COLS
    spec = lambda width: pl.BlockSpec((None, rb, width), lambda l, i: (l, i, 0))
    return pl.pallas_call(
        _w_in_kernel,
        grid=(depth, rows // rb),
        in_specs=[spec(cols)],
        out_specs=[spec(COL_DT), spec(n_post), spec(DT_PAD)],
        out_shape=[jax.ShapeDtypeStruct((depth, rows, COL_DT), BF16),
                   jax.ShapeDtypeStruct((depth, rows, n_post), BF16),
                   jax.ShapeDtypeStruct((depth, rows, DT_PAD), BF16)],
        compiler_params=_cparams(("arbitrary", "arbitrary")),
        name="cast_w_in",
    )(w_in)


TN_ADA = 2048


def _ada_kernel(cond_ref, w0_ref, w1_ref, b_ref, o_ref):
    s = _silu(cond_ref[...]).astype(BF16)
    kh = D_MODEL // 2
    o_ref[...] = (_dot(s[:, :kh], w0_ref[...].astype(BF16))
                  + _dot(s[:, kh:], w1_ref[...].astype(BF16)) + b_ref[...])


def _ada_call(cond, ada_w, ada_b):
    n = ada_w.shape[-1]
    return pl.pallas_call(
        _ada_kernel,
        grid=(DEPTH, n // TN_ADA),
        in_specs=[
            pl.BlockSpec((8, D_MODEL), lambda l, j: (0, 0)),
            pl.BlockSpec((None, D_MODEL // 2, TN_ADA), lambda l, j: (l, 0, j)),
            pl.BlockSpec((None, D_MODEL // 2, TN_ADA), lambda l, j: (l, 1, j)),
            pl.BlockSpec((None, 1, TN_ADA), lambda l, j: (l, 0, j)),
        ],
        out_specs=pl.BlockSpec((None, 8, TN_ADA), lambda l, j: (l, 0, j)),
        out_shape=jax.ShapeDtypeStruct((DEPTH, 8, n), F32),
        compiler_params=_cparams(("arbitrary", "arbitrary")),
        name="ada",
    )(cond, ada_w, ada_w, ada_b.reshape(DEPTH, 1, n))


def _rope_store(acc, cos, sin, o_ref, n_heads, scale, rows=slice(None)):
    for h in range(n_heads):
        xs = acc[:, h * HEAD_DIM:(h + 1) * HEAD_DIM]
        rot = pltpu.roll(xs, HEAD_DIM // 2, axis=1)
        o_ref[rows, h * HEAD_DIM:(h + 1) * HEAD_DIM] = ((xs * cos + rot * sin) * scale).astype(BF16)


def _inproj_kernel(skip_ctx_cols, tile0, has_prev, x_ref, shift_ref, scale_ref, wa_ref, wb_ref, wdt_ref,
                   cos_ref, sin_ref, *rest):
    p_ref, dt_ref, h_scr = rest[2:] if has_prev else rest
    i = pl.program_id(0) + tile0
    j = pl.program_id(1)
    chunks = [slice(r, r + RC_IN) for r in range(0, TM_IN, RC_IN)]
    heads_per_tile = TN_IN // HEAD_DIM
    if skip_ctx_cols:
        unused = (j == J_AQ) | (j == J_RG) | (j >= J_GATES)
        active = jnp.logical_not((i == T_ALL // TM_IN - 1) & unused)
    else:
        active = True

    def prologue(rows):
        h = _ln_rows(x_ref[rows, :]) * (1.0 + scale_ref[...]) + shift_ref[...]
        h_scr[rows, :] = h.astype(BF16)
        dt_ref[rows, :] = _dot(h_scr[rows, :], wdt_ref[...])

    def rotated(w_ref, scale):
        acc = _dot(h_scr[...], w_ref[...])
        _rope_store(acc, cos_ref[...], sin_ref[...], p_ref, heads_per_tile, scale)

    def plain(w_ref):
        p_ref[...] = _dot(h_scr[...], w_ref[...]).astype(BF16)

    @pl.when((j == J_AQ) & active)
    def _():
        for rows in chunks:
            prologue(rows)
            acc = _dot(h_scr[rows, :], wa_ref[...])
            _rope_store(acc, cos_ref[rows, :], sin_ref[rows, :], p_ref, heads_per_tile,
                        Q_SCALE * LOG2E, rows)

    if skip_ctx_cols:
        @pl.when((j == J_AQ) & jnp.logical_not(active))
        def _():
            for rows in chunks:
                prologue(rows)

    @pl.when(j == J_AKAV)
    def _():
        acc = _dot(h_scr[...], wa_ref[...])
        _rope_store(acc, cos_ref[...], sin_ref[...], p_ref, ATTN_HKV, 1.0)
        p_ref[:, ATTN_HKV * HEAD_DIM:] = acc[:, ATTN_HKV * HEAD_DIM:].astype(BF16)

    pl.when((j > J_AKAV) & (j < J_SPLIT))(lambda: plain(wa_ref))
    pl.when(j == J_RQ)(lambda: rotated(wb_ref, Q_SCALE))
    pl.when(j == J_RK)(lambda: rotated(wb_ref, 1.0))
    pl.when((j > J_RK) & active)(lambda: plain(wb_ref))


def _inproj_call(l, x_src, tile0, mod, w_pre, w_post, w_dt, cos2, sin2, skip_ctx_cols, prev=None):
    lat_tiles = T_LAT // TM_IN

    def mod_map(k):
        return lambda i, j: (_batch_of_tile(i + tile0, TM_IN) * 6 + k, 0, 0)

    def rope_map(i, j):
        g = i + tile0
        return (jnp.where(g < lat_tiles, g % (SEQ // TM_IN), SEQ // TM_IN), 0)

    in_specs = [
        pl.BlockSpec((TM_IN, D_MODEL), lambda i, j: (i, 0)),
        pl.BlockSpec((None, 1, D_MODEL), mod_map(0)),
        pl.BlockSpec((None, 1, D_MODEL), mod_map(1)),
        pl.BlockSpec((None, D_MODEL, TN_IN), lambda i, j: (l, 0, jnp.minimum(j, J_SPLIT - 1))),
        pl.BlockSpec((None, D_MODEL, TN_IN), lambda i, j: (l, 0, jnp.maximum(j, J_SPLIT) - J_SPLIT)),
        pl.BlockSpec((None, D_MODEL, DT_PAD), lambda i, j: (l, 0, 0)),
        pl.BlockSpec((TM_IN, HEAD_DIM), rope_map),
        pl.BlockSpec((TM_IN, HEAD_DIM), rope_map),
    ]
    args = [x_src, mod, mod, w_pre, w_post, w_dt, cos2, sin2]
    aliases = {}
    if prev is not None:
        aliases = {len(args): 0, len(args) + 1: 1}
        in_specs += [pl.BlockSpec(memory_space=pl.ANY)] * 2
        args += list(prev)
    return pl.pallas_call(
        functools.partial(_inproj_kernel, skip_ctx_cols, tile0, prev is not None),
        grid=(x_src.shape[0] // TM_IN, P_COLS // TN_IN),
        in_specs=in_specs,
        out_specs=[
            pl.BlockSpec((TM_IN, TN_IN), lambda i, j: (i + tile0, j)),
            pl.BlockSpec((TM_IN, DT_PAD), lambda i, j: (i + tile0, 0)),
        ],
        out_shape=[
            jax.ShapeDtypeStruct((T_ALL, P_COLS), BF16),
            jax.ShapeDtypeStruct((T_ALL, DT_PAD), F32),
        ],
        scratch_shapes=[pltpu.VMEM((TM_IN, D_MODEL), BF16)],
        input_output_aliases=aliases,
        compiler_params=_cparams(("arbitrary", "arbitrary")),
        name="inproj",
    )(*args)


TN_CONV = 512


def _conv_taps(prev, x, nxt, w_ref, b_ref, pos, slen):
    xe = jnp.concatenate([prev, x, nxt], axis=0)
    w = w_ref[...]
    acc = jnp.zeros((TM_CONV, TN_CONV), F32) + b_ref[...]
    half = SSD_CONV // 2
    for k in range(SSD_CONV):
        off = HALO - half + k
        xk = xe[off:off + TM_CONV, :]
        if pos is not None and k < half:
            xk = jnp.where(pos + (k - half) >= 0, xk, 0.0)
        elif pos is not None and k > half:
            xk = jnp.where(pos + (k - half) < slen, xk, 0.0)
        acc = acc + xk * w[k:k + 1, :]
    return _silu(acc).astype(BF16)


def _conv_kernel(x_ref, prev_ref, next_ref, w_ref, b_ref, o_ref):
    i = pl.program_id(0)
    tiles_per_seq = SEQ // TM_CONV
    x = x_ref[...].astype(F32)
    prev = prev_ref[...].astype(F32)
    nxt = next_ref[...].astype(F32)

    @pl.when(i < T_LAT // TM_CONV)
    def _():
        t = i & (tiles_per_seq - 1)
        o_ref[...] = _conv_taps(jnp.where(t == 0, 0.0, prev), x,
                                jnp.where(t == tiles_per_seq - 1, 0.0, nxt), w_ref, b_ref, None, None)

    @pl.when(i >= T_LAT // TM_CONV)
    def _():
        row = lax.broadcasted_iota(jnp.int32, (TM_CONV, TN_CONV), 0)
        o_ref[...] = _conv_taps(prev, x, nxt, w_ref, b_ref, row & (CTX_LEN - 1), CTX_LEN)


def _conv_call(p_all, conv_w, conv_b):
    cb0 = COL_XBC // TN_CONV
    hpt = TM_CONV // HALO
    last_halo = T_ALL // HALO - 1
    return pl.pallas_call(
        _conv_kernel,
        grid=(T_ALL // TM_CONV, SSD_CONV_CH // TN_CONV),
        in_specs=[
            pl.BlockSpec((TM_CONV, TN_CONV), lambda i, j: (i, cb0 + j)),
            pl.BlockSpec((HALO, TN_CONV), lambda i, j: (jnp.maximum(i * hpt - 1, 0), cb0 + j)),
            pl.BlockSpec((HALO, TN_CONV), lambda i, j: (jnp.minimum((i + 1) * hpt, last_halo), cb0 + j)),
            pl.BlockSpec((SSD_CONV, TN_CONV), lambda i, j: (0, j)),
            pl.BlockSpec((1, TN_CONV), lambda i, j: (0, j)),
        ],
        out_specs=pl.BlockSpec((TM_CONV, TN_CONV), lambda i, j: (i, j)),
        out_shape=jax.ShapeDtypeStruct((T_ALL, SSD_CONV_CH), BF16),
        compiler_params=_cparams(("arbitrary", "arbitrary")),
        name="conv",
    )(p_all, p_all, p_all, conv_w, conv_b.reshape(1, SSD_CONV_CH))


def _softmax_pv(s, vals, sink_col):
    m = jnp.maximum(jnp.max(s, axis=-1, keepdims=True), sink_col)
    p = jnp.exp2(s - m)
    l = jnp.sum(p, axis=-1, keepdims=True) + jnp.exp2(sink_col - m)
    return _dot(p.astype(BF16), vals) / l


def _group_queries(q_ref, hkv):
    h0 = hkv * ATTN_GROUP
    return jnp.concatenate(
        [q_ref[:, (h0 + g) * HEAD_DIM:(h0 + g + 1) * HEAD_DIM] for g in range(ATTN_GROUP)], axis=0)


def _sink_column(sink_ref, hkv, nq):
    return jnp.concatenate(
        [jnp.full((nq, 1), sink_ref[hkv * ATTN_GROUP + g] * LOG2E, F32) for g in range(ATTN_GROUP)], axis=0)


def _attn_kernel(sink_ref, q_ref, kp_ref, kc_ref, kn_ref, vp_ref, vc_ref, vn_ref, kx_ref, vx_ref,
                 o_ref):
    n = pl.program_id(1)
    rows = ATTN_GROUP * BLOCK
    qi = lax.broadcasted_iota(jnp.int32, (rows, BLOCK), 0) & (BLOCK - 1)
    kj = lax.broadcasted_iota(jnp.int32, (rows, BLOCK), 1)
    prev_ok = (kj >= qi) & (n > 0)
    next_ok = (kj <= qi) & (n < SEQ // BLOCK - 1)

    def mask_fn(s):
        return jnp.concatenate([
            jnp.where(prev_ok, s[:, :BLOCK], NEG_INF), s[:, BLOCK:2 * BLOCK],
            jnp.where(next_ok, s[:, 2 * BLOCK:3 * BLOCK], NEG_INF), s[:, 3 * BLOCK:]], axis=1)

    kv_cols = lambda hkv: slice(hkv * HEAD_DIM, (hkv + 1) * HEAD_DIM)
    scores = []
    for hkv in range(ATTN_HKV):
        c = kv_cols(hkv)
        keys = jnp.concatenate([kp_ref[:, c], kc_ref[:, c], kn_ref[:, c], kx_ref[:, c]], axis=0)
        scores.append(mask_fn(_dot_nt(_group_queries(q_ref, hkv), keys)))
    outs = []
    for hkv in range(ATTN_HKV):
        c = kv_cols(hkv)
        vals = jnp.concatenate([vp_ref[:, c], vc_ref[:, c], vn_ref[:, c], vx_ref[:, c]], axis=0)
        o = _softmax_pv(scores[hkv], vals, _sink_column(sink_ref, hkv, BLOCK))
        outs += [o[g * BLOCK:(g + 1) * BLOCK, :] for g in range(ATTN_GROUP)]
    o_ref[...] = jnp.concatenate(outs, axis=1).astype(BF16)


def _ctx_attn_kernel(sink_ref, q_ref, kx_ref, vx_ref, prev_ref, o_ref):
    del prev_ref
    kv_cols = lambda hkv: slice(hkv * HEAD_DIM, (hkv + 1) * HEAD_DIM)
    scores = [_dot_nt(_group_queries(q_ref, hkv), kx_ref[:, kv_cols(hkv)]) for hkv in range(ATTN_HKV)]
    outs = []
    for hkv in range(ATTN_HKV):
        o = _softmax_pv(scores[hkv], vx_ref[:, kv_cols(hkv)], _sink_column(sink_ref, hkv, CTX_LEN))
        outs += [o[g * CTX_LEN:(g + 1) * CTX_LEN, :] for g in range(ATTN_GROUP)]
    o_ref[...] = jnp.concatenate(outs, axis=1).astype(BF16)


def _attn_calls(p_all, sink, update_ctx):
    nb = SEQ // BLOCK
    kvw = ATTN_HKV * HEAD_DIM
    ck = COL_AK // kvw
    cv = COL_AV // kvw
    ctx0 = T_LAT // CTX_LEN
    smem = pl.BlockSpec(memory_space=pltpu.SMEM)

    def kv_spec(col, dn):
        return pl.BlockSpec((BLOCK, kvw), lambda b, n: (b * nb + jnp.clip(n + dn, 0, nb - 1), col))

    def ctx_spec(col):
        return pl.BlockSpec((CTX_LEN, kvw), lambda b, n: (ctx0 + b, col))

    attn = pl.pallas_call(
        _attn_kernel,
        grid=(BATCH, nb),
        in_specs=[
            smem,
            pl.BlockSpec((BLOCK, ATTN_WIDTH), lambda b, n: (b * nb + n, 0)),
            kv_spec(ck, -1), kv_spec(ck, 0), kv_spec(ck, 1),
            kv_spec(cv, -1), kv_spec(cv, 0), kv_spec(cv, 1),
            ctx_spec(ck), ctx_spec(cv),
        ],
        out_specs=pl.BlockSpec((BLOCK, ATTN_WIDTH), lambda b, n: (b * nb + n, 0)),
        out_shape=jax.ShapeDtypeStruct((T_ALL, ATTN_WIDTH), BF16),
        compiler_params=_cparams(("arbitrary", "arbitrary")),
        name="attn",
    )(sink, p_all, p_all, p_all, p_all, p_all, p_all, p_all, p_all, p_all)
    if not update_ctx:
        return attn
    return pl.pallas_call(
        _ctx_attn_kernel,
        grid=(BATCH,),
        in_specs=[
            smem,
            pl.BlockSpec((CTX_LEN, ATTN_WIDTH), lambda b: (ctx0 + b, 0)),
            pl.BlockSpec((CTX_LEN, kvw), lambda b: (ctx0 + b, ck)),
            pl.BlockSpec((CTX_LEN, kvw), lambda b: (ctx0 + b, cv)),
            pl.BlockSpec(memory_space=pl.ANY),
        ],
        out_specs=pl.BlockSpec((CTX_LEN, ATTN_WIDTH), lambda b: (ctx0 + b, 0)),
        out_shape=jax.ShapeDtypeStruct((T_ALL, ATTN_WIDTH), BF16),
        input_output_aliases={4: 0},
        compiler_params=_cparams(("arbitrary",)),
        name="ctx_attn",
    )(sink, p_all, p_all, p_all, attn)


def _fwd_chunk(b, s):
    ctx = T_LAT // CHUNK + b * NCHUNK_CTX + s
    lat = b * NCHUNK_LAT + (s - NCHUNK_CTX)
    return jnp.where(s < NCHUNK_CTX, ctx, lat)


def _bwd_chunk(b, s):
    ctx = T_LAT // CHUNK + b * NCHUNK_CTX + (NCHUNK_CTX - 1 - s)
    lat = b * NCHUNK_LAT + (NSTEP - 1 - s)
    return jnp.where(s < NCHUNK_CTX, ctx, lat)


def _ssd_kernel(xs_f, bm_f, cm_f, dt_f, xs_b, bm_b, cm_b, dt_b,
                bias_row, alog_row, bias_col, alog_col, yf_ref, yb_ref, h_scr):
    s = pl.program_id(1)

    @pl.when(s == 0)
    def _():
        h_scr[...] = jnp.zeros_like(h_scr)

    ii = lax.broadcasted_iota(jnp.int32, (CHUNK, CHUNK), 0)
    jj = lax.broadcasted_iota(jnp.int32, (CHUNK, CHUNK), 1)
    lane_lo = jj < SSD_P
    lower = jj <= ii
    upper = jj >= ii
    hg = SSD_HEADS // SSD_GROUPS
    dirs = ((xs_f, bm_f, cm_f, dt_f, yf_ref), (xs_b, bm_b, cm_b, dt_b, yb_ref))
    causal = (lower, upper)
    grp = lambda g: slice(g * SSD_STATE, (g + 1) * SSD_STATE)
    pair_lanes = lambda pair: slice(pair * 2 * SSD_P, (pair + 1) * 2 * SSD_P)

    acs, acs_t, row_t, dte_t, tot = {}, {}, {}, {}, {}
    for d in range(2):
        dt_ref = dirs[d][3]
        tri = jnp.where(causal[d], 1.0, 0.0).astype(BF16)
        tri_t = jnp.where(causal[1 - d], 1.0, 0.0).astype(BF16)
        last = CHUNK - 1 if d == 0 else 0
        r0 = d * SSD_HEADS
        dt_raw = dt_ref[...]
        dt_c = _softplus(dt_raw + bias_row[...])
        adt_c = dt_c * (-LOG2E * jnp.exp(alog_row[...]))
        acs[d] = sum(_dot(tri, p) for p in _split3(adt_c))
        dt_t = _softplus(dt_raw.T[r0:r0 + SSD_HEADS, :] + bias_col[r0:r0 + SSD_HEADS, :])
        adt_t = dt_t * (-LOG2E * jnp.exp(alog_col[r0:r0 + SSD_HEADS, :]))
        acs_t[d] = sum(_dot(p, tri_t) for p in _split3(adt_t))
        row_t[d] = acs_t[d] - jnp.log2(dt_t)
        dte_t[d] = jnp.exp2(acs_t[d][:, last:last + 1] - acs_t[d]) * dt_t
        tot[d] = acs[d][last:last + 1, :]

    cb, bm_t, cm = {}, {}, {}
    for d in range(2):
        _, bm_ref, cm_ref, _, _ = dirs[d]
        for g in range(SSD_GROUPS):
            cm[d, g] = cm_ref[:, grp(g)]
            cb[d, g] = _dot_nt(cm[d, g], bm_ref[:, grp(g)]).astype(BF16)
            bm_t[d, g] = bm_ref[:, grp(g)].astype(F32).T

    for d in range(2):
        xs_ref, y_ref = dirs[d][0], dirs[d][4]
        r0 = d * SSD_HEADS
        y_pairs, h_pairs = [], []
        for pair in range(SSD_HEADS // 2):
            g = pair // (hg // 2)
            x_pair = xs_ref[:, pair_lanes(pair)]
            h_pair = h_scr[d, :, pair_lanes(pair)]
            rhs = jnp.concatenate([x_pair, h_pair.astype(BF16)], axis=0)
            ys, ups, cds = [], [], []
            for h in (2 * pair, 2 * pair + 1):
                col = jnp.broadcast_to(acs[d][:, r0 + h:r0 + h + 1], (CHUNK, CHUNK))
                dec = jnp.exp2(jnp.where(causal[d], col - row_t[d][h:h + 1, :], NEG_INF))
                m_intra = cb[d, g] * dec.astype(BF16)
                m_state = cm[d, g] * jnp.exp2(col).astype(BF16)
                ys.append(_dot(jnp.concatenate([m_intra, m_state], axis=1), rhs))
                ups.append(_dot((bm_t[d, g] * dte_t[d][h:h + 1, :]).astype(BF16), x_pair))
                cds.append(jnp.exp2(tot[d][:, r0 + h:r0 + h + 1]))
            y_pairs.append(jnp.where(lane_lo, ys[0], ys[1]))
            h_pairs.append(jnp.where(lane_lo, cds[0], cds[1]) * h_pair
                           + jnp.where(lane_lo, ups[0], ups[1]))
        y_ref[...] = jnp.concatenate(y_pairs, axis=1).astype(BF16)
        h_scr[d] = jnp.concatenate(h_pairs, axis=1)


def _ssd_call(xbc, dt_all, bias_row, alog_row, bias_col, alog_col):
    bcol = SSD_INNER // (SSD_GROUPS * SSD_STATE)

    def specs(chunk_fn):
        return [
            pl.BlockSpec((CHUNK, SSD_INNER), lambda b, s: (chunk_fn(b, s), 0)),
            pl.BlockSpec((CHUNK, SSD_GROUPS * SSD_STATE), lambda b, s: (chunk_fn(b, s), bcol)),
            pl.BlockSpec((CHUNK, SSD_GROUPS * SSD_STATE), lambda b, s: (chunk_fn(b, s), bcol + 1)),
            pl.BlockSpec((CHUNK, DT_PAD), lambda b, s: (chunk_fn(b, s), 0)),
        ]

    const = lambda shape: pl.BlockSpec(shape, lambda b, s: (0, 0))
    return pl.pallas_call(
        _ssd_kernel,
        grid=(BATCH, NSTEP),
        in_specs=specs(_fwd_chunk) + specs(_bwd_chunk) + [
            const((1, DT_PAD)), const((1, DT_PAD)), const((DT_PAD, 1)), const((DT_PAD, 1))],
        out_specs=[
            pl.BlockSpec((CHUNK, SSD_INNER), lambda b, s: (_fwd_chunk(b, s), 0)),
            pl.BlockSpec((CHUNK, SSD_INNER), lambda b, s: (_bwd_chunk(b, s), 0)),
        ],
        out_shape=[jax.ShapeDtypeStruct((T_ALL, SSD_INNER), BF16)] * 2,
        scratch_shapes=[pltpu.VMEM((2, SSD_STATE, SSD_INNER), F32)],
        compiler_params=_cparams(("arbitrary", "arbitrary")),
        name="ssd",
    )(xbc, xbc, xbc, dt_all, xbc, xbc, xbc, dt_all, bias_row, alog_row, bias_col, alog_col)


def _ret_kernel(lg_ref, q_f, k_f, v_f, q_b, k_b, v_b, of_ref, ob_ref, s_scr, tab_scr):
    s = pl.program_id(1)

    @pl.when(s == 0)
    def _():
        s_scr[...] = jnp.zeros_like(s_scr)
        ii = lax.broadcasted_iota(jnp.int32, (CHUNK, CHUNK), 0)
        jj = lax.broadcasted_iota(jnp.int32, (CHUNK, CHUNK), 1)
        for d in range(2):
            if d == 0:
                dist = (ii - jj).astype(F32)
                row_pow = (ii + 1).astype(F32)
                key_pow = (CHUNK - 1 - ii).astype(F32)
            else:
                dist = (jj - ii).astype(F32)
                row_pow = (CHUNK - ii).astype(F32)
                key_pow = ii.astype(F32)
            for h in range(RET_HEADS):
                lg = lg_ref[d, h]
                tab_scr[d, h, 0] = jnp.where(dist >= 0.0, jnp.exp(jnp.maximum(dist, 0.0) * lg), 0.0)
                tab_scr[d, h, 1] = jnp.exp(row_pow * lg)
                tab_scr[d, h, 2] = jnp.exp(key_pow * lg)

    dirs = ((q_f, k_f, v_f, of_ref), (q_b, k_b, v_b, ob_ref))
    heads = [(d, h) for d in range(2) for h in range(RET_HEADS)]
    lanes = lambda h: slice(h * HEAD_DIM, (h + 1) * HEAD_DIM)
    raw = {}
    for d, h in heads:
        q_ref, k_ref, v_ref, _ = dirs[d]
        raw[d, h] = _dot_nt(q_ref[:, lanes(h)], k_ref[:, lanes(h)])
    kv = {}
    for d, h in heads:
        _, k_ref, v_ref, _ = dirs[d]
        k_dec = (k_ref[:, lanes(h)].astype(F32) * tab_scr[d, h, 2]).astype(BF16)
        kv[d, h] = _dot_tn(k_dec, v_ref[:, lanes(h)])
    outs = {}
    for d, h in heads:
        q_ref, _, v_ref, _ = dirs[d]
        scores = raw[d, h] * tab_scr[d, h, 0]
        q_cross = q_ref[:, lanes(h)].astype(F32) * tab_scr[d, h, 1]
        state = s_scr[d, h * HEAD_DIM:(h + 1) * HEAD_DIM, :]
        lhs = jnp.concatenate([scores.astype(BF16), q_cross.astype(BF16)], axis=1)
        rhs = jnp.concatenate([v_ref[:, lanes(h)], state.astype(BF16)], axis=0)
        outs[d, h] = _dot(lhs, rhs)
    for d in range(2):
        states = []
        for h in range(RET_HEADS):
            chunk_decay = jnp.exp(jnp.full((1, HEAD_DIM), float(CHUNK), F32) * lg_ref[d, h])
            states.append(chunk_decay * s_scr[d, h * HEAD_DIM:(h + 1) * HEAD_DIM, :] + kv[d, h])
        dirs[d][3][...] = jnp.concatenate([outs[d, h] for h in range(RET_HEADS)], axis=1).astype(BF16)
        s_scr[d] = jnp.concatenate(states, axis=0)


def _ret_call(p_all, log_decay):
    def specs(chunk_fn):
        return [
            pl.BlockSpec((CHUNK, RET_WIDTH), lambda b, s: (chunk_fn(b, s), COL_RQ // RET_WIDTH)),
            pl.BlockSpec((CHUNK, RET_WIDTH), lambda b, s: (chunk_fn(b, s), COL_RK // RET_WIDTH)),
            pl.BlockSpec((CHUNK, RET_WIDTH), lambda b, s: (chunk_fn(b, s), COL_RV // RET_WIDTH)),
        ]

    return pl.pallas_call(
        _ret_kernel,
        grid=(BATCH, NSTEP),
        in_specs=[pl.BlockSpec(memory_space=pltpu.SMEM)] + specs(_fwd_chunk) + specs(_bwd_chunk),
        out_specs=[
            pl.BlockSpec((CHUNK, RET_WIDTH), lambda b, s: (_fwd_chunk(b, s), 0)),
            pl.BlockSpec((CHUNK, RET_WIDTH), lambda b, s: (_bwd_chunk(b, s), 0)),
        ],
        out_shape=[jax.ShapeDtypeStruct((T_ALL, RET_WIDTH), BF16)] * 2,
        scratch_shapes=[pltpu.VMEM((2, RET_HEADS * HEAD_DIM, HEAD_DIM), F32),
                        pltpu.VMEM((2, RET_HEADS, 3, CHUNK, CHUNK), F32)],
        compiler_params=_cparams(("arbitrary", "arbitrary")),
        name="retention",
    )(log_decay, p_all, p_all, p_all, p_all, p_all, p_all)


def _merge_kernel(attn_ref, yf_ref, yb_ref, xs_ref, z0_ref, z1_ref, of_ref, ob_ref, rg_ref,
                  ga_ref, gs_ref, gr_ref, dskip_ref, ssd_g_ref, ret_g_ref,
                  wa_ref, ws_ref, wr_ref, o_ref):
    z = jnp.concatenate([z0_ref[...], z1_ref[...]], axis=1).astype(F32)
    y = (yf_ref[...].astype(F32) + yb_ref[...].astype(F32)
         + dskip_ref[...] * xs_ref[...].astype(F32))
    y = y * _silu(z)
    ssd_o = y * lax.rsqrt(jnp.mean(y * y, axis=-1, keepdims=True) + LN_EPS) * ssd_g_ref[...]

    o = of_ref[...].astype(F32) + ob_ref[...].astype(F32)
    normed = jnp.concatenate(
        [_ln_rows(o[:, h * HEAD_DIM:(h + 1) * HEAD_DIM]) for h in range(RET_HEADS)], axis=1)
    ret_o = normed * ret_g_ref[...] * _silu(rg_ref[...].astype(F32))

    merged = (_sigmoid(ga_ref[...].astype(F32)) * _dot(attn_ref[...], wa_ref[...])
              + _sigmoid(gs_ref[...].astype(F32)) * _dot(ssd_o.astype(BF16), ws_ref[...])
              + _sigmoid(gr_ref[...].astype(F32)) * _dot(ret_o.astype(BF16), wr_ref[...]))
    o_ref[...] = merged.astype(BF16)


def _merge_call(l, n_rows, attn, yf, yb, xbc, p_all, of, ob, dskip, ssd_g, ret_g, wa, ws, wr):
    tm = TM_MERGE
    zw = SSD_INNER // 2
    w1024 = lambda c: pl.BlockSpec((tm, 1024), lambda i: (i, c))
    zspec = lambda c: pl.BlockSpec((tm, zw), lambda i: (i, COL_Z // zw + c))
    gate = lambda c: pl.BlockSpec((tm, D_MODEL), lambda i: (i, COL_GATES // D_MODEL + c))
    vec = pl.BlockSpec((1, 1024), lambda i: (0, 0))
    wspec = pl.BlockSpec((None, 1024, D_MODEL), lambda i: (l, 0, 0), pipeline_mode=pl.Buffered(1))
    return pl.pallas_call(
        _merge_kernel,
        grid=(n_rows // tm,),
        in_specs=[
            w1024(0), w1024(0), w1024(0), w1024(0), zspec(0), zspec(1),
            w1024(0), w1024(0), w1024(COL_RG // 1024),
            gate(0), gate(1), gate(2), vec, vec, vec, wspec, wspec, wspec,
        ],
        out_specs=pl.BlockSpec((tm, D_MODEL), lambda i: (i, 0)),
        out_shape=jax.ShapeDtypeStruct((n_rows, D_MODEL), BF16),
        compiler_params=_cparams(("arbitrary",)),
        name="merge",
    )(attn, yf, yb, xbc, p_all, p_all, of, ob, p_all, p_all, p_all, p_all, dskip, ssd_g, ret_g,
      wa, ws, wr)


def _deepnorm(x, y, gate, g, b):
    return _ln_rows(DEEPNORM_ALPHA * x + gate * y) * g + b


def _outproj_kernel(m_ref, x_ref, w_ref, gate_ref, g_ref, b_ref, *rest):
    o_ref = rest[-1]
    for r in range(0, TM_OUT, RC_OUT):
        rows = slice(r, r + RC_OUT)
        mix = _dot(m_ref[rows, :], w_ref[...])
        o_ref[rows, :] = _deepnorm(x_ref[rows, :], mix, gate_ref[...], g_ref[...], b_ref[...])


def _outproj_call(l, n_rows, n_tiles, merged, x_src, tile0, w_out, mod, ln_g, ln_b, prev=None):
    tm = TM_OUT
    vec = pl.BlockSpec((1, D_MODEL), lambda i: (0, 0))
    in_specs = [
        pl.BlockSpec((tm, D_MODEL), lambda i: (i + tile0, 0)),
        pl.BlockSpec((tm, D_MODEL), lambda i: (i, 0)),
        pl.BlockSpec((None, D_MODEL, D_MODEL), lambda i: (l, 0, 0), pipeline_mode=pl.Buffered(1)),
        pl.BlockSpec((None, 1, D_MODEL), lambda i: (_batch_of_tile(i + tile0, tm) * 6 + 2, 0, 0)),
        vec, vec,
    ]
    args = [merged, x_src, w_out, mod, ln_g, ln_b]
    aliases = {}
    if prev is not None:
        aliases = {len(args): 0}
        in_specs.append(pl.BlockSpec(memory_space=pl.ANY))
        args.append(prev)
    return pl.pallas_call(
        _outproj_kernel,
        grid=(n_tiles,),
        in_specs=in_specs,
        out_specs=pl.BlockSpec((tm, D_MODEL), lambda i: (i + tile0, 0)),
        out_shape=jax.ShapeDtypeStruct((n_rows, D_MODEL), F32),
        input_output_aliases=aliases,
        compiler_params=_cparams(("arbitrary",)),
        name="outproj",
    )(*args)


def _mlp_kernel(x_ref, shift_ref, scale_ref, gate_ref, wup_ref, wdn_ref, g_ref, b_ref,
                o_ref, h_scr, acc_scr):
    j = pl.program_id(1)
    last = pl.num_programs(1) - 1
    chunks = [slice(r, r + RC_MLP) for r in range(0, TM_MLP, RC_MLP)]

    def up_down(rows):
        u = jnp.maximum(_dot(h_scr[rows, :], wup_ref[...]), 0.0)
        return _dot((u * u).astype(BF16), wdn_ref[...])

    @pl.when(j == 0)
    def _():
        for rows in chunks:
            h = _ln_rows(x_ref[rows, :]) * (1.0 + scale_ref[...]) + shift_ref[...]
            h_scr[rows, :] = h.astype(BF16)
        for rows in chunks:
            acc_scr[rows, :] = up_down(rows)

    @pl.when((j > 0) & (j < last))
    def _():
        acc_scr[...] += up_down(slice(None))

    @pl.when(j == last)
    def _():
        for rows in chunks:
            y = acc_scr[rows, :] + up_down(rows)
            o_ref[rows, :] = _deepnorm(x_ref[rows, :], y, gate_ref[...], g_ref[...], b_ref[...])


def _mlp_call(l, n_rows, x1, mod, w_up, w_down, ln_g, ln_b):
    tm, tf = TM_MLP, TF_MLP

    def mod_map(k):
        return lambda i, j: (_batch_of_tile(i, tm) * 6 + k, 0, 0)

    vec = pl.BlockSpec((1, D_MODEL), lambda i, j: (0, 0))
    return pl.pallas_call(
        _mlp_kernel,
        grid=(n_rows // tm, D_FF // tf),
        in_specs=[
            pl.BlockSpec((tm, D_MODEL), lambda i, j: (i, 0)),
            pl.BlockSpec((None, 1, D_MODEL), mod_map(3)),
            pl.BlockSpec((None, 1, D_MODEL), mod_map(4)),
            pl.BlockSpec((None, 1, D_MODEL), mod_map(5)),
            pl.BlockSpec((None, D_MODEL, tf), lambda i, j: (l, 0, j)),
            pl.BlockSpec((None, tf, D_MODEL), lambda i, j: (l, j, 0)),
            vec, vec,
        ],
        out_specs=pl.BlockSpec((tm, D_MODEL), lambda i, j: (i, 0)),
        out_shape=jax.ShapeDtypeStruct((n_rows, D_MODEL), F32),
        scratch_shapes=[pltpu.VMEM((tm, D_MODEL), BF16), pltpu.VMEM((tm, D_MODEL), F32)],
        compiler_params=_cparams(("arbitrary", "arbitrary")),
        name="mlp",
    )(x1, mod, mod, mod, w_up, w_down, ln_g, ln_b)


def _rope_tables():
    f32 = np.float32
    rows = SEQ // GRID_W
    row = np.repeat(np.arange(rows), GRID_W).astype(f32)
    col = (np.arange(rows * GRID_W) % GRID_W).astype(f32)
    n_freq = HEAD_DIM // 4
    inv = (f32(ROPE_BASE) ** (-np.arange(n_freq, dtype=f32) / f32(n_freq))).astype(f32)
    ang = np.concatenate([row[:, None] * inv, col[:, None] * inv], axis=-1).astype(f32)
    cos, sin = np.cos(ang).astype(f32), np.sin(ang).astype(f32)
    cos2 = np.concatenate([cos, cos], axis=-1)
    sin2 = np.concatenate([-sin, sin], axis=-1)
    cos2 = np.concatenate([cos2, np.ones((TM_IN, HEAD_DIM), f32)], axis=0)
    sin2 = np.concatenate([sin2, np.zeros((TM_IN, HEAD_DIM), f32)], axis=0)
    return jnp.asarray(cos2), jnp.asarray(sin2)


def _split_w_in(w):
    w_pre = w[:, :COL_DT].astype(BF16)
    w_post = w[:, DT_COLS:].astype(BF16)
    w_dt = jnp.pad(w[:, COL_DT:COL_DT + DT_COLS], ((0, 0), (0, DT_PAD - DT_COLS))).astype(BF16)
    return w_pre, w_post, w_dt


def _pad_heads(v):
    flat = jnp.pad(v.reshape(-1).astype(F32), (0, DT_PAD - DT_COLS))
    return flat.reshape(1, DT_PAD), flat.reshape(DT_PAD, 1)


def kernel(x, c, ctx, c_ctx, ada_w, ada_b, w_in, attn_sink, ssd_conv_w, ssd_conv_b, ssd_a_log,
           ssd_dt_bias, ssd_d, ssd_norm_g, ret_log_decay, ret_norm_g, w_branch_attn, w_branch_ssd,
           w_branch_ret, w_out, ln1_g, ln1_b, w_mlp_up, w_mlp_down, ln2_g, ln2_b):
    lat_tiles_in, lat_tiles_out = T_LAT // TM_IN, T_LAT // TM_OUT
    sources = [(x.reshape(T_LAT, D_MODEL), 0, 0), (ctx.reshape(T_CTX, D_MODEL), lat_tiles_in, lat_tiles_out)]
    cond = jnp.concatenate([c, c_ctx[None, :], jnp.zeros((8 - BATCH - 1, D_MODEL), F32)], axis=0)
    mod_all = _ada_call(cond, ada_w, ada_b).reshape(DEPTH, 8 * 6, 1, D_MODEL)
    cos2, sin2 = _rope_tables()
    wb_attn, wb_ssd, wb_ret = (_cast_stacked(w) for w in (w_branch_attn, w_branch_ssd, w_branch_ret))
    w_out_b, w_up_b, w_down_b = _cast_stacked(w_out), _cast_stacked(w_mlp_up), _cast_stacked(w_mlp_down)
    w_pre, w_post, w_dt = _cast_w_in(w_in)

    for l in range(DEPTH):
        update_ctx = l < DEPTH - 1
        n_rows = T_ALL if update_ctx else T_LAT
        mod = mod_all[l]
        proj = None
        for x_src, tile_in, _ in sources:
            proj = _inproj_call(l, x_src, tile_in, mod, w_pre, w_post, w_dt, cos2, sin2, not update_ctx, proj)
        p_all, dt_all = proj
        xbc = _conv_call(p_all, ssd_conv_w[l], ssd_conv_b[l])

        attn = _attn_calls(p_all, attn_sink[l].astype(F32), update_ctx)

        bias_row, bias_col = _pad_heads(ssd_dt_bias[l])
        alog_row, alog_col = _pad_heads(ssd_a_log[l])
        yf, yb = _ssd_call(xbc, dt_all, bias_row, alog_row, bias_col, alog_col)
        of, ob = _ret_call(p_all, ret_log_decay[l].astype(F32))

        dskip = jnp.repeat(ssd_d[l].astype(F32), SSD_P).reshape(1, SSD_INNER)
        merged = _merge_call(
            l, n_rows, attn, yf, yb, xbc, p_all, of, ob, dskip,
            ssd_norm_g[l].reshape(1, SSD_INNER), ret_norm_g[l].reshape(1, RET_WIDTH), wb_attn, wb_ssd, wb_ret)
        x1 = None
        for x_src, _, tile_out in sources:
            n_tiles = min(x_src.shape[0], n_rows - tile_out * TM_OUT) // TM_OUT
            x1 = _outproj_call(l, n_rows, n_tiles, merged, x_src, tile_out, w_out_b, mod,
                               ln1_g[l].reshape(1, D_MODEL), ln1_b[l].reshape(1, D_MODEL), x1)
        x_all = _mlp_call(l, n_rows, x1, mod, w_up_b, w_down_b,
                          ln2_g[l].reshape(1, D_MODEL), ln2_b[l].reshape(1, D_MODEL))
        sources = [(x_all, 0, 0)]
    return x_all.reshape(BATCH, SEQ, D_MODEL)
```

```python
import functools

import jax
import jax.numpy as jnp
from jax import lax
from jax.experimental import pallas as pl
from jax.experimental.pallas import tpu as pltpu
import numpy as np

F32 = jnp.float32
BF16 = jnp.bfloat16

D_MODEL = 2048
BATCH = 4
SEQ = 4096
DEPTH = 2
GRID_W = 64
CTX_LEN = 256
BLOCK = 128
HEAD_DIM = 128
ATTN_HQ = 8
ATTN_HKV = 2
ATTN_GROUP = ATTN_HQ // ATTN_HKV
ATTN_WIDTH = ATTN_HQ * HEAD_DIM
SSD_HEADS = 16
SSD_P = 64
SSD_INNER = SSD_HEADS * SSD_P
SSD_GROUPS = 2
SSD_STATE = 128
SSD_CONV = 5
SSD_CONV_CH = SSD_INNER + 2 * SSD_GROUPS * SSD_STATE
CHUNK = 128
RET_HEADS = 8
RET_WIDTH = RET_HEADS * HEAD_DIM
D_FF = 4 * D_MODEL
ROPE_BASE = 10000.0
DEEPNORM_ALPHA = (2 * DEPTH) ** 0.25
LN_EPS = 1e-6
NEG_INF = -1e30
Q_SCALE = HEAD_DIM ** -0.5
LOG2E = 1.4426950408889634

T_LAT = BATCH * SEQ
T_CTX = BATCH * CTX_LEN
T_ALL = T_LAT + T_CTX
NCHUNK_LAT = SEQ // CHUNK
NCHUNK_CTX = CTX_LEN // CHUNK
NSTEP = NCHUNK_LAT + NCHUNK_CTX

COL_AQ = 0
COL_AK = 1024
COL_AV = 1280
COL_Z = 1536
COL_XBC = 2560
COL_DT = 4096
COL_RQ = 4096
COL_RK = 5120
COL_RV = 6144
COL_RG = 7168
COL_GATES = 8192
P_COLS = 14336
DT_COLS = 2 * SSD_HEADS
DT_PAD = 128

TM_IN = 1024
RC_IN = 256
TN_IN = 1024
J_AQ, J_AKAV, J_RQ, J_RK = COL_AQ // TN_IN, COL_AK // TN_IN, COL_RQ // TN_IN, COL_RK // TN_IN
J_SPLIT = COL_DT // TN_IN
J_RG, J_GATES = COL_RG // TN_IN, COL_GATES // TN_IN
TM_MERGE = 256
TM_OUT = 1024
RC_OUT = 256
TM_MLP = 512
TF_MLP = 1024
RC_MLP = 256
TM_CONV = 1024
HALO = 16

VMEM_LIMIT = 56 * 1024 * 1024


def _cparams(sem):
    return pltpu.CompilerParams(dimension_semantics=sem, vmem_limit_bytes=VMEM_LIMIT)


def _sigmoid(x):
    return 0.5 * jnp.tanh(0.5 * x) + 0.5


def _silu(x):
    return x * _sigmoid(x)


def _softplus(x):
    return jnp.maximum(x, 0.0) + jnp.log1p(jnp.exp(-jnp.abs(x)))


def _ln_rows(x):
    mu = jnp.mean(x, axis=-1, keepdims=True)
    xc = x - mu
    var = jnp.mean(xc * xc, axis=-1, keepdims=True)
    return xc * lax.rsqrt(var + LN_EPS)


def _dot(a, b):
    return jnp.dot(a, b, preferred_element_type=F32)


def _dot_nt(a, b):
    return lax.dot_general(a, b, (((1,), (1,)), ((), ())), preferred_element_type=F32)


def _dot_tn(a, b):
    return lax.dot_general(a, b, (((0,), (0,)), ((), ())), preferred_element_type=F32)


def _split3(x):
    x1 = x.astype(BF16)
    r1 = x - x1.astype(F32)
    x2 = r1.astype(BF16)
    r2 = r1 - x2.astype(F32)
    return x1, x2, r2.astype(BF16)


def _batch_of_tile(i, tm):
    return jnp.where(i < T_LAT // tm, i // (SEQ // tm), BATCH)


CAST_BLOCK_BYTES = 8 * 1024 * 1024


def _cast_kernel(w_ref, o_ref):
    o_ref[...] = w_ref[...].astype(BF16)


def _cast_stacked(w):
    depth, rows, cols = w.shape
    rb = min(rows, CAST_BLOCK_BYTES // (cols * 4))
    spec = pl.BlockSpec((None, rb, cols), lambda l, i: (l, i, 0))
    return pl.pallas_call(
        _cast_kernel,
        grid=(depth, rows // rb),
        in_specs=[spec],
        out_specs=spec,
        out_shape=jax.ShapeDtypeStruct(w.shape, BF16),
        compiler_params=_cparams(("arbitrary", "arbitrary")),
        name="cast",
    )(w)


def _cast_w_in(w_in):
    w_t = jnp.swapaxes(w_in, 1, 2)
    depth, cols, k = w_t.shape

    def src_map(l, j):
        return (l, (j * (TN_IN // DT_COLS) + jnp.where(j >= J_SPLIT, 1, 0)) * DT_COLS, 0)

    w_main = pl.pallas_call(
        _cast_kernel,
        grid=(depth, P_COLS // TN_IN),
        in_specs=[pl.BlockSpec((pl.Squeezed(), pl.Element(TN_IN), pl.Element(k)), src_map)],
        out_specs=pl.BlockSpec((None, TN_IN, k), lambda l, j: (l, j, 0)),
        out_shape=jax.ShapeDtypeStruct((depth, P_COLS, k), BF16),
        compiler_params=_cparams(("arbitrary", "arbitrary")),
        name="cast_w_in",
    )(w_t)
    w_dt = jnp.pad(w_t[:, COL_DT:COL_DT + DT_COLS, :], ((0, 0), (0, DT_PAD - DT_COLS), (0, 0))).astype(BF16)
    return w_main, w_dt


TN_ADA = 2048


def _ada_kernel(cond_ref, w0_ref, w1_ref, b_ref, o_ref):
    s = _silu(cond_ref[...]).astype(BF16)
    kh = D_MODEL // 2
    o_ref[...] = (_dot(s[:, :kh], w0_ref[...].astype(BF16))
                  + _dot(s[:, kh:], w1_ref[...].astype(BF16)) + b_ref[...])


def _ada_call(cond, ada_w, ada_b):
    n = ada_w.shape[-1]
    return pl.pallas_call(
        _ada_kernel,
        grid=(DEPTH, n // TN_ADA),
        in_specs=[
            pl.BlockSpec((8, D_MODEL), lambda l, j: (0, 0)),
            pl.BlockSpec((None, D_MODEL // 2, TN_ADA), lambda l, j: (l, 0, j)),
            pl.BlockSpec((None, D_MODEL // 2, TN_ADA), lambda l, j: (l, 1, j)),
            pl.BlockSpec((None, 1, TN_ADA), lambda l, j: (l, 0, j)),
        ],
        out_specs=pl.BlockSpec((None, 8, TN_ADA), lambda l, j: (l, 0, j)),
        out_shape=jax.ShapeDtypeStruct((DEPTH, 8, n), F32),
        compiler_params=_cparams(("arbitrary", "arbitrary")),
        name="ada",
    )(cond, ada_w, ada_w, ada_b.reshape(DEPTH, 1, n))


def _rope_store(acc, cos, sin, o_ref, n_heads, scale, rows=slice(None)):
    for h in range(n_heads):
        xs = acc[:, h * HEAD_DIM:(h + 1) * HEAD_DIM]
        rot = pltpu.roll(xs, HEAD_DIM // 2, axis=1)
        o_ref[rows, h * HEAD_DIM:(h + 1) * HEAD_DIM] = ((xs * cos + rot * sin) * scale).astype(BF16)


def _inproj_kernel(skip_ctx_cols, tile0, has_prev, x_ref, shift_ref, scale_ref, w_ref, wdt_ref,
                   cos_ref, sin_ref, *rest):
    p_ref, dt_ref, h_scr = rest[2:] if has_prev else rest
    i = pl.program_id(0) + tile0
    j = pl.program_id(1)
    chunks = [slice(r, r + RC_IN) for r in range(0, TM_IN, RC_IN)]
    heads_per_tile = TN_IN // HEAD_DIM
    if skip_ctx_cols:
        unused = (j == J_AQ) | (j == J_RG) | (j >= J_GATES)
        active = jnp.logical_not((i == T_ALL // TM_IN - 1) & unused)
    else:
        active = True

    def prologue(rows):
        h = _ln_rows(x_ref[rows, :]) * (1.0 + scale_ref[...]) + shift_ref[...]
        h_scr[rows, :] = h.astype(BF16)
        dt_ref[rows, :] = _dot_nt(h_scr[rows, :], wdt_ref[...])

    def rotated(scale):
        acc = _dot_nt(h_scr[...], w_ref[...])
        _rope_store(acc, cos_ref[...], sin_ref[...], p_ref, heads_per_tile, scale)

    def plain():
        p_ref[...] = _dot_nt(h_scr[...], w_ref[...]).astype(BF16)

    @pl.when((j == J_AQ) & active)
    def _():
        for rows in chunks:
            prologue(rows)
            acc = _dot_nt(h_scr[rows, :], w_ref[...])
            _rope_store(acc, cos_ref[rows, :], sin_ref[rows, :], p_ref, heads_per_tile,
                        Q_SCALE * LOG2E, rows)

    if skip_ctx_cols:
        @pl.when((j == J_AQ) & jnp.logical_not(active))
        def _():
            for rows in chunks:
                prologue(rows)

    @pl.when(j == J_AKAV)
    def _():
        acc = _dot_nt(h_scr[...], w_ref[...])
        _rope_store(acc, cos_ref[...], sin_ref[...], p_ref, ATTN_HKV, 1.0)
        p_ref[:, ATTN_HKV * HEAD_DIM:] = acc[:, ATTN_HKV * HEAD_DIM:].astype(BF16)

    pl.when(j == J_RQ)(lambda: rotated(Q_SCALE))
    pl.when(j == J_RK)(lambda: rotated(1.0))
    pl.when((j > J_AKAV) & (j != J_RQ) & (j != J_RK) & active)(plain)


def _inproj_call(l, x_src, tile0, mod, w_t, w_dt, cos2, sin2, skip_ctx_cols, prev=None):
    lat_tiles = T_LAT // TM_IN

    def mod_map(k):
        return lambda i, j: (_batch_of_tile(i + tile0, TM_IN) * 6 + k, 0, 0)

    def rope_map(i, j):
        g = i + tile0
        return (jnp.where(g < lat_tiles, g % (SEQ // TM_IN), SEQ // TM_IN), 0)

    in_specs = [
        pl.BlockSpec((TM_IN, D_MODEL), lambda i, j: (i, 0)),
        pl.BlockSpec((None, 1, D_MODEL), mod_map(0)),
        pl.BlockSpec((None, 1, D_MODEL), mod_map(1)),
        pl.BlockSpec((None, TN_IN, D_MODEL), lambda i, j: (l, j, 0)),
        pl.BlockSpec((None, DT_PAD, D_MODEL), lambda i, j: (l, 0, 0)),
        pl.BlockSpec((TM_IN, HEAD_DIM), rope_map),
        pl.BlockSpec((TM_IN, HEAD_DIM), rope_map),
    ]
    args = [x_src, mod, mod, w_t, w_dt, cos2, sin2]
    aliases = {}
    if prev is not None:
        aliases = {len(args): 0, len(args) + 1: 1}
        in_specs += [pl.BlockSpec(memory_space=pl.ANY)] * 2
        args += list(prev)
    return pl.pallas_call(
        functools.partial(_inproj_kernel, skip_ctx_cols, tile0, prev is not None),
        grid=(x_src.shape[0] // TM_IN, P_COLS // TN_IN),
        in_specs=in_specs,
        out_specs=[
            pl.BlockSpec((TM_IN, TN_IN), lambda i, j: (i + tile0, j)),
            pl.BlockSpec((TM_IN, DT_PAD), lambda i, j: (i + tile0, 0)),
        ],
        out_shape=[
            jax.ShapeDtypeStruct((T_ALL, P_COLS), BF16),
            jax.ShapeDtypeStruct((T_ALL, DT_PAD), F32),
        ],
        scratch_shapes=[pltpu.VMEM((TM_IN, D_MODEL), BF16)],
        input_output_aliases=aliases,
        compiler_params=_cparams(("arbitrary", "arbitrary")),
        name="inproj",
    )(*args)


TN_CONV = 512


def _conv_taps(prev, x, nxt, w_ref, b_ref, pos, slen):
    xe = jnp.concatenate([prev, x, nxt], axis=0)
    w = w_ref[...]
    acc = jnp.zeros((TM_CONV, TN_CONV), F32) + b_ref[...]
    half = SSD_CONV // 2
    for k in range(SSD_CONV):
        off = HALO - half + k
        xk = xe[off:off + TM_CONV, :]
        if pos is not None and k < half:
            xk = jnp.where(pos + (k - half) >= 0, xk, 0.0)
        elif pos is not None and k > half:
            xk = jnp.where(pos + (k - half) < slen, xk, 0.0)
        acc = acc + xk * w[k:k + 1, :]
    return _silu(acc).astype(BF16)


def _conv_kernel(x_ref, prev_ref, next_ref, w_ref, b_ref, o_ref):
    i = pl.program_id(0)
    tiles_per_seq = SEQ // TM_CONV
    x = x_ref[...].astype(F32)
    prev = prev_ref[...].astype(F32)
    nxt = next_ref[...].astype(F32)

    @pl.when(i < T_LAT // TM_CONV)
    def _():
        t = i & (tiles_per_seq - 1)
        o_ref[...] = _conv_taps(jnp.where(t == 0, 0.0, prev), x,
                                jnp.where(t == tiles_per_seq - 1, 0.0, nxt), w_ref, b_ref, None, None)

    @pl.when(i >= T_LAT // TM_CONV)
    def _():
        row = lax.broadcasted_iota(jnp.int32, (TM_CONV, TN_CONV), 0)
        o_ref[...] = _conv_taps(prev, x, nxt, w_ref, b_ref, row & (CTX_LEN - 1), CTX_LEN)


def _conv_call(p_all, conv_w, conv_b):
    cb0 = COL_XBC // TN_CONV
    hpt = TM_CONV // HALO
    last_halo = T_ALL // HALO - 1
    return pl.pallas_call(
        _conv_kernel,
        grid=(T_ALL // TM_CONV, SSD_CONV_CH // TN_CONV),
        in_specs=[
            pl.BlockSpec((TM_CONV, TN_CONV), lambda i, j: (i, cb0 + j)),
            pl.BlockSpec((HALO, TN_CONV), lambda i, j: (jnp.maximum(i * hpt - 1, 0), cb0 + j)),
            pl.BlockSpec((HALO, TN_CONV), lambda i, j: (jnp.minimum((i + 1) * hpt, last_halo), cb0 + j)),
            pl.BlockSpec((SSD_CONV, TN_CONV), lambda i, j: (0, j)),
            pl.BlockSpec((1, TN_CONV), lambda i, j: (0, j)),
        ],
        out_specs=pl.BlockSpec((TM_CONV, TN_CONV), lambda i, j: (i, j)),
        out_shape=jax.ShapeDtypeStruct((T_ALL, SSD_CONV_CH), BF16),
        compiler_params=_cparams(("arbitrary", "arbitrary")),
        name="conv",
    )(p_all, p_all, p_all, conv_w, conv_b.reshape(1, SSD_CONV_CH))


def _softmax_pv(s, vals, sink_col):
    m = jnp.maximum(jnp.max(s, axis=-1, keepdims=True), sink_col)
    p = jnp.exp2(s - m)
    l = jnp.sum(p, axis=-1, keepdims=True) + jnp.exp2(sink_col - m)
    return _dot(p.astype(BF16), vals) / l


def _group_queries(q_ref, hkv):
    h0 = hkv * ATTN_GROUP
    return jnp.concatenate(
        [q_ref[:, (h0 + g) * HEAD_DIM:(h0 + g + 1) * HEAD_DIM] for g in range(ATTN_GROUP)], axis=0)


def _sink_column(sink_ref, hkv, nq):
    return jnp.concatenate(
        [jnp.full((nq, 1), sink_ref[hkv * ATTN_GROUP + g] * LOG2E, F32) for g in range(ATTN_GROUP)], axis=0)


def _attn_kernel(sink_ref, q_ref, kp_ref, kc_ref, kn_ref, vp_ref, vc_ref, vn_ref, kx_ref, vx_ref,
                 o_ref):
    n = pl.program_id(1)
    rows = ATTN_GROUP * BLOCK
    qi = lax.broadcasted_iota(jnp.int32, (rows, BLOCK), 0) & (BLOCK - 1)
    kj = lax.broadcasted_iota(jnp.int32, (rows, BLOCK), 1)
    prev_ok = (kj >= qi) & (n > 0)
    next_ok = (kj <= qi) & (n < SEQ // BLOCK - 1)

    def mask_fn(s):
        return jnp.concatenate([
            jnp.where(prev_ok, s[:, :BLOCK], NEG_INF), s[:, BLOCK:2 * BLOCK],
            jnp.where(next_ok, s[:, 2 * BLOCK:3 * BLOCK], NEG_INF), s[:, 3 * BLOCK:]], axis=1)

    kv_cols = lambda hkv: slice(hkv * HEAD_DIM, (hkv + 1) * HEAD_DIM)
    scores = []
    for hkv in range(ATTN_HKV):
        c = kv_cols(hkv)
        keys = jnp.concatenate([kp_ref[:, c], kc_ref[:, c], kn_ref[:, c], kx_ref[:, c]], axis=0)
        scores.append(mask_fn(_dot_nt(_group_queries(q_ref, hkv), keys)))
    outs = []
    for hkv in range(ATTN_HKV):
        c = kv_cols(hkv)
        vals = jnp.concatenate([vp_ref[:, c], vc_ref[:, c], vn_ref[:, c], vx_ref[:, c]], axis=0)
        o = _softmax_pv(scores[hkv], vals, _sink_column(sink_ref, hkv, BLOCK))
        outs += [o[g * BLOCK:(g + 1) * BLOCK, :] for g in range(ATTN_GROUP)]
    o_ref[...] = jnp.concatenate(outs, axis=1).astype(BF16)


def _ctx_attn_kernel(sink_ref, q_ref, kx_ref, vx_ref, prev_ref, o_ref):
    del prev_ref
    kv_cols = lambda hkv: slice(hkv * HEAD_DIM, (hkv + 1) * HEAD_DIM)
    scores = [_dot_nt(_group_queries(q_ref, hkv), kx_ref[:, kv_cols(hkv)]) for hkv in range(ATTN_HKV)]
    outs = []
    for hkv in range(ATTN_HKV):
        o = _softmax_pv(scores[hkv], vx_ref[:, kv_cols(hkv)], _sink_column(sink_ref, hkv, CTX_LEN))
        outs += [o[g * CTX_LEN:(g + 1) * CTX_LEN, :] for g in range(ATTN_GROUP)]
    o_ref[...] = jnp.concatenate(outs, axis=1).astype(BF16)


def _attn_calls(p_all, sink, update_ctx):
    nb = SEQ // BLOCK
    kvw = ATTN_HKV * HEAD_DIM
    ck = COL_AK // kvw
    cv = COL_AV // kvw
    ctx0 = T_LAT // CTX_LEN
    smem = pl.BlockSpec(memory_space=pltpu.SMEM)

    def kv_spec(col, dn):
        return pl.BlockSpec((BLOCK, kvw), lambda b, n: (b * nb + jnp.clip(n + dn, 0, nb - 1), col))

    def ctx_spec(col):
        return pl.BlockSpec((CTX_LEN, kvw), lambda b, n: (ctx0 + b, col))

    attn = pl.pallas_call(
        _attn_kernel,
        grid=(BATCH, nb),
        in_specs=[
            smem,
            pl.BlockSpec((BLOCK, ATTN_WIDTH), lambda b, n: (b * nb + n, 0)),
            kv_spec(ck, -1), kv_spec(ck, 0), kv_spec(ck, 1),
            kv_spec(cv, -1), kv_spec(cv, 0), kv_spec(cv, 1),
            ctx_spec(ck), ctx_spec(cv),
        ],
        out_specs=pl.BlockSpec((BLOCK, ATTN_WIDTH), lambda b, n: (b * nb + n, 0)),
        out_shape=jax.ShapeDtypeStruct((T_ALL, ATTN_WIDTH), BF16),
        compiler_params=_cparams(("arbitrary", "arbitrary")),
        name="attn",
    )(sink, p_all, p_all, p_all, p_all, p_all, p_all, p_all, p_all, p_all)
    if not update_ctx:
        return attn
    return pl.pallas_call(
        _ctx_attn_kernel,
        grid=(BATCH,),
        in_specs=[
            smem,
            pl.BlockSpec((CTX_LEN, ATTN_WIDTH), lambda b: (ctx0 + b, 0)),
            pl.BlockSpec((CTX_LEN, kvw), lambda b: (ctx0 + b, ck)),
            pl.BlockSpec((CTX_LEN, kvw), lambda b: (ctx0 + b, cv)),
            pl.BlockSpec(memory_space=pl.ANY),
        ],
        out_specs=pl.BlockSpec((CTX_LEN, ATTN_WIDTH), lambda b: (ctx0 + b, 0)),
        out_shape=jax.ShapeDtypeStruct((T_ALL, ATTN_WIDTH), BF16),
        input_output_aliases={4: 0},
        compiler_params=_cparams(("arbitrary",)),
        name="ctx_attn",
    )(sink, p_all, p_all, p_all, attn)


def _fwd_chunk(b, s):
    ctx = T_LAT // CHUNK + b * NCHUNK_CTX + s
    lat = b * NCHUNK_LAT + (s - NCHUNK_CTX)
    return jnp.where(s < NCHUNK_CTX, ctx, lat)


def _bwd_chunk(b, s):
    ctx = T_LAT // CHUNK + b * NCHUNK_CTX + (NCHUNK_CTX - 1 - s)
    lat = b * NCHUNK_LAT + (NSTEP - 1 - s)
    return jnp.where(s < NCHUNK_CTX, ctx, lat)


def _ssd_kernel(xs_f, bm_f, cm_f, dt_f, xs_b, bm_b, cm_b, dt_b,
                bias_row, alog_row, bias_col, alog_col, yf_ref, yb_ref, h_scr):
    s = pl.program_id(1)

    @pl.when(s == 0)
    def _():
        h_scr[...] = jnp.zeros_like(h_scr)

    ii = lax.broadcasted_iota(jnp.int32, (CHUNK, CHUNK), 0)
    jj = lax.broadcasted_iota(jnp.int32, (CHUNK, CHUNK), 1)
    lane_lo = jj < SSD_P
    lower = jj <= ii
    upper = jj >= ii
    hg = SSD_HEADS // SSD_GROUPS
    dirs = ((xs_f, bm_f, cm_f, dt_f, yf_ref), (xs_b, bm_b, cm_b, dt_b, yb_ref))
    causal = (lower, upper)
    grp = lambda g: slice(g * SSD_STATE, (g + 1) * SSD_STATE)
    pair_lanes = lambda pair: slice(pair * 2 * SSD_P, (pair + 1) * 2 * SSD_P)

    acs, acs_t, row_t, dte_t, tot = {}, {}, {}, {}, {}
    for d in range(2):
        dt_ref = dirs[d][3]
        tri = jnp.where(causal[d], 1.0, 0.0).astype(BF16)
        tri_t = jnp.where(causal[1 - d], 1.0, 0.0).astype(BF16)
        last = CHUNK - 1 if d == 0 else 0
        r0 = d * SSD_HEADS
        dt_raw = dt_ref[...]
        dt_c = _softplus(dt_raw + bias_row[...])
        adt_c = dt_c * (-LOG2E * jnp.exp(alog_row[...]))
        acs[d] = sum(_dot(tri, p) for p in _split3(adt_c))
        dt_t = _softplus(dt_raw.T[r0:r0 + SSD_HEADS, :] + bias_col[r0:r0 + SSD_HEADS, :])
        adt_t = dt_t * (-LOG2E * jnp.exp(alog_col[r0:r0 + SSD_HEADS, :]))
        acs_t[d] = sum(_dot(p, tri_t) for p in _split3(adt_t))
        row_t[d] = acs_t[d] - jnp.log2(dt_t)
        dte_t[d] = jnp.exp2(acs_t[d][:, last:last + 1] - acs_t[d]) * dt_t
        tot[d] = acs[d][last:last + 1, :]

    cb, bm_t, cm = {}, {}, {}
    for d in range(2):
        _, bm_ref, cm_ref, _, _ = dirs[d]
        for g in range(SSD_GROUPS):
            cm[d, g] = cm_ref[:, grp(g)]
            cb[d, g] = _dot_nt(cm[d, g], bm_ref[:, grp(g)]).astype(BF16)
            bm_t[d, g] = bm_ref[:, grp(g)].astype(F32).T

    for d in range(2):
        xs_ref, y_ref = dirs[d][0], dirs[d][4]
        r0 = d * SSD_HEADS
        y_pairs, h_pairs = [], []
        for pair in range(SSD_HEADS // 2):
            g = pair // (hg // 2)
            x_pair = xs_ref[:, pair_lanes(pair)]
            h_pair = h_scr[d, :, pair_lanes(pair)]
            rhs = jnp.concatenate([x_pair, h_pair.astype(BF16)], axis=0)
            ys, ups, cds = [], [], []
            for h in (2 * pair, 2 * pair + 1):
                col = jnp.broadcast_to(acs[d][:, r0 + h:r0 + h + 1], (CHUNK, CHUNK))
                dec = jnp.exp2(jnp.where(causal[d], col - row_t[d][h:h + 1, :], NEG_INF))
                m_intra = cb[d, g] * dec.astype(BF16)
                m_state = cm[d, g] * jnp.exp2(col).astype(BF16)
                ys.append(_dot(jnp.concatenate([m_intra, m_state], axis=1), rhs))
                ups.append(_dot((bm_t[d, g] * dte_t[d][h:h + 1, :]).astype(BF16), x_pair))
                cds.append(jnp.exp2(tot[d][:, r0 + h:r0 + h + 1]))
            y_pairs.append(jnp.where(lane_lo, ys[0], ys[1]))
            h_pairs.append(jnp.where(lane_lo, cds[0], cds[1]) * h_pair
                           + jnp.where(lane_lo, ups[0], ups[1]))
        y_ref[...] = jnp.concatenate(y_pairs, axis=1).astype(BF16)
        h_scr[d] = jnp.concatenate(h_pairs, axis=1)


def _ssd_call(xbc, dt_all, bias_row, alog_row, bias_col, alog_col):
    bcol = SSD_INNER // (SSD_GROUPS * SSD_STATE)

    def specs(chunk_fn):
        return [
            pl.BlockSpec((CHUNK, SSD_INNER), lambda b, s: (chunk_fn(b, s), 0)),
            pl.BlockSpec((CHUNK, SSD_GROUPS * SSD_STATE), lambda b, s: (chunk_fn(b, s), bcol)),
            pl.BlockSpec((CHUNK, SSD_GROUPS * SSD_STATE), lambda b, s: (chunk_fn(b, s), bcol + 1)),
            pl.BlockSpec((CHUNK, DT_PAD), lambda b, s: (chunk_fn(b, s), 0)),
        ]

    const = lambda shape: pl.BlockSpec(shape, lambda b, s: (0, 0))
    return pl.pallas_call(
        _ssd_kernel,
        grid=(BATCH, NSTEP),
        in_specs=specs(_fwd_chunk) + specs(_bwd_chunk) + [
            const((1, DT_PAD)), const((1, DT_PAD)), const((DT_PAD, 1)), const((DT_PAD, 1))],
        out_specs=[
            pl.BlockSpec((CHUNK, SSD_INNER), lambda b, s: (_fwd_chunk(b, s), 0)),
            pl.BlockSpec((CHUNK, SSD_INNER), lambda b, s: (_bwd_chunk(b, s), 0)),
        ],
        out_shape=[jax.ShapeDtypeStruct((T_ALL, SSD_INNER), BF16)] * 2,
        scratch_shapes=[pltpu.VMEM((2, SSD_STATE, SSD_INNER), F32)],
        compiler_params=_cparams(("arbitrary", "arbitrary")),
        name="ssd",
    )(xbc, xbc, xbc, dt_all, xbc, xbc, xbc, dt_all, bias_row, alog_row, bias_col, alog_col)


def _ret_kernel(lg_ref, q_f, k_f, v_f, q_b, k_b, v_b, of_ref, ob_ref, s_scr, tab_scr):
    s = pl.program_id(1)

    @pl.when(s == 0)
    def _():
        s_scr[...] = jnp.zeros_like(s_scr)
        ii = lax.broadcasted_iota(jnp.int32, (CHUNK, CHUNK), 0)
        jj = lax.broadcasted_iota(jnp.int32, (CHUNK, CHUNK), 1)
        for d in range(2):
            if d == 0:
                dist = (ii - jj).astype(F32)
                row_pow = (ii + 1).astype(F32)
                key_pow = (CHUNK - 1 - ii).astype(F32)
            else:
                dist = (jj - ii).astype(F32)
                row_pow = (CHUNK - ii).astype(F32)
                key_pow = ii.astype(F32)
            for h in range(RET_HEADS):
                lg = lg_ref[d, h]
                tab_scr[d, h, 0] = jnp.where(dist >= 0.0, jnp.exp(jnp.maximum(dist, 0.0) * lg), 0.0)
                tab_scr[d, h, 1] = jnp.exp(row_pow * lg)
                tab_scr[d, h, 2] = jnp.exp(key_pow * lg)

    dirs = ((q_f, k_f, v_f, of_ref), (q_b, k_b, v_b, ob_ref))
    heads = [(d, h) for d in range(2) for h in range(RET_HEADS)]
    lanes = lambda h: slice(h * HEAD_DIM, (h + 1) * HEAD_DIM)
    raw = {}
    for d, h in heads:
        q_ref, k_ref, v_ref, _ = dirs[d]
        raw[d, h] = _dot_nt(q_ref[:, lanes(h)], k_ref[:, lanes(h)])
    kv = {}
    for d, h in heads:
        _, k_ref, v_ref, _ = dirs[d]
        k_dec = (k_ref[:, lanes(h)].astype(F32) * tab_scr[d, h, 2]).astype(BF16)
        kv[d, h] = _dot_tn(k_dec, v_ref[:, lanes(h)])
    outs = {}
    for d, h in heads:
        q_ref, _, v_ref, _ = dirs[d]
        scores = raw[d, h] * tab_scr[d, h, 0]
        q_cross = q_ref[:, lanes(h)].astype(F32) * tab_scr[d, h, 1]
        state = s_scr[d, h * HEAD_DIM:(h + 1) * HEAD_DIM, :]
        lhs = jnp.concatenate([scores.astype(BF16), q_cross.astype(BF16)], axis=1)
        rhs = jnp.concatenate([v_ref[:, lanes(h)], state.astype(BF16)], axis=0)
        outs[d, h] = _dot(lhs, rhs)
    for d in range(2):
        states = []
        for h in range(RET_HEADS):
            chunk_decay = jnp.exp(jnp.full((1, HEAD_DIM), float(CHUNK), F32) * lg_ref[d, h])
            states.append(chunk_decay * s_scr[d, h * HEAD_DIM:(h + 1) * HEAD_DIM, :] + kv[d, h])
        dirs[d][3][...] = jnp.concatenate([outs[d, h] for h in range(RET_HEADS)], axis=1).astype(BF16)
        s_scr[d] = jnp.concatenate(states, axis=0)


def _ret_call(p_all, log_decay):
    def specs(chunk_fn):
        return [
            pl.BlockSpec((CHUNK, RET_WIDTH), lambda b, s: (chunk_fn(b, s), COL_RQ // RET_WIDTH)),
            pl.BlockSpec((CHUNK, RET_WIDTH), lambda b, s: (chunk_fn(b, s), COL_RK // RET_WIDTH)),
            pl.BlockSpec((CHUNK, RET_WIDTH), lambda b, s: (chunk_fn(b, s), COL_RV // RET_WIDTH)),
        ]

    return pl.pallas_call(
        _ret_kernel,
        grid=(BATCH, NSTEP),
        in_specs=[pl.BlockSpec(memory_space=pltpu.SMEM)] + specs(_fwd_chunk) + specs(_bwd_chunk),
        out_specs=[
            pl.BlockSpec((CHUNK, RET_WIDTH), lambda b, s: (_fwd_chunk(b, s), 0)),
            pl.BlockSpec((CHUNK, RET_WIDTH), lambda b, s: (_bwd_chunk(b, s), 0)),
        ],
        out_shape=[jax.ShapeDtypeStruct((T_ALL, RET_WIDTH), BF16)] * 2,
        scratch_shapes=[pltpu.VMEM((2, RET_HEADS * HEAD_DIM, HEAD_DIM), F32),
                        pltpu.VMEM((2, RET_HEADS, 3, CHUNK, CHUNK), F32)],
        compiler_params=_cparams(("arbitrary", "arbitrary")),
        name="retention",
    )(log_decay, p_all, p_all, p_all, p_all, p_all, p_all)


def _merge_kernel(attn_ref, yf_ref, yb_ref, xs_ref, z0_ref, z1_ref, of_ref, ob_ref, rg_ref,
                  ga_ref, gs_ref, gr_ref, dskip_ref, ssd_g_ref, ret_g_ref,
                  wa_ref, ws_ref, wr_ref, o_ref):
    z = jnp.concatenate([z0_ref[...], z1_ref[...]], axis=1).astype(F32)
    y = (yf_ref[...].astype(F32) + yb_ref[...].astype(F32)
         + dskip_ref[...] * xs_ref[...].astype(F32))
    y = y * _silu(z)
    ssd_o = y * lax.rsqrt(jnp.mean(y * y, axis=-1, keepdims=True) + LN_EPS) * ssd_g_ref[...]

    o = of_ref[...].astype(F32) + ob_ref[...].astype(F32)
    normed = jnp.concatenate(
        [_ln_rows(o[:, h * HEAD_DIM:(h + 1) * HEAD_DIM]) for h in range(RET_HEADS)], axis=1)
    ret_o = normed * ret_g_ref[...] * _silu(rg_ref[...].astype(F32))

    merged = (_sigmoid(ga_ref[...].astype(F32)) * _dot(attn_ref[...], wa_ref[...])
              + _sigmoid(gs_ref[...].astype(F32)) * _dot(ssd_o.astype(BF16), ws_ref[...])
              + _sigmoid(gr_ref[...].astype(F32)) * _dot(ret_o.astype(BF16), wr_ref[...]))
    o_ref[...] = merged.astype(BF16)


def _merge_call(l, n_rows, attn, yf, yb, xbc, p_all, of, ob, dskip, ssd_g, ret_g, wa, ws, wr):
    tm = TM_MERGE
    zw = SSD_INNER // 2
    w1024 = lambda c: pl.BlockSpec((tm, 1024), lambda i: (i, c))
    zspec = lambda c: pl.BlockSpec((tm, zw), lambda i: (i, COL_Z // zw + c))
    gate = lambda c: pl.BlockSpec((tm, D_MODEL), lambda i: (i, COL_GATES // D_MODEL + c))
    vec = pl.BlockSpec((1, 1024), lambda i: (0, 0))
    wspec = pl.BlockSpec((None, 1024, D_MODEL), lambda i: (l, 0, 0), pipeline_mode=pl.Buffered(1))
    return pl.pallas_call(
        _merge_kernel,
        grid=(n_rows // tm,),
        in_specs=[
            w1024(0), w1024(0), w1024(0), w1024(0), zspec(0), zspec(1),
            w1024(0), w1024(0), w1024(COL_RG // 1024),
            gate(0), gate(1), gate(2), vec, vec, vec, wspec, wspec, wspec,
        ],
        out_specs=pl.BlockSpec((tm, D_MODEL), lambda i: (i, 0)),
        out_shape=jax.ShapeDtypeStruct((n_rows, D_MODEL), BF16),
        compiler_params=_cparams(("arbitrary",)),
        name="merge",
    )(attn, yf, yb, xbc, p_all, p_all, of, ob, p_all, p_all, p_all, p_all, dskip, ssd_g, ret_g,
      wa, ws, wr)


def _deepnorm(x, y, gate, g, b):
    return _ln_rows(DEEPNORM_ALPHA * x + gate * y) * g + b


def _outproj_kernel(m_ref, x_ref, w_ref, gate_ref, g_ref, b_ref, *rest):
    o_ref = rest[-1]
    for r in range(0, TM_OUT, RC_OUT):
        rows = slice(r, r + RC_OUT)
        mix = _dot(m_ref[rows, :], w_ref[...])
        o_ref[rows, :] = _deepnorm(x_ref[rows, :], mix, gate_ref[...], g_ref[...], b_ref[...])


def _outproj_call(l, n_rows, n_tiles, merged, x_src, tile0, w_out, mod, ln_g, ln_b, prev=None):
    tm = TM_OUT
    vec = pl.BlockSpec((1, D_MODEL), lambda i: (0, 0))
    in_specs = [
        pl.BlockSpec((tm, D_MODEL), lambda i: (i + tile0, 0)),
        pl.BlockSpec((tm, D_MODEL), lambda i: (i, 0)),
        pl.BlockSpec((None, D_MODEL, D_MODEL), lambda i: (l, 0, 0), pipeline_mode=pl.Buffered(1)),
        pl.BlockSpec((None, 1, D_MODEL), lambda i: (_batch_of_tile(i + tile0, tm) * 6 + 2, 0, 0)),
        vec, vec,
    ]
    args = [merged, x_src, w_out, mod, ln_g, ln_b]
    aliases = {}
    if prev is not None:
        aliases = {len(args): 0}
        in_specs.append(pl.BlockSpec(memory_space=pl.ANY))
        args.append(prev)
    return pl.pallas_call(
        _outproj_kernel,
        grid=(n_tiles,),
        in_specs=in_specs,
        out_specs=pl.BlockSpec((tm, D_MODEL), lambda i: (i + tile0, 0)),
        out_shape=jax.ShapeDtypeStruct((n_rows, D_MODEL), F32),
        input_output_aliases=aliases,
        compiler_params=_cparams(("arbitrary",)),
        name="outproj",
    )(*args)


def _mlp_kernel(x_ref, shift_ref, scale_ref, gate_ref, wup_ref, wdn_ref, g_ref, b_ref,
                o_ref, h_scr, acc_scr):
    j = pl.program_id(1)
    last = pl.num_programs(1) - 1
    chunks = [slice(r, r + RC_MLP) for r in range(0, TM_MLP, RC_MLP)]

    def up_down(rows):
        u = jnp.maximum(_dot(h_scr[rows, :], wup_ref[...]), 0.0)
        return _dot((u * u).astype(BF16), wdn_ref[...])

    @pl.when(j == 0)
    def _():
        for rows in chunks:
            h = _ln_rows(x_ref[rows, :]) * (1.0 + scale_ref[...]) + shift_ref[...]
            h_scr[rows, :] = h.astype(BF16)
        for rows in chunks:
            acc_scr[rows, :] = up_down(rows)

    @pl.when((j > 0) & (j < last))
    def _():
        acc_scr[...] += up_down(slice(None))

    @pl.when(j == last)
    def _():
        for rows in chunks:
            y = acc_scr[rows, :] + up_down(rows)
            o_ref[rows, :] = _deepnorm(x_ref[rows, :], y, gate_ref[...], g_ref[...], b_ref[...])


def _mlp_call(l, n_rows, x1, mod, w_up, w_down, ln_g, ln_b):
    tm, tf = TM_MLP, TF_MLP

    def mod_map(k):
        return lambda i, j: (_batch_of_tile(i, tm) * 6 + k, 0, 0)

    vec = pl.BlockSpec((1, D_MODEL), lambda i, j: (0, 0))
    return pl.pallas_call(
        _mlp_kernel,
        grid=(n_rows // tm, D_FF // tf),
        in_specs=[
            pl.BlockSpec((tm, D_MODEL), lambda i, j: (i, 0)),
            pl.BlockSpec((None, 1, D_MODEL), mod_map(3)),
            pl.BlockSpec((None, 1, D_MODEL), mod_map(4)),
            pl.BlockSpec((None, 1, D_MODEL), mod_map(5)),
            pl.BlockSpec((None, D_MODEL, tf), lambda i, j: (l, 0, j)),
            pl.BlockSpec((None, tf, D_MODEL), lambda i, j: (l, j, 0)),
            vec, vec,
        ],
        out_specs=pl.BlockSpec((tm, D_MODEL), lambda i, j: (i, 0)),
        out_shape=jax.ShapeDtypeStruct((n_rows, D_MODEL), F32),
        scratch_shapes=[pltpu.VMEM((tm, D_MODEL), BF16), pltpu.VMEM((tm, D_MODEL), F32)],
        compiler_params=_cparams(("arbitrary", "arbitrary")),
        name="mlp",
    )(x1, mod, mod, mod, w_up, w_down, ln_g, ln_b)


def _rope_tables():
    f32 = np.float32
    rows = SEQ // GRID_W
    row = np.repeat(np.arange(rows), GRID_W).astype(f32)
    col = (np.arange(rows * GRID_W) % GRID_W).astype(f32)
    n_freq = HEAD_DIM // 4
    inv = (f32(ROPE_BASE) ** (-np.arange(n_freq, dtype=f32) / f32(n_freq))).astype(f32)
    ang = np.concatenate([row[:, None] * inv, col[:, None] * inv], axis=-1).astype(f32)
    cos, sin = np.cos(ang).astype(f32), np.sin(ang).astype(f32)
    cos2 = np.concatenate([cos, cos], axis=-1)
    sin2 = np.concatenate([-sin, sin], axis=-1)
    cos2 = np.concatenate([cos2, np.ones((TM_IN, HEAD_DIM), f32)], axis=0)
    sin2 = np.concatenate([sin2, np.zeros((TM_IN, HEAD_DIM), f32)], axis=0)
    return jnp.asarray(cos2), jnp.asarray(sin2)


def _pad_heads(v):
    flat = jnp.pad(v.reshape(-1).astype(F32), (0, DT_PAD - DT_COLS))
    return flat.reshape(1, DT_PAD), flat.reshape(DT_PAD, 1)


def kernel(x, c, ctx, c_ctx, ada_w, ada_b, w_in, attn_sink, ssd_conv_w, ssd_conv_b, ssd_a_log,
           ssd_dt_bias, ssd_d, ssd_norm_g, ret_log_decay, ret_norm_g, w_branch_attn, w_branch_ssd,
           w_branch_ret, w_out, ln1_g, ln1_b, w_mlp_up, w_mlp_down, ln2_g, ln2_b):
    lat_tiles_in, lat_tiles_out = T_LAT // TM_IN, T_LAT // TM_OUT
    sources = [(x.reshape(T_LAT, D_MODEL), 0, 0), (ctx.reshape(T_CTX, D_MODEL), lat_tiles_in, lat_tiles_out)]
    cond = jnp.concatenate([c, c_ctx[None, :], jnp.zeros((8 - BATCH - 1, D_MODEL), F32)], axis=0)
    mod_all = _ada_call(cond, ada_w, ada_b).reshape(DEPTH, 8 * 6, 1, D_MODEL)
    cos2, sin2 = _rope_tables()
    wb_attn, wb_ssd, wb_ret = (_cast_stacked(w) for w in (w_branch_attn, w_branch_ssd, w_branch_ret))
    w_out_b, w_up_b, w_down_b = _cast_stacked(w_out), _cast_stacked(w_mlp_up), _cast_stacked(w_mlp_down)
    w_in_t, w_dt = _cast_w_in(w_in)

    for l in range(DEPTH):
        update_ctx = l < DEPTH - 1
        n_rows = T_ALL if update_ctx else T_LAT
        mod = mod_all[l]
        proj = None
        for x_src, tile_in, _ in sources:
            proj = _inproj_call(l, x_src, tile_in, mod, w_in_t, w_dt, cos2, sin2, not update_ctx, proj)
        p_all, dt_all = proj
        xbc = _conv_call(p_all, ssd_conv_w[l], ssd_conv_b[l])

        attn = _attn_calls(p_all, attn_sink[l].astype(F32), update_ctx)

        bias_row, bias_col = _pad_heads(ssd_dt_bias[l])
        alog_row, alog_col = _pad_heads(ssd_a_log[l])
        yf, yb = _ssd_call(xbc, dt_all, bias_row, alog_row, bias_col, alog_col)
        of, ob = _ret_call(p_all, ret_log_decay[l].astype(F32))

        dskip = jnp.repeat(ssd_d[l].astype(F32), SSD_P).reshape(1, SSD_INNER)
        merged = _merge_call(
            l, n_rows, attn, yf, yb, xbc, p_all, of, ob, dskip,
            ssd_norm_g[l].reshape(1, SSD_INNER), ret_norm_g[l].reshape(1, RET_WIDTH), wb_attn, wb_ssd, wb_ret)
        x1 = None
        for x_src, _, tile_out in sources:
            n_tiles = min(x_src.shape[0], n_rows - tile_out * TM_OUT) // TM_OUT
            x1 = _outproj_call(l, n_rows, n_tiles, merged, x_src, tile_out, w_out_b, mod,
                               ln1_g[l].reshape(1, D_MODEL), ln1_b[l].reshape(1, D_MODEL), x1)
        x_all = _mlp_call(l, n_rows, x1, mod, w_up_b, w_down_b,
                          ln2_g[l].reshape(1, D_MODEL), ln2_b[l].reshape(1, D_MODEL))
        sources = [(x_all, 0, 0)]
    return x_all.reshape(BATCH, SEQ, D_MODEL)
```

```python
import functools

import jax
import jax.numpy as jnp
from jax import lax
from jax.experimental import pallas as pl
from jax.experimental.pallas import tpu as pltpu
import numpy as np

F32 = jnp.float32
BF16 = jnp.bfloat16

D_MODEL = 2048
BATCH = 4
SEQ = 4096
DEPTH = 2
GRID_W = 64
CTX_LEN = 256
BLOCK = 128
HEAD_DIM = 128
ATTN_HQ = 8
ATTN_HKV = 2
ATTN_GROUP = ATTN_HQ // ATTN_HKV
ATTN_WIDTH = ATTN_HQ * HEAD_DIM
SSD_HEADS = 16
SSD_P = 64
SSD_INNER = SSD_HEADS * SSD_P
SSD_GROUPS = 2
SSD_STATE = 128
SSD_CONV = 5
SSD_CONV_CH = SSD_INNER + 2 * SSD_GROUPS * SSD_STATE
CHUNK = 128
RET_HEADS = 8
RET_WIDTH = RET_HEADS * HEAD_DIM
D_FF = 4 * D_MODEL
ROPE_BASE = 10000.0
DEEPNORM_ALPHA = (2 * DEPTH) ** 0.25
LN_EPS = 1e-6
NEG_INF = -1e30
Q_SCALE = HEAD_DIM ** -0.5
LOG2E = 1.4426950408889634

T_LAT = BATCH * SEQ
T_CTX = BATCH * CTX_LEN
T_ALL = T_LAT + T_CTX
NCHUNK_LAT = SEQ // CHUNK
NCHUNK_CTX = CTX_LEN // CHUNK
NSTEP = NCHUNK_LAT + NCHUNK_CTX

COL_AQ = 0
COL_AK = 1024
COL_AV = 1280
COL_Z = 1536
COL_XBC = 2560
COL_DT = 4096
COL_RQ = 4096
COL_RK = 5120
COL_RV = 6144
COL_RG = 7168
COL_GATES = 8192
P_COLS = 14336
DT_COLS = 2 * SSD_HEADS
DT_PAD = 128

TM_IN = 1024
RC_IN = 256
TN_IN = 1024
J_AQ, J_AKAV, J_RQ, J_RK = COL_AQ // TN_IN, COL_AK // TN_IN, COL_RQ // TN_IN, COL_RK // TN_IN
J_SPLIT = COL_DT // TN_IN
J_RG, J_GATES = COL_RG // TN_IN, COL_GATES // TN_IN
TM_MERGE = 256
TM_OUT = 1024
RC_OUT = 256
TM_MLP = 512
TF_MLP = 1024
RC_MLP = 256
TM_CONV = 1024
HALO = 16

VMEM_LIMIT = 56 * 1024 * 1024


def _cparams(sem):
    return pltpu.CompilerParams(dimension_semantics=sem, vmem_limit_bytes=VMEM_LIMIT)


def _sigmoid(x):
    return 0.5 * jnp.tanh(0.5 * x) + 0.5


def _silu(x):
    return x * _sigmoid(x)


def _softplus(x):
    return jnp.maximum(x, 0.0) + jnp.log1p(jnp.exp(-jnp.abs(x)))


def _ln_rows(x):
    mu = jnp.mean(x, axis=-1, keepdims=True)
    xc = x - mu
    var = jnp.mean(xc * xc, axis=-1, keepdims=True)
    return xc * lax.rsqrt(var + LN_EPS)


def _dot(a, b):
    return jnp.dot(a, b, preferred_element_type=F32)


def _dot_nt(a, b):
    return lax.dot_general(a, b, (((1,), (1,)), ((), ())), preferred_element_type=F32)


def _dot_tn(a, b):
    return lax.dot_general(a, b, (((0,), (0,)), ((), ())), preferred_element_type=F32)


def _split3(x):
    x1 = x.astype(BF16)
    r1 = x - x1.astype(F32)
    x2 = r1.astype(BF16)
    r2 = r1 - x2.astype(F32)
    return x1, x2, r2.astype(BF16)


def _batch_of_tile(i, tm):
    return jnp.where(i < T_LAT // tm, i // (SEQ // tm), BATCH)


CAST_BLOCK_BYTES = 8 * 1024 * 1024


def _cast_kernel(w_ref, o_ref):
    o_ref[...] = w_ref[...].astype(BF16)


def _cast_stacked(w):
    depth, rows, cols = w.shape
    rb = min(rows, CAST_BLOCK_BYTES // (cols * 4))
    spec = pl.BlockSpec((None, rb, cols), lambda l, i: (l, i, 0))
    return pl.pallas_call(
        _cast_kernel,
        grid=(depth, rows // rb),
        in_specs=[spec],
        out_specs=spec,
        out_shape=jax.ShapeDtypeStruct(w.shape, BF16),
        compiler_params=_cparams(("arbitrary", "arbitrary")),
        name="cast",
    )(w)


def _cast_col_tiled(w, tile_cols):
    depth, rows, cols = w.shape
    return pl.pallas_call(
        _cast_kernel,
        grid=(depth, cols // tile_cols),
        in_specs=[pl.BlockSpec((None, rows, tile_cols), lambda l, j: (l, 0, j))],
        out_specs=pl.BlockSpec((None, None, rows, tile_cols), lambda l, j: (l, j, 0, 0)),
        out_shape=jax.ShapeDtypeStruct((depth, cols // tile_cols, rows, tile_cols), BF16),
        compiler_params=_cparams(("arbitrary", "arbitrary")),
        name="cast_tiled",
    )(w)


def _cast_w_in(w_in):
    w_t = jnp.swapaxes(w_in, 1, 2)
    depth, cols, k = w_t.shape

    def src_map(l, j):
        return (l, (j * (TN_IN // DT_COLS) + jnp.where(j >= J_SPLIT, 1, 0)) * DT_COLS, 0)

    w_main = pl.pallas_call(
        _cast_kernel,
        grid=(depth, P_COLS // TN_IN),
        in_specs=[pl.BlockSpec((pl.Squeezed(), pl.Element(TN_IN), pl.Element(k)), src_map)],
        out_specs=pl.BlockSpec((None, TN_IN, k), lambda l, j: (l, j, 0)),
        out_shape=jax.ShapeDtypeStruct((depth, P_COLS, k), BF16),
        compiler_params=_cparams(("arbitrary", "arbitrary")),
        name="cast_w_in",
    )(w_t)
    w_dt = jnp.pad(w_t[:, COL_DT:COL_DT + DT_COLS, :], ((0, 0), (0, DT_PAD - DT_COLS), (0, 0))).astype(BF16)
    return w_main, w_dt


TN_ADA = 2048


def _ada_kernel(cond_ref, w0_ref, w1_ref, b_ref, o_ref):
    s = _silu(cond_ref[...]).astype(BF16)
    kh = D_MODEL // 2
    o_ref[...] = (_dot(s[:, :kh], w0_ref[...].astype(BF16))
                  + _dot(s[:, kh:], w1_ref[...].astype(BF16)) + b_ref[...])


def _ada_call(cond, ada_w, ada_b):
    n = ada_w.shape[-1]
    return pl.pallas_call(
        _ada_kernel,
        grid=(DEPTH, n // TN_ADA),
        in_specs=[
            pl.BlockSpec((8, D_MODEL), lambda l, j: (0, 0)),
            pl.BlockSpec((None, D_MODEL // 2, TN_ADA), lambda l, j: (l, 0, j)),
            pl.BlockSpec((None, D_MODEL // 2, TN_ADA), lambda l, j: (l, 1, j)),
            pl.BlockSpec((None, 1, TN_ADA), lambda l, j: (l, 0, j)),
        ],
        out_specs=pl.BlockSpec((None, 8, TN_ADA), lambda l, j: (l, 0, j)),
        out_shape=jax.ShapeDtypeStruct((DEPTH, 8, n), F32),
        compiler_params=_cparams(("arbitrary", "arbitrary")),
        name="ada",
    )(cond, ada_w, ada_w, ada_b.reshape(DEPTH, 1, n))


def _rope_store(acc, cos, sin, o_ref, n_heads, scale, rows=slice(None)):
    for h in range(n_heads):
        xs = acc[:, h * HEAD_DIM:(h + 1) * HEAD_DIM]
        rot = pltpu.roll(xs, HEAD_DIM // 2, axis=1)
        o_ref[rows, h * HEAD_DIM:(h + 1) * HEAD_DIM] = ((xs * cos + rot * sin) * scale).astype(BF16)


def _inproj_kernel(skip_ctx_cols, tile0, has_prev, x_ref, shift_ref, scale_ref, w_ref, wdt_ref,
                   cos_ref, sin_ref, *rest):
    p_ref, dt_ref, h_scr = rest[2:] if has_prev else rest
    i = pl.program_id(0) + tile0
    j = pl.program_id(1)
    chunks = [slice(r, r + RC_IN) for r in range(0, TM_IN, RC_IN)]
    heads_per_tile = TN_IN // HEAD_DIM
    if skip_ctx_cols:
        unused = (j == J_AQ) | (j == J_RG) | (j >= J_GATES)
        active = jnp.logical_not((i == T_ALL // TM_IN - 1) & unused)
    else:
        active = True

    def prologue(rows):
        h = _ln_rows(x_ref[rows, :]) * (1.0 + scale_ref[...]) + shift_ref[...]
        h_scr[rows, :] = h.astype(BF16)
        dt_ref[rows, :] = _dot_nt(h_scr[rows, :], wdt_ref[...])

    def rotated(scale):
        acc = _dot_nt(h_scr[...], w_ref[...])
        _rope_store(acc, cos_ref[...], sin_ref[...], p_ref, heads_per_tile, scale)

    def plain():
        p_ref[...] = _dot_nt(h_scr[...], w_ref[...]).astype(BF16)

    @pl.when((j == J_AQ) & active)
    def _():
        for rows in chunks:
            prologue(rows)
            acc = _dot_nt(h_scr[rows, :], w_ref[...])
            _rope_store(acc, cos_ref[rows, :], sin_ref[rows, :], p_ref, heads_per_tile,
                        Q_SCALE * LOG2E, rows)

    if skip_ctx_cols:
        @pl.when((j == J_AQ) & jnp.logical_not(active))
        def _():
            for rows in chunks:
                prologue(rows)

    @pl.when(j == J_AKAV)
    def _():
        acc = _dot_nt(h_scr[...], w_ref[...])
        _rope_store(acc, cos_ref[...], sin_ref[...], p_ref, ATTN_HKV, 1.0)
        p_ref[:, ATTN_HKV * HEAD_DIM:] = acc[:, ATTN_HKV * HEAD_DIM:].astype(BF16)

    pl.when(j == J_RQ)(lambda: rotated(Q_SCALE))
    pl.when(j == J_RK)(lambda: rotated(1.0))
    pl.when((j > J_AKAV) & (j != J_RQ) & (j != J_RK) & active)(plain)


def _inproj_call(l, x_src, tile0, mod, w_t, w_dt, cos2, sin2, skip_ctx_cols, prev=None):
    lat_tiles = T_LAT // TM_IN

    def mod_map(k):
        return lambda i, j: (_batch_of_tile(i + tile0, TM_IN) * 6 + k, 0, 0)

    def rope_map(i, j):
        g = i + tile0
        return (jnp.where(g < lat_tiles, g % (SEQ // TM_IN), SEQ // TM_IN), 0)

    in_specs = [
        pl.BlockSpec((TM_IN, D_MODEL), lambda i, j: (i, 0)),
        pl.BlockSpec((None, 1, D_MODEL), mod_map(0)),
        pl.BlockSpec((None, 1, D_MODEL), mod_map(1)),
        pl.BlockSpec((None, TN_IN, D_MODEL), lambda i, j: (l, j, 0)),
        pl.BlockSpec((None, DT_PAD, D_MODEL), lambda i, j: (l, 0, 0)),
        pl.BlockSpec((TM_IN, HEAD_DIM), rope_map),
        pl.BlockSpec((TM_IN, HEAD_DIM), rope_map),
    ]
    args = [x_src, mod, mod, w_t, w_dt, cos2, sin2]
    aliases = {}
    if prev is not None:
        aliases = {len(args): 0, len(args) + 1: 1}
        in_specs += [pl.BlockSpec(memory_space=pl.ANY)] * 2
        args += list(prev)
    return pl.pallas_call(
        functools.partial(_inproj_kernel, skip_ctx_cols, tile0, prev is not None),
        grid=(x_src.shape[0] // TM_IN, P_COLS // TN_IN),
        in_specs=in_specs,
        out_specs=[
            pl.BlockSpec((TM_IN, TN_IN), lambda i, j: (i + tile0, j)),
            pl.BlockSpec((TM_IN, DT_PAD), lambda i, j: (i + tile0, 0)),
        ],
        out_shape=[
            jax.ShapeDtypeStruct((T_ALL, P_COLS), BF16),
            jax.ShapeDtypeStruct((T_ALL, DT_PAD), F32),
        ],
        scratch_shapes=[pltpu.VMEM((TM_IN, D_MODEL), BF16)],
        input_output_aliases=aliases,
        compiler_params=_cparams(("arbitrary", "arbitrary")),
        name="inproj",
    )(*args)


TN_CONV = 512


def _conv_taps(prev, x, nxt, w_ref, b_ref, pos, slen):
    xe = jnp.concatenate([prev, x, nxt], axis=0)
    w = w_ref[...]
    acc = jnp.zeros((TM_CONV, TN_CONV), F32) + b_ref[...]
    half = SSD_CONV // 2
    for k in range(SSD_CONV):
        off = HALO - half + k
        xk = xe[off:off + TM_CONV, :]
        if pos is not None and k < half:
            xk = jnp.where(pos + (k - half) >= 0, xk, 0.0)
        elif pos is not None and k > half:
            xk = jnp.where(pos + (k - half) < slen, xk, 0.0)
        acc = acc + xk * w[k:k + 1, :]
    return _silu(acc).astype(BF16)


def _conv_kernel(x_ref, prev_ref, next_ref, w_ref, b_ref, o_ref):
    i = pl.program_id(0)
    tiles_per_seq = SEQ // TM_CONV
    x = x_ref[...].astype(F32)
    prev = prev_ref[...].astype(F32)
    nxt = next_ref[...].astype(F32)

    @pl.when(i < T_LAT // TM_CONV)
    def _():
        t = i & (tiles_per_seq - 1)
        o_ref[...] = _conv_taps(jnp.where(t == 0, 0.0, prev), x,
                                jnp.where(t == tiles_per_seq - 1, 0.0, nxt), w_ref, b_ref, None, None)

    @pl.when(i >= T_LAT // TM_CONV)
    def _():
        row = lax.broadcasted_iota(jnp.int32, (TM_CONV, TN_CONV), 0)
        o_ref[...] = _conv_taps(prev, x, nxt, w_ref, b_ref, row & (CTX_LEN - 1), CTX_LEN)


def _conv_call(p_all, conv_w, conv_b):
    cb0 = COL_XBC // TN_CONV
    hpt = TM_CONV // HALO
    last_halo = T_ALL // HALO - 1
    return pl.pallas_call(
        _conv_kernel,
        grid=(T_ALL // TM_CONV, SSD_CONV_CH // TN_CONV),
        in_specs=[
            pl.BlockSpec((TM_CONV, TN_CONV), lambda i, j: (i, cb0 + j)),
            pl.BlockSpec((HALO, TN_CONV), lambda i, j: (jnp.maximum(i * hpt - 1, 0), cb0 + j)),
            pl.BlockSpec((HALO, TN_CONV), lambda i, j: (jnp.minimum((i + 1) * hpt, last_halo), cb0 + j)),
            pl.BlockSpec((SSD_CONV, TN_CONV), lambda i, j: (0, j)),
            pl.BlockSpec((1, TN_CONV), lambda i, j: (0, j)),
        ],
        out_specs=pl.BlockSpec((TM_CONV, TN_CONV), lambda i, j: (i, j)),
        out_shape=jax.ShapeDtypeStruct((T_ALL, SSD_CONV_CH), BF16),
        compiler_params=_cparams(("arbitrary", "arbitrary")),
        name="conv",
    )(p_all, p_all, p_all, conv_w, conv_b.reshape(1, SSD_CONV_CH))


def _softmax_pv(s, vals, sink_col):
    m = jnp.maximum(jnp.max(s, axis=-1, keepdims=True), sink_col)
    p = jnp.exp2(s - m)
    l = jnp.sum(p, axis=-1, keepdims=True) + jnp.exp2(sink_col - m)
    return _dot(p.astype(BF16), vals) / l


def _group_queries(q_ref, hkv):
    h0 = hkv * ATTN_GROUP
    return jnp.concatenate(
        [q_ref[:, (h0 + g) * HEAD_DIM:(h0 + g + 1) * HEAD_DIM] for g in range(ATTN_GROUP)], axis=0)


def _sink_column(sink_ref, hkv, nq):
    return jnp.concatenate(
        [jnp.full((nq, 1), sink_ref[hkv * ATTN_GROUP + g] * LOG2E, F32) for g in range(ATTN_GROUP)], axis=0)


def _attn_kernel(sink_ref, q_ref, kp_ref, kc_ref, kn_ref, vp_ref, vc_ref, vn_ref, kx_ref, vx_ref,
                 o_ref):
    n = pl.program_id(1)
    rows = ATTN_GROUP * BLOCK
    qi = lax.broadcasted_iota(jnp.int32, (rows, BLOCK), 0) & (BLOCK - 1)
    kj = lax.broadcasted_iota(jnp.int32, (rows, BLOCK), 1)
    prev_ok = (kj >= qi) & (n > 0)
    next_ok = (kj <= qi) & (n < SEQ // BLOCK - 1)

    def mask_fn(s):
        return jnp.concatenate([
            jnp.where(prev_ok, s[:, :BLOCK], NEG_INF), s[:, BLOCK:2 * BLOCK],
            jnp.where(next_ok, s[:, 2 * BLOCK:3 * BLOCK], NEG_INF), s[:, 3 * BLOCK:]], axis=1)

    kv_cols = lambda hkv: slice(hkv * HEAD_DIM, (hkv + 1) * HEAD_DIM)
    scores = []
    for hkv in range(ATTN_HKV):
        c = kv_cols(hkv)
        keys = jnp.concatenate([kp_ref[:, c], kc_ref[:, c], kn_ref[:, c], kx_ref[:, c]], axis=0)
        scores.append(mask_fn(_dot_nt(_group_queries(q_ref, hkv), keys)))
    outs = []
    for hkv in range(ATTN_HKV):
        c = kv_cols(hkv)
        vals = jnp.concatenate([vp_ref[:, c], vc_ref[:, c], vn_ref[:, c], vx_ref[:, c]], axis=0)
        o = _softmax_pv(scores[hkv], vals, _sink_column(sink_ref, hkv, BLOCK))
        outs += [o[g * BLOCK:(g + 1) * BLOCK, :] for g in range(ATTN_GROUP)]
    o_ref[...] = jnp.concatenate(outs, axis=1).astype(BF16)


def _ctx_attn_kernel(sink_ref, q_ref, kx_ref, vx_ref, prev_ref, o_ref):
    del prev_ref
    kv_cols = lambda hkv: slice(hkv * HEAD_DIM, (hkv + 1) * HEAD_DIM)
    scores = [_dot_nt(_group_queries(q_ref, hkv), kx_ref[:, kv_cols(hkv)]) for hkv in range(ATTN_HKV)]
    outs = []
    for hkv in range(ATTN_HKV):
        o = _softmax_pv(scores[hkv], vx_ref[:, kv_cols(hkv)], _sink_column(sink_ref, hkv, CTX_LEN))
        outs += [o[g * CTX_LEN:(g + 1) * CTX_LEN, :] for g in range(ATTN_GROUP)]
    o_ref[...] = jnp.concatenate(outs, axis=1).astype(BF16)


def _attn_calls(p_all, sink, update_ctx):
    nb = SEQ // BLOCK
    kvw = ATTN_HKV * HEAD_DIM
    ck = COL_AK // kvw
    cv = COL_AV // kvw
    ctx0 = T_LAT // CTX_LEN
    smem = pl.BlockSpec(memory_space=pltpu.SMEM)

    def kv_spec(col, dn):
        return pl.BlockSpec((BLOCK, kvw), lambda b, n: (b * nb + jnp.clip(n + dn, 0, nb - 1), col))

    def ctx_spec(col):
        return pl.BlockSpec((CTX_LEN, kvw), lambda b, n: (ctx0 + b, col))

    attn = pl.pallas_call(
        _attn_kernel,
        grid=(BATCH, nb),
        in_specs=[
            smem,
            pl.BlockSpec((BLOCK, ATTN_WIDTH), lambda b, n: (b * nb + n, 0)),
            kv_spec(ck, -1), kv_spec(ck, 0), kv_spec(ck, 1),
            kv_spec(cv, -1), kv_spec(cv, 0), kv_spec(cv, 1),
            ctx_spec(ck), ctx_spec(cv),
        ],
        out_specs=pl.BlockSpec((BLOCK, ATTN_WIDTH), lambda b, n: (b * nb + n, 0)),
        out_shape=jax.ShapeDtypeStruct((T_ALL, ATTN_WIDTH), BF16),
        compiler_params=_cparams(("arbitrary", "arbitrary")),
        name="attn",
    )(sink, p_all, p_all, p_all, p_all, p_all, p_all, p_all, p_all, p_all)
    if not update_ctx:
        return attn
    return pl.pallas_call(
        _ctx_attn_kernel,
        grid=(BATCH,),
        in_specs=[
            smem,
            pl.BlockSpec((CTX_LEN, ATTN_WIDTH), lambda b: (ctx0 + b, 0)),
            pl.BlockSpec((CTX_LEN, kvw), lambda b: (ctx0 + b, ck)),
            pl.BlockSpec((CTX_LEN, kvw), lambda b: (ctx0 + b, cv)),
            pl.BlockSpec(memory_space=pl.ANY),
        ],
        out_specs=pl.BlockSpec((CTX_LEN, ATTN_WIDTH), lambda b: (ctx0 + b, 0)),
        out_shape=jax.ShapeDtypeStruct((T_ALL, ATTN_WIDTH), BF16),
        input_output_aliases={4: 0},
        compiler_params=_cparams(("arbitrary",)),
        name="ctx_attn",
    )(sink, p_all, p_all, p_all, attn)


def _fwd_chunk(b, s):
    ctx = T_LAT // CHUNK + b * NCHUNK_CTX + s
    lat = b * NCHUNK_LAT + (s - NCHUNK_CTX)
    return jnp.where(s < NCHUNK_CTX, ctx, lat)


def _bwd_chunk(b, s):
    ctx = T_LAT // CHUNK + b * NCHUNK_CTX + (NCHUNK_CTX - 1 - s)
    lat = b * NCHUNK_LAT + (NSTEP - 1 - s)
    return jnp.where(s < NCHUNK_CTX, ctx, lat)


def _ssd_kernel(xs_f, bm_f, cm_f, dt_f, xs_b, bm_b, cm_b, dt_b,
                bias_row, alog_row, bias_col, alog_col, yf_ref, yb_ref, h_scr):
    s = pl.program_id(1)

    @pl.when(s == 0)
    def _():
        h_scr[...] = jnp.zeros_like(h_scr)

    ii = lax.broadcasted_iota(jnp.int32, (CHUNK, CHUNK), 0)
    jj = lax.broadcasted_iota(jnp.int32, (CHUNK, CHUNK), 1)
    lane_lo = jj < SSD_P
    lower = jj <= ii
    upper = jj >= ii
    hg = SSD_HEADS // SSD_GROUPS
    dirs = ((xs_f, bm_f, cm_f, dt_f, yf_ref), (xs_b, bm_b, cm_b, dt_b, yb_ref))
    causal = (lower, upper)
    grp = lambda g: slice(g * SSD_STATE, (g + 1) * SSD_STATE)
    pair_lanes = lambda pair: slice(pair * 2 * SSD_P, (pair + 1) * 2 * SSD_P)

    acs, acs_t, row_t, dte_t, tot = {}, {}, {}, {}, {}
    for d in range(2):
        dt_ref = dirs[d][3]
        tri = jnp.where(causal[d], 1.0, 0.0).astype(BF16)
        tri_t = jnp.where(causal[1 - d], 1.0, 0.0).astype(BF16)
        last = CHUNK - 1 if d == 0 else 0
        r0 = d * SSD_HEADS
        dt_raw = dt_ref[...]
        dt_c = _softplus(dt_raw + bias_row[...])
        adt_c = dt_c * (-LOG2E * jnp.exp(alog_row[...]))
        acs[d] = sum(_dot(tri, p) for p in _split3(adt_c))
        dt_t = _softplus(dt_raw.T[r0:r0 + SSD_HEADS, :] + bias_col[r0:r0 + SSD_HEADS, :])
        adt_t = dt_t * (-LOG2E * jnp.exp(alog_col[r0:r0 + SSD_HEADS, :]))
        acs_t[d] = sum(_dot(p, tri_t) for p in _split3(adt_t))
        row_t[d] = acs_t[d] - jnp.log2(dt_t)
        dte_t[d] = jnp.exp2(acs_t[d][:, last:last + 1] - acs_t[d]) * dt_t
        tot[d] = acs[d][last:last + 1, :]

    cb, bm_t, cm = {}, {}, {}
    for d in range(2):
        _, bm_ref, cm_ref, _, _ = dirs[d]
        for g in range(SSD_GROUPS):
            cm[d, g] = cm_ref[:, grp(g)]
            cb[d, g] = _dot_nt(cm[d, g], bm_ref[:, grp(g)]).astype(BF16)
            bm_t[d, g] = bm_ref[:, grp(g)].astype(F32).T

    for d in range(2):
        xs_ref, y_ref = dirs[d][0], dirs[d][4]
        r0 = d * SSD_HEADS
        y_pairs, h_pairs = [], []
        for pair in range(SSD_HEADS // 2):
            g = pair // (hg // 2)
            x_pair = xs_ref[:, pair_lanes(pair)]
            h_pair = h_scr[d, :, pair_lanes(pair)]
            rhs = jnp.concatenate([x_pair, h_pair.astype(BF16)], axis=0)
            ys, ups, cds = [], [], []
            for h in (2 * pair, 2 * pair + 1):
                col = jnp.broadcast_to(acs[d][:, r0 + h:r0 + h + 1], (CHUNK, CHUNK))
                dec = jnp.exp2(jnp.where(causal[d], col - row_t[d][h:h + 1, :], NEG_INF))
                m_intra = cb[d, g] * dec.astype(BF16)
                m_state = cm[d, g] * jnp.exp2(col).astype(BF16)
                ys.append(_dot(jnp.concatenate([m_intra, m_state], axis=1), rhs))
                ups.append(_dot((bm_t[d, g] * dte_t[d][h:h + 1, :]).astype(BF16), x_pair))
                cds.append(jnp.exp2(tot[d][:, r0 + h:r0 + h + 1]))
            y_pairs.append(jnp.where(lane_lo, ys[0], ys[1]))
            h_pairs.append(jnp.where(lane_lo, cds[0], cds[1]) * h_pair
                           + jnp.where(lane_lo, ups[0], ups[1]))
        y_ref[...] = jnp.concatenate(y_pairs, axis=1).astype(BF16)
        h_scr[d] = jnp.concatenate(h_pairs, axis=1)


def _ssd_call(xbc, dt_all, bias_row, alog_row, bias_col, alog_col):
    bcol = SSD_INNER // (SSD_GROUPS * SSD_STATE)

    def specs(chunk_fn):
        return [
            pl.BlockSpec((CHUNK, SSD_INNER), lambda b, s: (chunk_fn(b, s), 0)),
            pl.BlockSpec((CHUNK, SSD_GROUPS * SSD_STATE), lambda b, s: (chunk_fn(b, s), bcol)),
            pl.BlockSpec((CHUNK, SSD_GROUPS * SSD_STATE), lambda b, s: (chunk_fn(b, s), bcol + 1)),
            pl.BlockSpec((CHUNK, DT_PAD), lambda b, s: (chunk_fn(b, s), 0)),
        ]

    const = lambda shape: pl.BlockSpec(shape, lambda b, s: (0, 0))
    return pl.pallas_call(
        _ssd_kernel,
        grid=(BATCH, NSTEP),
        in_specs=specs(_fwd_chunk) + specs(_bwd_chunk) + [
            const((1, DT_PAD)), const((1, DT_PAD)), const((DT_PAD, 1)), const((DT_PAD, 1))],
        out_specs=[
            pl.BlockSpec((CHUNK, SSD_INNER), lambda b, s: (_fwd_chunk(b, s), 0)),
            pl.BlockSpec((CHUNK, SSD_INNER), lambda b, s: (_bwd_chunk(b, s), 0)),
        ],
        out_shape=[jax.ShapeDtypeStruct((T_ALL, SSD_INNER), BF16)] * 2,
        scratch_shapes=[pltpu.VMEM((2, SSD_STATE, SSD_INNER), F32)],
        compiler_params=_cparams(("arbitrary", "arbitrary")),
        name="ssd",
    )(xbc, xbc, xbc, dt_all, xbc, xbc, xbc, dt_all, bias_row, alog_row, bias_col, alog_col)


def _ret_kernel(lg_ref, q_f, k_f, v_f, q_b, k_b, v_b, of_ref, ob_ref, s_scr, tab_scr):
    s = pl.program_id(1)

    @pl.when(s == 0)
    def _():
        s_scr[...] = jnp.zeros_like(s_scr)
        ii = lax.broadcasted_iota(jnp.int32, (CHUNK, CHUNK), 0)
        jj = lax.broadcasted_iota(jnp.int32, (CHUNK, CHUNK), 1)
        for d in range(2):
            if d == 0:
                dist = (ii - jj).astype(F32)
                row_pow = (ii + 1).astype(F32)
                key_pow = (CHUNK - 1 - ii).astype(F32)
            else:
                dist = (jj - ii).astype(F32)
                row_pow = (CHUNK - ii).astype(F32)
                key_pow = ii.astype(F32)
            for h in range(RET_HEADS):
                lg = lg_ref[d, h]
                tab_scr[d, h, 0] = jnp.where(dist >= 0.0, jnp.exp(jnp.maximum(dist, 0.0) * lg), 0.0)
                tab_scr[d, h, 1] = jnp.exp(row_pow * lg)
                tab_scr[d, h, 2] = jnp.exp(key_pow * lg)

    dirs = ((q_f, k_f, v_f, of_ref), (q_b, k_b, v_b, ob_ref))
    heads = [(d, h) for d in range(2) for h in range(RET_HEADS)]
    lanes = lambda h: slice(h * HEAD_DIM, (h + 1) * HEAD_DIM)
    raw = {}
    for d, h in heads:
        q_ref, k_ref, v_ref, _ = dirs[d]
        raw[d, h] = _dot_nt(q_ref[:, lanes(h)], k_ref[:, lanes(h)])
    kv = {}
    for d, h in heads:
        _, k_ref, v_ref, _ = dirs[d]
        k_dec = (k_ref[:, lanes(h)].astype(F32) * tab_scr[d, h, 2]).astype(BF16)
        kv[d, h] = _dot_tn(k_dec, v_ref[:, lanes(h)])
    outs = {}
    for d, h in heads:
        q_ref, _, v_ref, _ = dirs[d]
        scores = raw[d, h] * tab_scr[d, h, 0]
        q_cross = q_ref[:, lanes(h)].astype(F32) * tab_scr[d, h, 1]
        state = s_scr[d, h * HEAD_DIM:(h + 1) * HEAD_DIM, :]
        lhs = jnp.concatenate([scores.astype(BF16), q_cross.astype(BF16)], axis=1)
        rhs = jnp.concatenate([v_ref[:, lanes(h)], state.astype(BF16)], axis=0)
        outs[d, h] = _dot(lhs, rhs)
    for d in range(2):
        states = []
        for h in range(RET_HEADS):
            chunk_decay = jnp.exp(jnp.full((1, HEAD_DIM), float(CHUNK), F32) * lg_ref[d, h])
            states.append(chunk_decay * s_scr[d, h * HEAD_DIM:(h + 1) * HEAD_DIM, :] + kv[d, h])
        dirs[d][3][...] = jnp.concatenate([outs[d, h] for h in range(RET_HEADS)], axis=1).astype(BF16)
        s_scr[d] = jnp.concatenate(states, axis=0)


def _ret_call(p_all, log_decay):
    def specs(chunk_fn):
        return [
            pl.BlockSpec((CHUNK, RET_WIDTH), lambda b, s: (chunk_fn(b, s), COL_RQ // RET_WIDTH)),
            pl.BlockSpec((CHUNK, RET_WIDTH), lambda b, s: (chunk_fn(b, s), COL_RK // RET_WIDTH)),
            pl.BlockSpec((CHUNK, RET_WIDTH), lambda b, s: (chunk_fn(b, s), COL_RV // RET_WIDTH)),
        ]

    return pl.pallas_call(
        _ret_kernel,
        grid=(BATCH, NSTEP),
        in_specs=[pl.BlockSpec(memory_space=pltpu.SMEM)] + specs(_fwd_chunk) + specs(_bwd_chunk),
        out_specs=[
            pl.BlockSpec((CHUNK, RET_WIDTH), lambda b, s: (_fwd_chunk(b, s), 0)),
            pl.BlockSpec((CHUNK, RET_WIDTH), lambda b, s: (_bwd_chunk(b, s), 0)),
        ],
        out_shape=[jax.ShapeDtypeStruct((T_ALL, RET_WIDTH), BF16)] * 2,
        scratch_shapes=[pltpu.VMEM((2, RET_HEADS * HEAD_DIM, HEAD_DIM), F32),
                        pltpu.VMEM((2, RET_HEADS, 3, CHUNK, CHUNK), F32)],
        compiler_params=_cparams(("arbitrary", "arbitrary")),
        name="retention",
    )(log_decay, p_all, p_all, p_all, p_all, p_all, p_all)


def _merge_kernel(attn_ref, yf_ref, yb_ref, xs_ref, z0_ref, z1_ref, of_ref, ob_ref, rg_ref,
                  ga_ref, gs_ref, gr_ref, dskip_ref, ssd_g_ref, ret_g_ref,
                  wa_ref, ws_ref, wr_ref, o_ref):
    z = jnp.concatenate([z0_ref[...], z1_ref[...]], axis=1).astype(F32)
    y = (yf_ref[...].astype(F32) + yb_ref[...].astype(F32)
         + dskip_ref[...] * xs_ref[...].astype(F32))
    y = y * _silu(z)
    ssd_o = y * lax.rsqrt(jnp.mean(y * y, axis=-1, keepdims=True) + LN_EPS) * ssd_g_ref[...]

    o = of_ref[...].astype(F32) + ob_ref[...].astype(F32)
    normed = jnp.concatenate(
        [_ln_rows(o[:, h * HEAD_DIM:(h + 1) * HEAD_DIM]) for h in range(RET_HEADS)], axis=1)
    ret_o = normed * ret_g_ref[...] * _silu(rg_ref[...].astype(F32))

    merged = (_sigmoid(ga_ref[...].astype(F32)) * _dot(attn_ref[...], wa_ref[...])
              + _sigmoid(gs_ref[...].astype(F32)) * _dot(ssd_o.astype(BF16), ws_ref[...])
              + _sigmoid(gr_ref[...].astype(F32)) * _dot(ret_o.astype(BF16), wr_ref[...]))
    o_ref[...] = merged.astype(BF16)


def _merge_call(l, n_rows, attn, yf, yb, xbc, p_all, of, ob, dskip, ssd_g, ret_g, wa, ws, wr):
    tm = TM_MERGE
    zw = SSD_INNER // 2
    w1024 = lambda c: pl.BlockSpec((tm, 1024), lambda i: (i, c))
    zspec = lambda c: pl.BlockSpec((tm, zw), lambda i: (i, COL_Z // zw + c))
    gate = lambda c: pl.BlockSpec((tm, D_MODEL), lambda i: (i, COL_GATES // D_MODEL + c))
    vec = pl.BlockSpec((1, 1024), lambda i: (0, 0))
    wspec = pl.BlockSpec((None, 1024, D_MODEL), lambda i: (l, 0, 0), pipeline_mode=pl.Buffered(1))
    return pl.pallas_call(
        _merge_kernel,
        grid=(n_rows // tm,),
        in_specs=[
            w1024(0), w1024(0), w1024(0), w1024(0), zspec(0), zspec(1),
            w1024(0), w1024(0), w1024(COL_RG // 1024),
            gate(0), gate(1), gate(2), vec, vec, vec, wspec, wspec, wspec,
        ],
        out_specs=pl.BlockSpec((tm, D_MODEL), lambda i: (i, 0)),
        out_shape=jax.ShapeDtypeStruct((n_rows, D_MODEL), BF16),
        compiler_params=_cparams(("arbitrary",)),
        name="merge",
    )(attn, yf, yb, xbc, p_all, p_all, of, ob, p_all, p_all, p_all, p_all, dskip, ssd_g, ret_g,
      wa, ws, wr)


def _deepnorm(x, y, gate, g, b):
    return _ln_rows(DEEPNORM_ALPHA * x + gate * y) * g + b


def _outproj_kernel(m_ref, x_ref, w_ref, gate_ref, g_ref, b_ref, *rest):
    o_ref = rest[-1]
    for r in range(0, TM_OUT, RC_OUT):
        rows = slice(r, r + RC_OUT)
        mix = _dot(m_ref[rows, :], w_ref[...])
        o_ref[rows, :] = _deepnorm(x_ref[rows, :], mix, gate_ref[...], g_ref[...], b_ref[...])


def _outproj_call(l, n_rows, n_tiles, merged, x_src, tile0, w_out, mod, ln_g, ln_b, prev=None):
    tm = TM_OUT
    vec = pl.BlockSpec((1, D_MODEL), lambda i: (0, 0))
    in_specs = [
        pl.BlockSpec((tm, D_MODEL), lambda i: (i + tile0, 0)),
        pl.BlockSpec((tm, D_MODEL), lambda i: (i, 0)),
        pl.BlockSpec((None, D_MODEL, D_MODEL), lambda i: (l, 0, 0), pipeline_mode=pl.Buffered(1)),
        pl.BlockSpec((None, 1, D_MODEL), lambda i: (_batch_of_tile(i + tile0, tm) * 6 + 2, 0, 0)),
        vec, vec,
    ]
    args = [merged, x_src, w_out, mod, ln_g, ln_b]
    aliases = {}
    if prev is not None:
        aliases = {len(args): 0}
        in_specs.append(pl.BlockSpec(memory_space=pl.ANY))
        args.append(prev)
    return pl.pallas_call(
        _outproj_kernel,
        grid=(n_tiles,),
        in_specs=in_specs,
        out_specs=pl.BlockSpec((tm, D_MODEL), lambda i: (i + tile0, 0)),
        out_shape=jax.ShapeDtypeStruct((n_rows, D_MODEL), F32),
        input_output_aliases=aliases,
        compiler_params=_cparams(("arbitrary",)),
        name="outproj",
    )(*args)


def _mlp_kernel(x_ref, shift_ref, scale_ref, gate_ref, wup_ref, wdn_ref, g_ref, b_ref,
                o_ref, h_scr, acc_scr):
    j = pl.program_id(1)
    last = pl.num_programs(1) - 1
    chunks = [slice(r, r + RC_MLP) for r in range(0, TM_MLP, RC_MLP)]

    def up_down(rows):
        u = jnp.maximum(_dot(h_scr[rows, :], wup_ref[...]), 0.0)
        return _dot((u * u).astype(BF16), wdn_ref[...])

    @pl.when(j == 0)
    def _():
        for rows in chunks:
            h = _ln_rows(x_ref[rows, :]) * (1.0 + scale_ref[...]) + shift_ref[...]
            h_scr[rows, :] = h.astype(BF16)
        for rows in chunks:
            acc_scr[rows, :] = up_down(rows)

    @pl.when((j > 0) & (j < last))
    def _():
        acc_scr[...] += up_down(slice(None))

    @pl.when(j == last)
    def _():
        for rows in chunks:
            y = acc_scr[rows, :] + up_down(rows)
            o_ref[rows, :] = _deepnorm(x_ref[rows, :], y, gate_ref[...], g_ref[...], b_ref[...])


def _mlp_call(l, n_rows, x1, mod, w_up, w_down, ln_g, ln_b):
    tm, tf = TM_MLP, TF_MLP

    def mod_map(k):
        return lambda i, j: (_batch_of_tile(i, tm) * 6 + k, 0, 0)

    vec = pl.BlockSpec((1, D_MODEL), lambda i, j: (0, 0))
    return pl.pallas_call(
        _mlp_kernel,
        grid=(n_rows // tm, D_FF // tf),
        in_specs=[
            pl.BlockSpec((tm, D_MODEL), lambda i, j: (i, 0)),
            pl.BlockSpec((None, 1, D_MODEL), mod_map(3)),
            pl.BlockSpec((None, 1, D_MODEL), mod_map(4)),
            pl.BlockSpec((None, 1, D_MODEL), mod_map(5)),
            pl.BlockSpec((None, None, D_MODEL, tf), lambda i, j: (l, j, 0, 0)),
            pl.BlockSpec((None, tf, D_MODEL), lambda i, j: (l, j, 0)),
            vec, vec,
        ],
        out_specs=pl.BlockSpec((tm, D_MODEL), lambda i, j: (i, 0)),
        out_shape=jax.ShapeDtypeStruct((n_rows, D_MODEL), F32),
        scratch_shapes=[pltpu.VMEM((tm, D_MODEL), BF16), pltpu.VMEM((tm, D_MODEL), F32)],
        compiler_params=_cparams(("arbitrary", "arbitrary")),
        name="mlp",
    )(x1, mod, mod, mod, w_up, w_down, ln_g, ln_b)


def _rope_tables():
    f32 = np.float32
    rows = SEQ // GRID_W
    row = np.repeat(np.arange(rows), GRID_W).astype(f32)
    col = (np.arange(rows * GRID_W) % GRID_W).astype(f32)
    n_freq = HEAD_DIM // 4
    inv = (f32(ROPE_BASE) ** (-np.arange(n_freq, dtype=f32) / f32(n_freq))).astype(f32)
    ang = np.concatenate([row[:, None] * inv, col[:, None] * inv], axis=-1).astype(f32)
    cos, sin = np.cos(ang).astype(f32), np.sin(ang).astype(f32)
    cos2 = np.concatenate([cos, cos], axis=-1)
    sin2 = np.concatenate([-sin, sin], axis=-1)
    cos2 = np.concatenate([cos2, np.ones((TM_IN, HEAD_DIM), f32)], axis=0)
    sin2 = np.concatenate([sin2, np.zeros((TM_IN, HEAD_DIM), f32)], axis=0)
    return jnp.asarray(cos2), jnp.asarray(sin2)


def _pad_heads(v):
    flat = jnp.pad(v.reshape(-1).astype(F32), (0, DT_PAD - DT_COLS))
    return flat.reshape(1, DT_PAD), flat.reshape(DT_PAD, 1)


def kernel(x, c, ctx, c_ctx, ada_w, ada_b, w_in, attn_sink, ssd_conv_w, ssd_conv_b, ssd_a_log,
           ssd_dt_bias, ssd_d, ssd_norm_g, ret_log_decay, ret_norm_g, w_branch_attn, w_branch_ssd,
           w_branch_ret, w_out, ln1_g, ln1_b, w_mlp_up, w_mlp_down, ln2_g, ln2_b):
    lat_tiles_in, lat_tiles_out = T_LAT // TM_IN, T_LAT // TM_OUT
    sources = [(x.reshape(T_LAT, D_MODEL), 0, 0), (ctx.reshape(T_CTX, D_MODEL), lat_tiles_in, lat_tiles_out)]
    cond = jnp.concatenate([c, c_ctx[None, :], jnp.zeros((8 - BATCH - 1, D_MODEL), F32)], axis=0)
    mod_all = _ada_call(cond, ada_w, ada_b).reshape(DEPTH, 8 * 6, 1, D_MODEL)
    cos2, sin2 = _rope_tables()
    wb_attn, wb_ssd, wb_ret = (_cast_stacked(w) for w in (w_branch_attn, w_branch_ssd, w_branch_ret))
    w_out_b, w_down_b = _cast_stacked(w_out), _cast_stacked(w_mlp_down)
    w_up_b = _cast_col_tiled(w_mlp_up, TF_MLP)
    w_in_t, w_dt = _cast_w_in(w_in)

    for l in range(DEPTH):
        update_ctx = l < DEPTH - 1
        n_rows = T_ALL if update_ctx else T_LAT
        mod = mod_all[l]
        proj = None
        for x_src, tile_in, _ in sources:
            proj = _inproj_call(l, x_src, tile_in, mod, w_in_t, w_dt, cos2, sin2, not update_ctx, proj)
        p_all, dt_all = proj
        xbc = _conv_call(p_all, ssd_conv_w[l], ssd_conv_b[l])

        attn = _attn_calls(p_all, attn_sink[l].astype(F32), update_ctx)

        bias_row, bias_col = _pad_heads(ssd_dt_bias[l])
        alog_row, alog_col = _pad_heads(ssd_a_log[l])
        yf, yb = _ssd_call(xbc, dt_all, bias_row, alog_row, bias_col, alog_col)
        of, ob = _ret_call(p_all, ret_log_decay[l].astype(F32))

        dskip = jnp.repeat(ssd_d[l].astype(F32), SSD_P).reshape(1, SSD_INNER)
        merged = _merge_call(
            l, n_rows, attn, yf, yb, xbc, p_all, of, ob, dskip,
            ssd_norm_g[l].reshape(1, SSD_INNER), ret_norm_g[l].reshape(1, RET_WIDTH), wb_attn, wb_ssd, wb_ret)
        x1 = None
        for x_src, _, tile_out in sources:
            n_tiles = min(x_src.shape[0], n_rows - tile_out * TM_OUT) // TM_OUT
            x1 = _outproj_call(l, n_rows, n_tiles, merged, x_src, tile_out, w_out_b, mod,
                               ln1_g[l].reshape(1, D_MODEL), ln1_b[l].reshape(1, D_MODEL), x1)
        x_all = _mlp_call(l, n_rows, x1, mod, w_up_b, w_down_b,
                          ln2_g[l].reshape(1, D_MODEL), ln2_b[l].reshape(1, D_MODEL))
        sources = [(x_all, 0, 0)]
    return x_all.reshape(BATCH, SEQ, D_MODEL)
```

```python
import functools

import jax
import jax.numpy as jnp
from jax import lax
from jax.experimental import pallas as pl
from jax.experimental.pallas import tpu as pltpu
import numpy as np

F32 = jnp.float32
BF16 = jnp.bfloat16

D_MODEL = 2048
BATCH = 4
SEQ = 4096
DEPTH = 2
GRID_W = 64
CTX_LEN = 256
BLOCK = 128
HEAD_DIM = 128
ATTN_HQ = 8
ATTN_HKV = 2
ATTN_GROUP = ATTN_HQ // ATTN_HKV
ATTN_WIDTH = ATTN_HQ * HEAD_DIM
SSD_HEADS = 16
SSD_P = 64
SSD_INNER = SSD_HEADS * SSD_P
SSD_GROUPS = 2
SSD_STATE = 128
SSD_CONV = 5
SSD_CONV_CH = SSD_INNER + 2 * SSD_GROUPS * SSD_STATE
CHUNK = 128
RET_HEADS = 8
RET_WIDTH = RET_HEADS * HEAD_DIM
D_FF = 4 * D_MODEL
ROPE_BASE = 10000.0
DEEPNORM_ALPHA = (2 * DEPTH) ** 0.25
LN_EPS = 1e-6
NEG_INF = -1e30
Q_SCALE = HEAD_DIM ** -0.5
LOG2E = 1.4426950408889634

T_LAT = BATCH * SEQ
T_CTX = BATCH * CTX_LEN
T_ALL = T_LAT + T_CTX
NCHUNK_LAT = SEQ // CHUNK
NCHUNK_CTX = CTX_LEN // CHUNK
NSTEP = NCHUNK_LAT + NCHUNK_CTX

COL_AQ = 0
COL_AK = 1024
COL_AV = 1280
COL_Z = 1536
COL_XBC = 2560
COL_DT = 4096
COL_RQ = 4096
COL_RK = 5120
COL_RV = 6144
COL_RG = 7168
COL_GATES = 8192
P_COLS = 14336
DT_COLS = 2 * SSD_HEADS
DT_PAD = 128

TM_IN = 1024
RC_IN = 256
TN_IN = 1024
J_AQ, J_AKAV, J_RQ, J_RK = COL_AQ // TN_IN, COL_AK // TN_IN, COL_RQ // TN_IN, COL_RK // TN_IN
J_SPLIT = COL_DT // TN_IN
J_RG, J_GATES = COL_RG // TN_IN, COL_GATES // TN_IN
TM_MERGE = 256
TM_OUT = 1024
RC_OUT = 256
TM_MLP = 512
TF_MLP = 1024
RC_MLP = 256
TM_CONV = 1024
HALO = 16

VMEM_LIMIT = 56 * 1024 * 1024


def _cparams(sem):
    return pltpu.CompilerParams(dimension_semantics=sem, vmem_limit_bytes=VMEM_LIMIT)


def _sigmoid(x):
    return 0.5 * jnp.tanh(0.5 * x) + 0.5


def _silu(x):
    return x * _sigmoid(x)


def _softplus(x):
    return jnp.maximum(x, 0.0) + jnp.log1p(jnp.exp(-jnp.abs(x)))


def _ln_rows(x):
    mu = jnp.mean(x, axis=-1, keepdims=True)
    xc = x - mu
    var = jnp.mean(xc * xc, axis=-1, keepdims=True)
    return xc * lax.rsqrt(var + LN_EPS)


def _dot(a, b):
    return jnp.dot(a, b, preferred_element_type=F32)


def _dot_nt(a, b):
    return lax.dot_general(a, b, (((1,), (1,)), ((), ())), preferred_element_type=F32)


def _dot_tn(a, b):
    return lax.dot_general(a, b, (((0,), (0,)), ((), ())), preferred_element_type=F32)


def _split3(x):
    x1 = x.astype(BF16)
    r1 = x - x1.astype(F32)
    x2 = r1.astype(BF16)
    r2 = r1 - x2.astype(F32)
    return x1, x2, r2.astype(BF16)


def _p_spec(rows, width, col, row_map):
    assert col % width == 0 and TN_IN % width == 0
    tile, sub = col // TN_IN, (col % TN_IN) // width
    return pl.BlockSpec((None, rows, width), lambda *g: (tile, row_map(*g), sub))


def _batch_of_tile(i, tm):
    return jnp.where(i < T_LAT // tm, i // (SEQ // tm), BATCH)


CAST_BLOCK_BYTES = 8 * 1024 * 1024


def _cast_kernel(w_ref, o_ref):
    o_ref[...] = w_ref[...].astype(BF16)


def _cast_stacked(w):
    depth, rows, cols = w.shape
    rb = min(rows, CAST_BLOCK_BYTES // (cols * 4))
    spec = pl.BlockSpec((None, rb, cols), lambda l, i: (l, i, 0))
    return pl.pallas_call(
        _cast_kernel,
        grid=(depth, rows // rb),
        in_specs=[spec],
        out_specs=spec,
        out_shape=jax.ShapeDtypeStruct(w.shape, BF16),
        compiler_params=_cparams(("arbitrary", "arbitrary")),
        name="cast",
    )(w)


def _cast_col_tiled(w, tile_cols):
    depth, rows, cols = w.shape
    return pl.pallas_call(
        _cast_kernel,
        grid=(depth, cols // tile_cols),
        in_specs=[pl.BlockSpec((None, rows, tile_cols), lambda l, j: (l, 0, j))],
        out_specs=pl.BlockSpec((None, None, rows, tile_cols), lambda l, j: (l, j, 0, 0)),
        out_shape=jax.ShapeDtypeStruct((depth, cols // tile_cols, rows, tile_cols), BF16),
        compiler_params=_cparams(("arbitrary", "arbitrary")),
        name="cast_tiled",
    )(w)


def _cast_w_in(w_in):
    w_t = jnp.swapaxes(w_in, 1, 2)
    depth, cols, k = w_t.shape

    def src_map(l, j):
        return (l, (j * (TN_IN // DT_COLS) + jnp.where(j >= J_SPLIT, 1, 0)) * DT_COLS, 0)

    w_main = pl.pallas_call(
        _cast_kernel,
        grid=(depth, P_COLS // TN_IN),
        in_specs=[pl.BlockSpec((pl.Squeezed(), pl.Element(TN_IN), pl.Element(k)), src_map)],
        out_specs=pl.BlockSpec((None, TN_IN, k), lambda l, j: (l, j, 0)),
        out_shape=jax.ShapeDtypeStruct((depth, P_COLS, k), BF16),
        compiler_params=_cparams(("arbitrary", "arbitrary")),
        name="cast_w_in",
    )(w_t)
    w_dt = jnp.pad(w_t[:, COL_DT:COL_DT + DT_COLS, :], ((0, 0), (0, DT_PAD - DT_COLS), (0, 0))).astype(BF16)
    return w_main, w_dt


TN_ADA = 2048


def _ada_kernel(cond_ref, w0_ref, w1_ref, b_ref, o_ref):
    s = _silu(cond_ref[...]).astype(BF16)
    kh = D_MODEL // 2
    o_ref[...] = (_dot(s[:, :kh], w0_ref[...].astype(BF16))
                  + _dot(s[:, kh:], w1_ref[...].astype(BF16)) + b_ref[...])


def _ada_call(cond, ada_w, ada_b):
    n = ada_w.shape[-1]
    return pl.pallas_call(
        _ada_kernel,
        grid=(DEPTH, n // TN_ADA),
        in_specs=[
            pl.BlockSpec((8, D_MODEL), lambda l, j: (0, 0)),
            pl.BlockSpec((None, D_MODEL // 2, TN_ADA), lambda l, j: (l, 0, j)),
            pl.BlockSpec((None, D_MODEL // 2, TN_ADA), lambda l, j: (l, 1, j)),
            pl.BlockSpec((None, 1, TN_ADA), lambda l, j: (l, 0, j)),
        ],
        out_specs=pl.BlockSpec((None, 8, TN_ADA), lambda l, j: (l, 0, j)),
        out_shape=jax.ShapeDtypeStruct((DEPTH, 8, n), F32),
        compiler_params=_cparams(("arbitrary", "arbitrary")),
        name="ada",
    )(cond, ada_w, ada_w, ada_b.reshape(DEPTH, 1, n))


def _rope_store(acc, cos, sin, o_ref, n_heads, scale, rows=slice(None)):
    for h in range(n_heads):
        xs = acc[:, h * HEAD_DIM:(h + 1) * HEAD_DIM]
        rot = pltpu.roll(xs, HEAD_DIM // 2, axis=1)
        o_ref[rows, h * HEAD_DIM:(h + 1) * HEAD_DIM] = ((xs * cos + rot * sin) * scale).astype(BF16)


def _inproj_kernel(skip_ctx_cols, tile0, has_prev, x_ref, shift_ref, scale_ref, w_ref, wdt_ref,
                   cos_ref, sin_ref, *rest):
    p_ref, dt_ref, h_scr = rest[2:] if has_prev else rest
    i = pl.program_id(0) + tile0
    j = pl.program_id(1)
    chunks = [slice(r, r + RC_IN) for r in range(0, TM_IN, RC_IN)]
    heads_per_tile = TN_IN // HEAD_DIM
    if skip_ctx_cols:
        unused = (j == J_AQ) | (j == J_RG) | (j >= J_GATES)
        active = jnp.logical_not((i == T_ALL // TM_IN - 1) & unused)
    else:
        active = True

    def prologue(rows):
        h = _ln_rows(x_ref[rows, :]) * (1.0 + scale_ref[...]) + shift_ref[...]
        h_scr[rows, :] = h.astype(BF16)
        dt_ref[rows, :] = _dot_nt(h_scr[rows, :], wdt_ref[...])

    def rotated(scale):
        acc = _dot_nt(h_scr[...], w_ref[...])
        _rope_store(acc, cos_ref[...], sin_ref[...], p_ref, heads_per_tile, scale)

    def plain():
        p_ref[...] = _dot_nt(h_scr[...], w_ref[...]).astype(BF16)

    @pl.when((j == J_AQ) & active)
    def _():
        for rows in chunks:
            prologue(rows)
            acc = _dot_nt(h_scr[rows, :], w_ref[...])
            _rope_store(acc, cos_ref[rows, :], sin_ref[rows, :], p_ref, heads_per_tile,
                        Q_SCALE * LOG2E, rows)

    if skip_ctx_cols:
        @pl.when((j == J_AQ) & jnp.logical_not(active))
        def _():
            for rows in chunks:
                prologue(rows)

    @pl.when(j == J_AKAV)
    def _():
        acc = _dot_nt(h_scr[...], w_ref[...])
        _rope_store(acc, cos_ref[...], sin_ref[...], p_ref, ATTN_HKV, 1.0)
        p_ref[:, ATTN_HKV * HEAD_DIM:] = acc[:, ATTN_HKV * HEAD_DIM:].astype(BF16)

    pl.when(j == J_RQ)(lambda: rotated(Q_SCALE))
    pl.when(j == J_RK)(lambda: rotated(1.0))
    pl.when((j > J_AKAV) & (j != J_RQ) & (j != J_RK) & active)(plain)


def _inproj_call(l, x_src, tile0, mod, w_t, w_dt, cos2, sin2, skip_ctx_cols, prev=None):
    lat_tiles = T_LAT // TM_IN

    def mod_map(k):
        return lambda i, j: (_batch_of_tile(i + tile0, TM_IN) * 6 + k, 0, 0)

    def rope_map(i, j):
        g = i + tile0
        return (jnp.where(g < lat_tiles, g % (SEQ // TM_IN), SEQ // TM_IN), 0)

    in_specs = [
        pl.BlockSpec((TM_IN, D_MODEL), lambda i, j: (i, 0)),
        pl.BlockSpec((None, 1, D_MODEL), mod_map(0)),
        pl.BlockSpec((None, 1, D_MODEL), mod_map(1)),
        pl.BlockSpec((None, TN_IN, D_MODEL), lambda i, j: (l, j, 0)),
        pl.BlockSpec((None, DT_PAD, D_MODEL), lambda i, j: (l, 0, 0)),
        pl.BlockSpec((TM_IN, HEAD_DIM), rope_map),
        pl.BlockSpec((TM_IN, HEAD_DIM), rope_map),
    ]
    args = [x_src, mod, mod, w_t, w_dt, cos2, sin2]
    aliases = {}
    if prev is not None:
        aliases = {len(args): 0, len(args) + 1: 1}
        in_specs += [pl.BlockSpec(memory_space=pl.ANY)] * 2
        args += list(prev)
    return pl.pallas_call(
        functools.partial(_inproj_kernel, skip_ctx_cols, tile0, prev is not None),
        grid=(x_src.shape[0] // TM_IN, P_COLS // TN_IN),
        in_specs=in_specs,
        out_specs=[
            pl.BlockSpec((None, TM_IN, TN_IN), lambda i, j: (j, i + tile0, 0)),
            pl.BlockSpec((TM_IN, DT_PAD), lambda i, j: (i + tile0, 0)),
        ],
        out_shape=[
            jax.ShapeDtypeStruct((P_COLS // TN_IN, T_ALL, TN_IN), BF16),
            jax.ShapeDtypeStruct((T_ALL, DT_PAD), F32),
        ],
        scratch_shapes=[pltpu.VMEM((TM_IN, D_MODEL), BF16)],
        input_output_aliases=aliases,
        compiler_params=_cparams(("arbitrary", "arbitrary")),
        name="inproj",
    )(*args)


TN_CONV = 512


def _conv_taps(prev, x, nxt, w_ref, b_ref, pos, slen):
    xe = jnp.concatenate([prev, x, nxt], axis=0)
    w = w_ref[...]
    acc = jnp.zeros((TM_CONV, TN_CONV), F32) + b_ref[...]
    half = SSD_CONV // 2
    for k in range(SSD_CONV):
        off = HALO - half + k
        xk = xe[off:off + TM_CONV, :]
        if pos is not None and k < half:
            xk = jnp.where(pos + (k - half) >= 0, xk, 0.0)
        elif pos is not None and k > half:
            xk = jnp.where(pos + (k - half) < slen, xk, 0.0)
        acc = acc + xk * w[k:k + 1, :]
    return _silu(acc).astype(BF16)


def _conv_kernel(x_ref, prev_ref, next_ref, w_ref, b_ref, o_ref):
    i = pl.program_id(0)
    tiles_per_seq = SEQ // TM_CONV
    x = x_ref[...].astype(F32)
    prev = prev_ref[...].astype(F32)
    nxt = next_ref[...].astype(F32)

    @pl.when(i < T_LAT // TM_CONV)
    def _():
        t = i & (tiles_per_seq - 1)
        o_ref[...] = _conv_taps(jnp.where(t == 0, 0.0, prev), x,
                                jnp.where(t == tiles_per_seq - 1, 0.0, nxt), w_ref, b_ref, None, None)

    @pl.when(i >= T_LAT // TM_CONV)
    def _():
        row = lax.broadcasted_iota(jnp.int32, (TM_CONV, TN_CONV), 0)
        o_ref[...] = _conv_taps(prev, x, nxt, w_ref, b_ref, row & (CTX_LEN - 1), CTX_LEN)


def _conv_call(p_all, conv_w, conv_b):
    cb0 = COL_XBC // TN_CONV
    per = TN_IN // TN_CONV
    hpt = TM_CONV // HALO
    last_halo = T_ALL // HALO - 1
    return pl.pallas_call(
        _conv_kernel,
        grid=(T_ALL // TM_CONV, SSD_CONV_CH // TN_CONV),
        in_specs=[
            pl.BlockSpec((None, TM_CONV, TN_CONV), lambda i, j: ((cb0 + j) // per, i, (cb0 + j) % per)),
            pl.BlockSpec((None, HALO, TN_CONV),
                         lambda i, j: ((cb0 + j) // per, jnp.maximum(i * hpt - 1, 0), (cb0 + j) % per)),
            pl.BlockSpec((None, HALO, TN_CONV),
                         lambda i, j: ((cb0 + j) // per, jnp.minimum((i + 1) * hpt, last_halo), (cb0 + j) % per)),
            pl.BlockSpec((SSD_CONV, TN_CONV), lambda i, j: (0, j)),
            pl.BlockSpec((1, TN_CONV), lambda i, j: (0, j)),
        ],
        out_specs=pl.BlockSpec((TM_CONV, TN_CONV), lambda i, j: (i, j)),
        out_shape=jax.ShapeDtypeStruct((T_ALL, SSD_CONV_CH), BF16),
        compiler_params=_cparams(("arbitrary", "arbitrary")),
        name="conv",
    )(p_all, p_all, p_all, conv_w, conv_b.reshape(1, SSD_CONV_CH))


def _softmax_pv(s, vals, sink_col):
    m = jnp.maximum(jnp.max(s, axis=-1, keepdims=True), sink_col)
    p = jnp.exp2(s - m)
    l = jnp.sum(p, axis=-1, keepdims=True) + jnp.exp2(sink_col - m)
    return _dot(p.astype(BF16), vals) / l


def _group_queries(q_ref, hkv):
    h0 = hkv * ATTN_GROUP
    return jnp.concatenate(
        [q_ref[:, (h0 + g) * HEAD_DIM:(h0 + g + 1) * HEAD_DIM] for g in range(ATTN_GROUP)], axis=0)


def _sink_column(sink_ref, hkv, nq):
    return jnp.concatenate(
        [jnp.full((nq, 1), sink_ref[hkv * ATTN_GROUP + g] * LOG2E, F32) for g in range(ATTN_GROUP)], axis=0)


def _attn_kernel(sink_ref, q_ref, kp_ref, kc_ref, kn_ref, vp_ref, vc_ref, vn_ref, kx_ref, vx_ref,
                 o_ref):
    n = pl.program_id(1)
    rows = ATTN_GROUP * BLOCK
    qi = lax.broadcasted_iota(jnp.int32, (rows, BLOCK), 0) & (BLOCK - 1)
    kj = lax.broadcasted_iota(jnp.int32, (rows, BLOCK), 1)
    prev_ok = (kj >= qi) & (n > 0)
    next_ok = (kj <= qi) & (n < SEQ // BLOCK - 1)

    def mask_fn(s):
        return jnp.concatenate([
            jnp.where(prev_ok, s[:, :BLOCK], NEG_INF), s[:, BLOCK:2 * BLOCK],
            jnp.where(next_ok, s[:, 2 * BLOCK:3 * BLOCK], NEG_INF), s[:, 3 * BLOCK:]], axis=1)

    kv_cols = lambda hkv: slice(hkv * HEAD_DIM, (hkv + 1) * HEAD_DIM)
    scores = []
    for hkv in range(ATTN_HKV):
        c = kv_cols(hkv)
        keys = jnp.concatenate([kp_ref[:, c], kc_ref[:, c], kn_ref[:, c], kx_ref[:, c]], axis=0)
        scores.append(mask_fn(_dot_nt(_group_queries(q_ref, hkv), keys)))
    outs = []
    for hkv in range(ATTN_HKV):
        c = kv_cols(hkv)
        vals = jnp.concatenate([vp_ref[:, c], vc_ref[:, c], vn_ref[:, c], vx_ref[:, c]], axis=0)
        o = _softmax_pv(scores[hkv], vals, _sink_column(sink_ref, hkv, BLOCK))
        outs += [o[g * BLOCK:(g + 1) * BLOCK, :] for g in range(ATTN_GROUP)]
    o_ref[...] = jnp.concatenate(outs, axis=1).astype(BF16)


def _ctx_attn_kernel(sink_ref, q_ref, kx_ref, vx_ref, prev_ref, o_ref):
    del prev_ref
    kv_cols = lambda hkv: slice(hkv * HEAD_DIM, (hkv + 1) * HEAD_DIM)
    scores = [_dot_nt(_group_queries(q_ref, hkv), kx_ref[:, kv_cols(hkv)]) for hkv in range(ATTN_HKV)]
    outs = []
    for hkv in range(ATTN_HKV):
        o = _softmax_pv(scores[hkv], vx_ref[:, kv_cols(hkv)], _sink_column(sink_ref, hkv, CTX_LEN))
        outs += [o[g * CTX_LEN:(g + 1) * CTX_LEN, :] for g in range(ATTN_GROUP)]
    o_ref[...] = jnp.concatenate(outs, axis=1).astype(BF16)


def _attn_calls(p_all, sink, update_ctx):
    nb = SEQ // BLOCK
    kvw = ATTN_HKV * HEAD_DIM
    ctx0 = T_LAT // CTX_LEN
    smem = pl.BlockSpec(memory_space=pltpu.SMEM)

    def kv_spec(col, dn):
        return _p_spec(BLOCK, kvw, col, lambda b, n: b * nb + jnp.clip(n + dn, 0, nb - 1))

    def ctx_spec(col):
        return _p_spec(CTX_LEN, kvw, col, lambda b, *_: ctx0 + b)

    attn = pl.pallas_call(
        _attn_kernel,
        grid=(BATCH, nb),
        in_specs=[
            smem,
            _p_spec(BLOCK, ATTN_WIDTH, COL_AQ, lambda b, n: b * nb + n),
            kv_spec(COL_AK, -1), kv_spec(COL_AK, 0), kv_spec(COL_AK, 1),
            kv_spec(COL_AV, -1), kv_spec(COL_AV, 0), kv_spec(COL_AV, 1),
            ctx_spec(COL_AK), ctx_spec(COL_AV),
        ],
        out_specs=pl.BlockSpec((BLOCK, ATTN_WIDTH), lambda b, n: (b * nb + n, 0)),
        out_shape=jax.ShapeDtypeStruct((T_ALL, ATTN_WIDTH), BF16),
        compiler_params=_cparams(("arbitrary", "arbitrary")),
        name="attn",
    )(sink, p_all, p_all, p_all, p_all, p_all, p_all, p_all, p_all, p_all)
    if not update_ctx:
        return attn
    return pl.pallas_call(
        _ctx_attn_kernel,
        grid=(BATCH,),
        in_specs=[
            smem,
            _p_spec(CTX_LEN, ATTN_WIDTH, COL_AQ, lambda b: ctx0 + b),
            ctx_spec(COL_AK), ctx_spec(COL_AV),
            pl.BlockSpec(memory_space=pl.ANY),
        ],
        out_specs=pl.BlockSpec((CTX_LEN, ATTN_WIDTH), lambda b: (ctx0 + b, 0)),
        out_shape=jax.ShapeDtypeStruct((T_ALL, ATTN_WIDTH), BF16),
        input_output_aliases={4: 0},
        compiler_params=_cparams(("arbitrary",)),
        name="ctx_attn",
    )(sink, p_all, p_all, p_all, attn)


def _fwd_chunk(b, s):
    ctx = T_LAT // CHUNK + b * NCHUNK_CTX + s
    lat = b * NCHUNK_LAT + (s - NCHUNK_CTX)
    return jnp.where(s < NCHUNK_CTX, ctx, lat)


def _bwd_chunk(b, s):
    ctx = T_LAT // CHUNK + b * NCHUNK_CTX + (NCHUNK_CTX - 1 - s)
    lat = b * NCHUNK_LAT + (NSTEP - 1 - s)
    return jnp.where(s < NCHUNK_CTX, ctx, lat)


def _ssd_kernel(xs_f, bm_f, cm_f, dt_f, xs_b, bm_b, cm_b, dt_b,
                bias_row, alog_row, bias_col, alog_col, yf_ref, yb_ref, h_scr):
    s = pl.program_id(1)

    @pl.when(s == 0)
    def _():
        h_scr[...] = jnp.zeros_like(h_scr)

    ii = lax.broadcasted_iota(jnp.int32, (CHUNK, CHUNK), 0)
    jj = lax.broadcasted_iota(jnp.int32, (CHUNK, CHUNK), 1)
    lane_lo = jj < SSD_P
    lower = jj <= ii
    upper = jj >= ii
    hg = SSD_HEADS // SSD_GROUPS
    dirs = ((xs_f, bm_f, cm_f, dt_f, yf_ref), (xs_b, bm_b, cm_b, dt_b, yb_ref))
    causal = (lower, upper)
    grp = lambda g: slice(g * SSD_STATE, (g + 1) * SSD_STATE)
    pair_lanes = lambda pair: slice(pair * 2 * SSD_P, (pair + 1) * 2 * SSD_P)

    acs, acs_t, row_t, dte_t, tot = {}, {}, {}, {}, {}
    for d in range(2):
        dt_ref = dirs[d][3]
        tri = jnp.where(causal[d], 1.0, 0.0).astype(BF16)
        tri_t = jnp.where(causal[1 - d], 1.0, 0.0).astype(BF16)
        last = CHUNK - 1 if d == 0 else 0
        r0 = d * SSD_HEADS
        dt_raw = dt_ref[...]
        dt_c = _softplus(dt_raw + bias_row[...])
        adt_c = dt_c * (-LOG2E * jnp.exp(alog_row[...]))
        acs[d] = sum(_dot(tri, p) for p in _split3(adt_c))
        dt_t = _softplus(dt_raw.T[r0:r0 + SSD_HEADS, :] + bias_col[r0:r0 + SSD_HEADS, :])
        adt_t = dt_t * (-LOG2E * jnp.exp(alog_col[r0:r0 + SSD_HEADS, :]))
        acs_t[d] = sum(_dot(p, tri_t) for p in _split3(adt_t))
        row_t[d] = acs_t[d] - jnp.log2(dt_t)
        dte_t[d] = jnp.exp2(acs_t[d][:, last:last + 1] - acs_t[d]) * dt_t
        tot[d] = acs[d][last:last + 1, :]

    cb, bm_t, cm = {}, {}, {}
    for d in range(2):
        _, bm_ref, cm_ref, _, _ = dirs[d]
        for g in range(SSD_GROUPS):
            cm[d, g] = cm_ref[:, grp(g)]
            cb[d, g] = _dot_nt(cm[d, g], bm_ref[:, grp(g)]).astype(BF16)
            bm_t[d, g] = bm_ref[:, grp(g)].astype(F32).T

    for d in range(2):
        xs_ref, y_ref = dirs[d][0], dirs[d][4]
        r0 = d * SSD_HEADS
        y_pairs, h_pairs = [], []
        for pair in range(SSD_HEADS // 2):
            g = pair // (hg // 2)
            x_pair = xs_ref[:, pair_lanes(pair)]
            h_pair = h_scr[d, :, pair_lanes(pair)]
            rhs = jnp.concatenate([x_pair, h_pair.astype(BF16)], axis=0)
            ys, ups, cds = [], [], []
            for h in (2 * pair, 2 * pair + 1):
                col = jnp.broadcast_to(acs[d][:, r0 + h:r0 + h + 1], (CHUNK, CHUNK))
                dec = jnp.exp2(jnp.where(causal[d], col - row_t[d][h:h + 1, :], NEG_INF))
                m_intra = cb[d, g] * dec.astype(BF16)
                m_state = cm[d, g] * jnp.exp2(col).astype(BF16)
                ys.append(_dot(jnp.concatenate([m_intra, m_state], axis=1), rhs))
                ups.append(_dot((bm_t[d, g] * dte_t[d][h:h + 1, :]).astype(BF16), x_pair))
                cds.append(jnp.exp2(tot[d][:, r0 + h:r0 + h + 1]))
            y_pairs.append(jnp.where(lane_lo, ys[0], ys[1]))
            h_pairs.append(jnp.where(lane_lo, cds[0], cds[1]) * h_pair
                           + jnp.where(lane_lo, ups[0], ups[1]))
        y_ref[...] = jnp.concatenate(y_pairs, axis=1).astype(BF16)
        h_scr[d] = jnp.concatenate(h_pairs, axis=1)


def _ssd_call(xbc, dt_all, bias_row, alog_row, bias_col, alog_col):
    bcol = SSD_INNER // (SSD_GROUPS * SSD_STATE)

    def specs(chunk_fn):
        return [
            pl.BlockSpec((CHUNK, SSD_INNER), lambda b, s: (chunk_fn(b, s), 0)),
            pl.BlockSpec((CHUNK, SSD_GROUPS * SSD_STATE), lambda b, s: (chunk_fn(b, s), bcol)),
            pl.BlockSpec((CHUNK, SSD_GROUPS * SSD_STATE), lambda b, s: (chunk_fn(b, s), bcol + 1)),
            pl.BlockSpec((CHUNK, DT_PAD), lambda b, s: (chunk_fn(b, s), 0)),
        ]

    const = lambda shape: pl.BlockSpec(shape, lambda b, s: (0, 0))
    return pl.pallas_call(
        _ssd_kernel,
        grid=(BATCH, NSTEP),
        in_specs=specs(_fwd_chunk) + specs(_bwd_chunk) + [
            const((1, DT_PAD)), const((1, DT_PAD)), const((DT_PAD, 1)), const((DT_PAD, 1))],
        out_specs=[
            pl.BlockSpec((CHUNK, SSD_INNER), lambda b, s: (_fwd_chunk(b, s), 0)),
            pl.BlockSpec((CHUNK, SSD_INNER), lambda b, s: (_bwd_chunk(b, s), 0)),
        ],
        out_shape=[jax.ShapeDtypeStruct((T_ALL, SSD_INNER), BF16)] * 2,
        scratch_shapes=[pltpu.VMEM((2, SSD_STATE, SSD_INNER), F32)],
        compiler_params=_cparams(("arbitrary", "arbitrary")),
        name="ssd",
    )(xbc, xbc, xbc, dt_all, xbc, xbc, xbc, dt_all, bias_row, alog_row, bias_col, alog_col)


def _ret_kernel(lg_ref, q_f, k_f, v_f, q_b, k_b, v_b, of_ref, ob_ref, s_scr, tab_scr):
    s = pl.program_id(1)

    @pl.when(s == 0)
    def _():
        s_scr[...] = jnp.zeros_like(s_scr)
        ii = lax.broadcasted_iota(jnp.int32, (CHUNK, CHUNK), 0)
        jj = lax.broadcasted_iota(jnp.int32, (CHUNK, CHUNK), 1)
        for d in range(2):
            if d == 0:
                dist = (ii - jj).astype(F32)
                row_pow = (ii + 1).astype(F32)
                key_pow = (CHUNK - 1 - ii).astype(F32)
            else:
                dist = (jj - ii).astype(F32)
                row_pow = (CHUNK - ii).astype(F32)
                key_pow = ii.astype(F32)
            for h in range(RET_HEADS):
                lg = lg_ref[d, h]
                tab_scr[d, h, 0] = jnp.where(dist >= 0.0, jnp.exp(jnp.maximum(dist, 0.0) * lg), 0.0)
                tab_scr[d, h, 1] = jnp.exp(row_pow * lg)
                tab_scr[d, h, 2] = jnp.exp(key_pow * lg)

    dirs = ((q_f, k_f, v_f, of_ref), (q_b, k_b, v_b, ob_ref))
    heads = [(d, h) for d in range(2) for h in range(RET_HEADS)]
    lanes = lambda h: slice(h * HEAD_DIM, (h + 1) * HEAD_DIM)
    raw = {}
    for d, h in heads:
        q_ref, k_ref, v_ref, _ = dirs[d]
        raw[d, h] = _dot_nt(q_ref[:, lanes(h)], k_ref[:, lanes(h)])
    kv = {}
    for d, h in heads:
        _, k_ref, v_ref, _ = dirs[d]
        k_dec = (k_ref[:, lanes(h)].astype(F32) * tab_scr[d, h, 2]).astype(BF16)
        kv[d, h] = _dot_tn(k_dec, v_ref[:, lanes(h)])
    outs = {}
    for d, h in heads:
        q_ref, _, v_ref, _ = dirs[d]
        scores = raw[d, h] * tab_scr[d, h, 0]
        q_cross = q_ref[:, lanes(h)].astype(F32) * tab_scr[d, h, 1]
        state = s_scr[d, h * HEAD_DIM:(h + 1) * HEAD_DIM, :]
        lhs = jnp.concatenate([scores.astype(BF16), q_cross.astype(BF16)], axis=1)
        rhs = jnp.concatenate([v_ref[:, lanes(h)], state.astype(BF16)], axis=0)
        outs[d, h] = _dot(lhs, rhs)
    for d in range(2):
        states = []
        for h in range(RET_HEADS):
            chunk_decay = jnp.exp(jnp.full((1, HEAD_DIM), float(CHUNK), F32) * lg_ref[d, h])
            states.append(chunk_decay * s_scr[d, h * HEAD_DIM:(h + 1) * HEAD_DIM, :] + kv[d, h])
        dirs[d][3][...] = jnp.concatenate([outs[d, h] for h in range(RET_HEADS)], axis=1).astype(BF16)
        s_scr[d] = jnp.concatenate(states, axis=0)


def _ret_call(p_all, log_decay):
    def specs(chunk_fn):
        return [
            _p_spec(CHUNK, RET_WIDTH, COL_RQ, chunk_fn),
            _p_spec(CHUNK, RET_WIDTH, COL_RK, chunk_fn),
            _p_spec(CHUNK, RET_WIDTH, COL_RV, chunk_fn),
        ]

    return pl.pallas_call(
        _ret_kernel,
        grid=(BATCH, NSTEP),
        in_specs=[pl.BlockSpec(memory_space=pltpu.SMEM)] + specs(_fwd_chunk) + specs(_bwd_chunk),
        out_specs=[
            pl.BlockSpec((CHUNK, RET_WIDTH), lambda b, s: (_fwd_chunk(b, s), 0)),
            pl.BlockSpec((CHUNK, RET_WIDTH), lambda b, s: (_bwd_chunk(b, s), 0)),
        ],
        out_shape=[jax.ShapeDtypeStruct((T_ALL, RET_WIDTH), BF16)] * 2,
        scratch_shapes=[pltpu.VMEM((2, RET_HEADS * HEAD_DIM, HEAD_DIM), F32),
                        pltpu.VMEM((2, RET_HEADS, 3, CHUNK, CHUNK), F32)],
        compiler_params=_cparams(("arbitrary", "arbitrary")),
        name="retention",
    )(log_decay, p_all, p_all, p_all, p_all, p_all, p_all)


def _merge_kernel(attn_ref, yf_ref, yb_ref, xs_ref, z0_ref, z1_ref, of_ref, ob_ref, rg_ref,
                  ga0_ref, ga1_ref, gs0_ref, gs1_ref, gr0_ref, gr1_ref, dskip_ref, ssd_g_ref, ret_g_ref,
                  wa_ref, ws_ref, wr_ref, o_ref):
    halves = lambda a, b: jnp.concatenate([a[...], b[...]], axis=1).astype(F32)
    z = halves(z0_ref, z1_ref)
    y = (yf_ref[...].astype(F32) + yb_ref[...].astype(F32)
         + dskip_ref[...] * xs_ref[...].astype(F32))
    y = y * _silu(z)
    ssd_o = y * lax.rsqrt(jnp.mean(y * y, axis=-1, keepdims=True) + LN_EPS) * ssd_g_ref[...]

    o = of_ref[...].astype(F32) + ob_ref[...].astype(F32)
    normed = jnp.concatenate(
        [_ln_rows(o[:, h * HEAD_DIM:(h + 1) * HEAD_DIM]) for h in range(RET_HEADS)], axis=1)
    ret_o = normed * ret_g_ref[...] * _silu(rg_ref[...].astype(F32))

    merged = (_sigmoid(halves(ga0_ref, ga1_ref)) * _dot(attn_ref[...], wa_ref[...])
              + _sigmoid(halves(gs0_ref, gs1_ref)) * _dot(ssd_o.astype(BF16), ws_ref[...])
              + _sigmoid(halves(gr0_ref, gr1_ref)) * _dot(ret_o.astype(BF16), wr_ref[...]))
    o_ref[...] = merged.astype(BF16)


def _merge_call(l, n_rows, attn, yf, yb, xbc, p_all, of, ob, dskip, ssd_g, ret_g, wa, ws, wr):
    tm = TM_MERGE
    zw = SSD_INNER // 2
    w1024 = lambda c: pl.BlockSpec((tm, 1024), lambda i: (i, c))
    row = lambda i: i
    zspec = lambda c: _p_spec(tm, zw, COL_Z + c * zw, row)
    gate = lambda c: _p_spec(tm, TN_IN, COL_GATES + c * TN_IN, row)
    vec = pl.BlockSpec((1, 1024), lambda i: (0, 0))
    wspec = pl.BlockSpec((None, 1024, D_MODEL), lambda i: (l, 0, 0), pipeline_mode=pl.Buffered(1))
    return pl.pallas_call(
        _merge_kernel,
        grid=(n_rows // tm,),
        in_specs=[
            w1024(0), w1024(0), w1024(0), w1024(0), zspec(0), zspec(1),
            w1024(0), w1024(0), _p_spec(tm, RET_WIDTH, COL_RG, row),
            gate(0), gate(1), gate(2), gate(3), gate(4), gate(5), vec, vec, vec, wspec, wspec, wspec,
        ],
        out_specs=pl.BlockSpec((tm, D_MODEL), lambda i: (i, 0)),
        out_shape=jax.ShapeDtypeStruct((n_rows, D_MODEL), BF16),
        compiler_params=_cparams(("arbitrary",)),
        name="merge",
    )(attn, yf, yb, xbc, p_all, p_all, of, ob, p_all, p_all, p_all, p_all, p_all, p_all, p_all,
      dskip, ssd_g, ret_g, wa, ws, wr)


def _deepnorm(x, y, gate, g, b):
    return _ln_rows(DEEPNORM_ALPHA * x + gate * y) * g + b


def _outproj_kernel(m_ref, x_ref, w_ref, gate_ref, g_ref, b_ref, *rest):
    o_ref = rest[-1]
    for r in range(0, TM_OUT, RC_OUT):
        rows = slice(r, r + RC_OUT)
        mix = _dot(m_ref[rows, :], w_ref[...])
        o_ref[rows, :] = _deepnorm(x_ref[rows, :], mix, gate_ref[...], g_ref[...], b_ref[...])


def _outproj_call(l, n_rows, n_tiles, merged, x_src, tile0, w_out, mod, ln_g, ln_b, prev=None):
    tm = TM_OUT
    vec = pl.BlockSpec((1, D_MODEL), lambda i: (0, 0))
    in_specs = [
        pl.BlockSpec((tm, D_MODEL), lambda i: (i + tile0, 0)),
        pl.BlockSpec((tm, D_MODEL), lambda i: (i, 0)),
        pl.BlockSpec((None, D_MODEL, D_MODEL), lambda i: (l, 0, 0), pipeline_mode=pl.Buffered(1)),
        pl.BlockSpec((None, 1, D_MODEL), lambda i: (_batch_of_tile(i + tile0, tm) * 6 + 2, 0, 0)),
        vec, vec,
    ]
    args = [merged, x_src, w_out, mod, ln_g, ln_b]
    aliases = {}
    if prev is not None:
        aliases = {len(args): 0}
        in_specs.append(pl.BlockSpec(memory_space=pl.ANY))
        args.append(prev)
    return pl.pallas_call(
        _outproj_kernel,
        grid=(n_tiles,),
        in_specs=in_specs,
        out_specs=pl.BlockSpec((tm, D_MODEL), lambda i: (i + tile0, 0)),
        out_shape=jax.ShapeDtypeStruct((n_rows, D_MODEL), F32),
        input_output_aliases=aliases,
        compiler_params=_cparams(("arbitrary",)),
        name="outproj",
    )(*args)


def _mlp_kernel(x_ref, shift_ref, scale_ref, gate_ref, wup_ref, wdn_ref, g_ref, b_ref,
                o_ref, h_scr, acc_scr):
    j = pl.program_id(1)
    last = pl.num_programs(1) - 1
    chunks = [slice(r, r + RC_MLP) for r in range(0, TM_MLP, RC_MLP)]

    def up_down(rows):
        u = jnp.maximum(_dot(h_scr[rows, :], wup_ref[...]), 0.0)
        return _dot((u * u).astype(BF16), wdn_ref[...])

    @pl.when(j == 0)
    def _():
        for rows in chunks:
            h = _ln_rows(x_ref[rows, :]) * (1.0 + scale_ref[...]) + shift_ref[...]
            h_scr[rows, :] = h.astype(BF16)
        for rows in chunks:
            acc_scr[rows, :] = up_down(rows)

    @pl.when((j > 0) & (j < last))
    def _():
        acc_scr[...] += up_down(slice(None))

    @pl.when(j == last)
    def _():
        for rows in chunks:
            y = acc_scr[rows, :] + up_down(rows)
            o_ref[rows, :] = _deepnorm(x_ref[rows, :], y, gate_ref[...], g_ref[...], b_ref[...])


def _mlp_call(l, n_rows, x1, mod, w_up, w_down, ln_g, ln_b):
    tm, tf = TM_MLP, TF_MLP

    def mod_map(k):
        return lambda i, j: (_batch_of_tile(i, tm) * 6 + k, 0, 0)

    vec = pl.BlockSpec((1, D_MODEL), lambda i, j: (0, 0))
    return pl.pallas_call(
        _mlp_kernel,
        grid=(n_rows // tm, D_FF // tf),
        in_specs=[
            pl.BlockSpec((tm, D_MODEL), lambda i, j: (i, 0)),
            pl.BlockSpec((None, 1, D_MODEL), mod_map(3)),
            pl.BlockSpec((None, 1, D_MODEL), mod_map(4)),
            pl.BlockSpec((None, 1, D_MODEL), mod_map(5)),
            pl.BlockSpec((None, None, D_MODEL, tf), lambda i, j: (l, j, 0, 0)),
            pl.BlockSpec((None, tf, D_MODEL), lambda i, j: (l, j, 0)),
            vec, vec,
        ],
        out_specs=pl.BlockSpec((tm, D_MODEL), lambda i, j: (i, 0)),
        out_shape=jax.ShapeDtypeStruct((n_rows, D_MODEL), F32),
        scratch_shapes=[pltpu.VMEM((tm, D_MODEL), BF16), pltpu.VMEM((tm, D_MODEL), F32)],
        compiler_params=_cparams(("arbitrary", "arbitrary")),
        name="mlp",
    )(x1, mod, mod, mod, w_up, w_down, ln_g, ln_b)


def _rope_tables():
    f32 = np.float32
    rows = SEQ // GRID_W
    row = np.repeat(np.arange(rows), GRID_W).astype(f32)
    col = (np.arange(rows * GRID_W) % GRID_W).astype(f32)
    n_freq = HEAD_DIM // 4
    inv = (f32(ROPE_BASE) ** (-np.arange(n_freq, dtype=f32) / f32(n_freq))).astype(f32)
    ang = np.concatenate([row[:, None] * inv, col[:, None] * inv], axis=-1).astype(f32)
    cos, sin = np.cos(ang).astype(f32), np.sin(ang).astype(f32)
    cos2 = np.concatenate([cos, cos], axis=-1)
    sin2 = np.concatenate([-sin, sin], axis=-1)
    cos2 = np.concatenate([cos2, np.ones((TM_IN, HEAD_DIM), f32)], axis=0)
    sin2 = np.concatenate([sin2, np.zeros((TM_IN, HEAD_DIM), f32)], axis=0)
    return jnp.asarray(cos2), jnp.asarray(sin2)


def _pad_heads(v):
    flat = jnp.pad(v.reshape(-1).astype(F32), (0, DT_PAD - DT_COLS))
    return flat.reshape(1, DT_PAD), flat.reshape(DT_PAD, 1)


def kernel(x, c, ctx, c_ctx, ada_w, ada_b, w_in, attn_sink, ssd_conv_w, ssd_conv_b, ssd_a_log,
           ssd_dt_bias, ssd_d, ssd_norm_g, ret_log_decay, ret_norm_g, w_branch_attn, w_branch_ssd,
           w_branch_ret, w_out, ln1_g, ln1_b, w_mlp_up, w_mlp_down, ln2_g, ln2_b):
    lat_tiles_in, lat_tiles_out = T_LAT // TM_IN, T_LAT // TM_OUT
    sources = [(x.reshape(T_LAT, D_MODEL), 0, 0), (ctx.reshape(T_CTX, D_MODEL), lat_tiles_in, lat_tiles_out)]
    cond = jnp.concatenate([c, c_ctx[None, :], jnp.zeros((8 - BATCH - 1, D_MODEL), F32)], axis=0)
    mod_all = _ada_call(cond, ada_w, ada_b).reshape(DEPTH, 8 * 6, 1, D_MODEL)
    cos2, sin2 = _rope_tables()
    wb_attn, wb_ssd, wb_ret = (_cast_stacked(w) for w in (w_branch_attn, w_branch_ssd, w_branch_ret))
    w_out_b, w_down_b = _cast_stacked(w_out), _cast_stacked(w_mlp_down)
    w_up_b = _cast_col_tiled(w_mlp_up, TF_MLP)
    w_in_t, w_dt = _cast_w_in(w_in)

    for l in range(DEPTH):
        update_ctx = l < DEPTH - 1
        n_rows = T_ALL if update_ctx else T_LAT
        mod = mod_all[l]
        proj = None
        for x_src, tile_in, _ in sources:
            proj = _inproj_call(l, x_src, tile_in, mod, w_in_t, w_dt, cos2, sin2, not update_ctx, proj)
        p_all, dt_all = proj
        xbc = _conv_call(p_all, ssd_conv_w[l], ssd_conv_b[l])

        attn = _attn_calls(p_all, attn_sink[l].astype(F32), update_ctx)

        bias_row, bias_col = _pad_heads(ssd_dt_bias[l])
        alog_row, alog_col = _pad_heads(ssd_a_log[l])
        yf, yb = _ssd_call(xbc, dt_all, bias_row, alog_row, bias_col, alog_col)
        of, ob = _ret_call(p_all, ret_log_decay[l].astype(F32))

        dskip = jnp.repeat(ssd_d[l].astype(F32), SSD_P).reshape(1, SSD_INNER)
        merged = _merge_call(
            l, n_rows, attn, yf, yb, xbc, p_all, of, ob, dskip,
            ssd_norm_g[l].reshape(1, SSD_INNER), ret_norm_g[l].reshape(1, RET_WIDTH), wb_attn, wb_ssd, wb_ret)
        x1 = None
        for x_src, _, tile_out in sources:
            n_tiles = min(x_src.shape[0], n_rows - tile_out * TM_OUT) // TM_OUT
            x1 = _outproj_call(l, n_rows, n_tiles, merged, x_src, tile_out, w_out_b, mod,
                               ln1_g[l].reshape(1, D_MODEL), ln1_b[l].reshape(1, D_MODEL), x1)
        x_all = _mlp_call(l, n_rows, x1, mod, w_up_b, w_down_b,
                          ln2_g[l].reshape(1, D_MODEL), ln2_b[l].reshape(1, D_MODEL))
        sources = [(x_all, 0, 0)]
    return x_all.reshape(BATCH, SEQ, D_MODEL)
```

```python
import functools

import jax
import jax.numpy as jnp
from jax import lax
from jax.experimental import pallas as pl
from jax.experimental.pallas import tpu as pltpu
import numpy as np

F32 = jnp.float32
BF16 = jnp.bfloat16

D_MODEL = 2048
BATCH = 4
SEQ = 4096
DEPTH = 2
GRID_W = 64
CTX_LEN = 256
BLOCK = 128
HEAD_DIM = 128
ATTN_HQ = 8
ATTN_HKV = 2
ATTN_GROUP = ATTN_HQ // ATTN_HKV
ATTN_WIDTH = ATTN_HQ * HEAD_DIM
SSD_HEADS = 16
SSD_P = 64
SSD_INNER = SSD_HEADS * SSD_P
SSD_GROUPS = 2
SSD_STATE = 128
SSD_CONV = 5
SSD_CONV_CH = SSD_INNER + 2 * SSD_GROUPS * SSD_STATE
CHUNK = 128
RET_HEADS = 8
RET_WIDTH = RET_HEADS * HEAD_DIM
D_FF = 4 * D_MODEL
ROPE_BASE = 10000.0
DEEPNORM_ALPHA = (2 * DEPTH) ** 0.25
LN_EPS = 1e-6
NEG_INF = -1e30
Q_SCALE = HEAD_DIM ** -0.5
LOG2E = 1.4426950408889634

T_LAT = BATCH * SEQ
T_CTX = BATCH * CTX_LEN
T_ALL = T_LAT + T_CTX
NCHUNK_LAT = SEQ // CHUNK
NCHUNK_CTX = CTX_LEN // CHUNK
NSTEP = NCHUNK_LAT + NCHUNK_CTX

COL_AQ = 0
COL_AK = 1024
COL_AV = 1280
COL_Z = 1536
COL_XBC = 2560
COL_DT = 4096
COL_RQ = 4096
COL_RK = 5120
COL_RV = 6144
COL_RG = 7168
COL_GATES = 8192
P_COLS = 14336
DT_COLS = 2 * SSD_HEADS
DT_PAD = 128

TM_IN = 1024
RC_IN = 256
TN_IN = 1024
J_AQ, J_AKAV, J_RQ, J_RK = COL_AQ // TN_IN, COL_AK // TN_IN, COL_RQ // TN_IN, COL_RK // TN_IN
J_SPLIT = COL_DT // TN_IN
J_RG, J_GATES = COL_RG // TN_IN, COL_GATES // TN_IN
TM_MERGE = 512
TM_OUT = 1024
RC_OUT = 256
TM_MLP = 512
TF_MLP = 1024
RC_MLP = 256
TM_CONV = 1024
HALO = 16

VMEM_LIMIT = 56 * 1024 * 1024


def _cparams(sem):
    return pltpu.CompilerParams(dimension_semantics=sem, vmem_limit_bytes=VMEM_LIMIT)


def _sigmoid(x):
    return 0.5 * jnp.tanh(0.5 * x) + 0.5


def _silu(x):
    return x * _sigmoid(x)


def _softplus(x):
    return jnp.maximum(x, 0.0) + jnp.log1p(jnp.exp(-jnp.abs(x)))


def _ln_rows(x):
    mu = jnp.mean(x, axis=-1, keepdims=True)
    xc = x - mu
    var = jnp.mean(xc * xc, axis=-1, keepdims=True)
    return xc * lax.rsqrt(var + LN_EPS)


def _dot(a, b):
    return jnp.dot(a, b, preferred_element_type=F32)


def _dot_nt(a, b):
    return lax.dot_general(a, b, (((1,), (1,)), ((), ())), preferred_element_type=F32)


def _dot_tn(a, b):
    return lax.dot_general(a, b, (((0,), (0,)), ((), ())), preferred_element_type=F32)


def _split3(x):
    x1 = x.astype(BF16)
    r1 = x - x1.astype(F32)
    x2 = r1.astype(BF16)
    r2 = r1 - x2.astype(F32)
    return x1, x2, r2.astype(BF16)


def _batch_of_tile(i, tm):
    return jnp.where(i < T_LAT // tm, i // (SEQ // tm), BATCH)


CAST_BLOCK_BYTES = 8 * 1024 * 1024


def _cast_kernel(w_ref, o_ref):
    o_ref[...] = w_ref[...].astype(BF16)


def _cast_stacked(w):
    depth, rows, cols = w.shape
    rb = min(rows, CAST_BLOCK_BYTES // (cols * 4))
    spec = pl.BlockSpec((None, rb, cols), lambda l, i: (l, i, 0))
    return pl.pallas_call(
        _cast_kernel,
        grid=(depth, rows // rb),
        in_specs=[spec],
        out_specs=spec,
        out_shape=jax.ShapeDtypeStruct(w.shape, BF16),
        compiler_params=_cparams(("arbitrary", "arbitrary")),
        name="cast",
    )(w)


def _cast_w_in(w_in):
    w_t = jnp.swapaxes(w_in, 1, 2)
    depth, cols, k = w_t.shape

    def src_map(l, j):
        return (l, (j * (TN_IN // DT_COLS) + jnp.where(j >= J_SPLIT, 1, 0)) * DT_COLS, 0)

    w_main = pl.pallas_call(
        _cast_kernel,
        grid=(depth, P_COLS // TN_IN),
        in_specs=[pl.BlockSpec((pl.Squeezed(), pl.Element(TN_IN), pl.Element(k)), src_map)],
        out_specs=pl.BlockSpec((None, TN_IN, k), lambda l, j: (l, j, 0)),
        out_shape=jax.ShapeDtypeStruct((depth, P_COLS, k), BF16),
        compiler_params=_cparams(("arbitrary", "arbitrary")),
        name="cast_w_in",
    )(w_t)
    w_dt = jnp.pad(w_t[:, COL_DT:COL_DT + DT_COLS, :], ((0, 0), (0, DT_PAD - DT_COLS), (0, 0))).astype(BF16)
    return w_main, w_dt


TN_ADA = 2048


def _ada_kernel(cond_ref, w0_ref, w1_ref, b_ref, o_ref):
    s = _silu(cond_ref[...]).astype(BF16)
    kh = D_MODEL // 2
    o_ref[...] = (_dot(s[:, :kh], w0_ref[...].astype(BF16))
                  + _dot(s[:, kh:], w1_ref[...].astype(BF16)) + b_ref[...])


def _ada_call(cond, ada_w, ada_b):
    n = ada_w.shape[-1]
    return pl.pallas_call(
        _ada_kernel,
        grid=(DEPTH, n // TN_ADA),
        in_specs=[
            pl.BlockSpec((8, D_MODEL), lambda l, j: (0, 0)),
            pl.BlockSpec((None, D_MODEL // 2, TN_ADA), lambda l, j: (l, 0, j)),
            pl.BlockSpec((None, D_MODEL // 2, TN_ADA), lambda l, j: (l, 1, j)),
            pl.BlockSpec((None, 1, TN_ADA), lambda l, j: (l, 0, j)),
        ],
        out_specs=pl.BlockSpec((None, 8, TN_ADA), lambda l, j: (l, 0, j)),
        out_shape=jax.ShapeDtypeStruct((DEPTH, 8, n), F32),
        compiler_params=_cparams(("arbitrary", "arbitrary")),
        name="ada",
    )(cond, ada_w, ada_w, ada_b.reshape(DEPTH, 1, n))


def _rope_store(acc, cos, sin, o_ref, n_heads, scale, rows=slice(None)):
    for h in range(n_heads):
        xs = acc[:, h * HEAD_DIM:(h + 1) * HEAD_DIM]
        rot = pltpu.roll(xs, HEAD_DIM // 2, axis=1)
        o_ref[rows, h * HEAD_DIM:(h + 1) * HEAD_DIM] = ((xs * cos + rot * sin) * scale).astype(BF16)


def _inproj_kernel(skip_ctx_cols, tile0, has_prev, x_ref, shift_ref, scale_ref, w_ref, wdt_ref,
                   cos_ref, sin_ref, *rest):
    p_ref, dt_ref, h_scr = rest[2:] if has_prev else rest
    i = pl.program_id(0) + tile0
    j = pl.program_id(1)
    chunks = [slice(r, r + RC_IN) for r in range(0, TM_IN, RC_IN)]
    heads_per_tile = TN_IN // HEAD_DIM
    if skip_ctx_cols:
        unused = (j == J_AQ) | (j == J_RG) | (j >= J_GATES)
        active = jnp.logical_not((i == T_ALL // TM_IN - 1) & unused)
    else:
        active = True

    def prologue(rows):
        h = _ln_rows(x_ref[rows, :]) * (1.0 + scale_ref[...]) + shift_ref[...]
        h_scr[rows, :] = h.astype(BF16)
        dt_ref[rows, :] = _dot_nt(h_scr[rows, :], wdt_ref[...])

    def rotated(scale):
        acc = _dot_nt(h_scr[...], w_ref[...])
        _rope_store(acc, cos_ref[...], sin_ref[...], p_ref, heads_per_tile, scale)

    def plain():
        p_ref[...] = _dot_nt(h_scr[...], w_ref[...]).astype(BF16)

    @pl.when((j == J_AQ) & active)
    def _():
        for rows in chunks:
            prologue(rows)
            acc = _dot_nt(h_scr[rows, :], w_ref[...])
            _rope_store(acc, cos_ref[rows, :], sin_ref[rows, :], p_ref, heads_per_tile,
                        Q_SCALE * LOG2E, rows)

    if skip_ctx_cols:
        @pl.when((j == J_AQ) & jnp.logical_not(active))
        def _():
            for rows in chunks:
                prologue(rows)

    @pl.when(j == J_AKAV)
    def _():
        acc = _dot_nt(h_scr[...], w_ref[...])
        _rope_store(acc, cos_ref[...], sin_ref[...], p_ref, ATTN_HKV, 1.0)
        p_ref[:, ATTN_HKV * HEAD_DIM:] = acc[:, ATTN_HKV * HEAD_DIM:].astype(BF16)

    pl.when(j == J_RQ)(lambda: rotated(Q_SCALE))
    pl.when(j == J_RK)(lambda: rotated(1.0))
    pl.when((j > J_AKAV) & (j != J_RQ) & (j != J_RK) & active)(plain)


def _inproj_call(l, x_src, tile0, mod, w_t, w_dt, cos2, sin2, skip_ctx_cols, prev=None):
    lat_tiles = T_LAT // TM_IN

    def mod_map(k):
        return lambda i, j: (_batch_of_tile(i + tile0, TM_IN) * 6 + k, 0, 0)

    def rope_map(i, j):
        g = i + tile0
        return (jnp.where(g < lat_tiles, g % (SEQ // TM_IN), SEQ // TM_IN), 0)

    in_specs = [
        pl.BlockSpec((TM_IN, D_MODEL), lambda i, j: (i, 0)),
        pl.BlockSpec((None, 1, D_MODEL), mod_map(0)),
        pl.BlockSpec((None, 1, D_MODEL), mod_map(1)),
        pl.BlockSpec((None, TN_IN, D_MODEL), lambda i, j: (l, j, 0)),
        pl.BlockSpec((None, DT_PAD, D_MODEL), lambda i, j: (l, 0, 0)),
        pl.BlockSpec((TM_IN, HEAD_DIM), rope_map),
        pl.BlockSpec((TM_IN, HEAD_DIM), rope_map),
    ]
    args = [x_src, mod, mod, w_t, w_dt, cos2, sin2]
    aliases = {}
    if prev is not None:
        aliases = {len(args): 0, len(args) + 1: 1}
        in_specs += [pl.BlockSpec(memory_space=pl.ANY)] * 2
        args += list(prev)
    return pl.pallas_call(
        functools.partial(_inproj_kernel, skip_ctx_cols, tile0, prev is not None),
        grid=(x_src.shape[0] // TM_IN, P_COLS // TN_IN),
        in_specs=in_specs,
        out_specs=[
            pl.BlockSpec((TM_IN, TN_IN), lambda i, j: (i + tile0, j)),
            pl.BlockSpec((TM_IN, DT_PAD), lambda i, j: (i + tile0, 0)),
        ],
        out_shape=[
            jax.ShapeDtypeStruct((T_ALL, P_COLS), BF16),
            jax.ShapeDtypeStruct((T_ALL, DT_PAD), F32),
        ],
        scratch_shapes=[pltpu.VMEM((TM_IN, D_MODEL), BF16)],
        input_output_aliases=aliases,
        compiler_params=_cparams(("arbitrary", "arbitrary")),
        name="inproj",
    )(*args)


TN_CONV = 512


def _conv_taps(prev, x, nxt, w_ref, b_ref, pos, slen):
    xe = jnp.concatenate([prev, x, nxt], axis=0)
    w = w_ref[...]
    acc = jnp.zeros((TM_CONV, TN_CONV), F32) + b_ref[...]
    half = SSD_CONV // 2
    for k in range(SSD_CONV):
        off = HALO - half + k
        xk = xe[off:off + TM_CONV, :]
        if pos is not None and k < half:
            xk = jnp.where(pos + (k - half) >= 0, xk, 0.0)
        elif pos is not None and k > half:
            xk = jnp.where(pos + (k - half) < slen, xk, 0.0)
        acc = acc + xk * w[k:k + 1, :]
    return _silu(acc).astype(BF16)


def _conv_kernel(x_ref, prev_ref, next_ref, w_ref, b_ref, o_ref):
    i = pl.program_id(0)
    tiles_per_seq = SEQ // TM_CONV
    x = x_ref[...].astype(F32)
    prev = prev_ref[...].astype(F32)
    nxt = next_ref[...].astype(F32)

    @pl.when(i < T_LAT // TM_CONV)
    def _():
        t = i & (tiles_per_seq - 1)
        o_ref[...] = _conv_taps(jnp.where(t == 0, 0.0, prev), x,
                                jnp.where(t == tiles_per_seq - 1, 0.0, nxt), w_ref, b_ref, None, None)

    @pl.when(i >= T_LAT // TM_CONV)
    def _():
        row = lax.broadcasted_iota(jnp.int32, (TM_CONV, TN_CONV), 0)
        o_ref[...] = _conv_taps(prev, x, nxt, w_ref, b_ref, row & (CTX_LEN - 1), CTX_LEN)


def _conv_call(p_all, conv_w, conv_b):
    cb0 = COL_XBC // TN_CONV
    hpt = TM_CONV // HALO
    last_halo = T_ALL // HALO - 1
    return pl.pallas_call(
        _conv_kernel,
        grid=(T_ALL // TM_CONV, SSD_CONV_CH // TN_CONV),
        in_specs=[
            pl.BlockSpec((TM_CONV, TN_CONV), lambda i, j: (i, cb0 + j)),
            pl.BlockSpec((HALO, TN_CONV), lambda i, j: (jnp.maximum(i * hpt - 1, 0), cb0 + j)),
            pl.BlockSpec((HALO, TN_CONV), lambda i, j: (jnp.minimum((i + 1) * hpt, last_halo), cb0 + j)),
            pl.BlockSpec((SSD_CONV, TN_CONV), lambda i, j: (0, j)),
            pl.BlockSpec((1, TN_CONV), lambda i, j: (0, j)),
        ],
        out_specs=pl.BlockSpec((TM_CONV, TN_CONV), lambda i, j: (i, j)),
        out_shape=jax.ShapeDtypeStruct((T_ALL, SSD_CONV_CH), BF16),
        compiler_params=_cparams(("arbitrary", "arbitrary")),
        name="conv",
    )(p_all, p_all, p_all, conv_w, conv_b.reshape(1, SSD_CONV_CH))


def _softmax_pv(s, vals, sink_col):
    m = jnp.maximum(jnp.max(s, axis=-1, keepdims=True), sink_col)
    p = jnp.exp2(s - m)
    l = jnp.sum(p, axis=-1, keepdims=True) + jnp.exp2(sink_col - m)
    return _dot(p.astype(BF16), vals) / l


def _group_queries(q_ref, hkv):
    h0 = hkv * ATTN_GROUP
    return jnp.concatenate(
        [q_ref[:, (h0 + g) * HEAD_DIM:(h0 + g + 1) * HEAD_DIM] for g in range(ATTN_GROUP)], axis=0)


def _sink_column(sink_ref, hkv, nq):
    return jnp.concatenate(
        [jnp.full((nq, 1), sink_ref[hkv * ATTN_GROUP + g] * LOG2E, F32) for g in range(ATTN_GROUP)], axis=0)


def _attn_kernel(sink_ref, q_ref, kp_ref, kc_ref, kn_ref, vp_ref, vc_ref, vn_ref, kx_ref, vx_ref,
                 o_ref):
    n = pl.program_id(1)
    rows = ATTN_GROUP * BLOCK
    qi = lax.broadcasted_iota(jnp.int32, (rows, BLOCK), 0) & (BLOCK - 1)
    kj = lax.broadcasted_iota(jnp.int32, (rows, BLOCK), 1)
    prev_ok = (kj >= qi) & (n > 0)
    next_ok = (kj <= qi) & (n < SEQ // BLOCK - 1)

    def mask_fn(s):
        return jnp.concatenate([
            jnp.where(prev_ok, s[:, :BLOCK], NEG_INF), s[:, BLOCK:2 * BLOCK],
            jnp.where(next_ok, s[:, 2 * BLOCK:3 * BLOCK], NEG_INF), s[:, 3 * BLOCK:]], axis=1)

    kv_cols = lambda hkv: slice(hkv * HEAD_DIM, (hkv + 1) * HEAD_DIM)
    scores = []
    for hkv in range(ATTN_HKV):
        c = kv_cols(hkv)
        keys = jnp.concatenate([kp_ref[:, c], kc_ref[:, c], kn_ref[:, c], kx_ref[:, c]], axis=0)
        scores.append(mask_fn(_dot_nt(_group_queries(q_ref, hkv), keys)))
    outs = []
    for hkv in range(ATTN_HKV):
        c = kv_cols(hkv)
        vals = jnp.concatenate([vp_ref[:, c], vc_ref[:, c], vn_ref[:, c], vx_ref[:, c]], axis=0)
        o = _softmax_pv(scores[hkv], vals, _sink_column(sink_ref, hkv, BLOCK))
        outs += [o[g * BLOCK:(g + 1) * BLOCK, :] for g in range(ATTN_GROUP)]
    o_ref[...] = jnp.concatenate(outs, axis=1).astype(BF16)


def _ctx_attn_kernel(sink_ref, q_ref, kx_ref, vx_ref, prev_ref, o_ref):
    del prev_ref
    kv_cols = lambda hkv: slice(hkv * HEAD_DIM, (hkv + 1) * HEAD_DIM)
    scores = [_dot_nt(_group_queries(q_ref, hkv), kx_ref[:, kv_cols(hkv)]) for hkv in range(ATTN_HKV)]
    outs = []
    for hkv in range(ATTN_HKV):
        o = _softmax_pv(scores[hkv], vx_ref[:, kv_cols(hkv)], _sink_column(sink_ref, hkv, CTX_LEN))
        outs += [o[g * CTX_LEN:(g + 1) * CTX_LEN, :] for g in range(ATTN_GROUP)]
    o_ref[...] = jnp.concatenate(outs, axis=1).astype(BF16)


def _attn_calls(p_all, sink, update_ctx):
    nb = SEQ // BLOCK
    kvw = ATTN_HKV * HEAD_DIM
    ck = COL_AK // kvw
    cv = COL_AV // kvw
    ctx0 = T_LAT // CTX_LEN
    smem = pl.BlockSpec(memory_space=pltpu.SMEM)

    def kv_spec(col, dn):
        return pl.BlockSpec((BLOCK, kvw), lambda b, n: (b * nb + jnp.clip(n + dn, 0, nb - 1), col))

    def ctx_spec(col):
        return pl.BlockSpec((CTX_LEN, kvw), lambda b, n: (ctx0 + b, col))

    attn = pl.pallas_call(
        _attn_kernel,
        grid=(BATCH, nb),
        in_specs=[
            smem,
            pl.BlockSpec((BLOCK, ATTN_WIDTH), lambda b, n: (b * nb + n, 0)),
            kv_spec(ck, -1), kv_spec(ck, 0), kv_spec(ck, 1),
            kv_spec(cv, -1), kv_spec(cv, 0), kv_spec(cv, 1),
            ctx_spec(ck), ctx_spec(cv),
        ],
        out_specs=pl.BlockSpec((BLOCK, ATTN_WIDTH), lambda b, n: (b * nb + n, 0)),
        out_shape=jax.ShapeDtypeStruct((T_ALL, ATTN_WIDTH), BF16),
        compiler_params=_cparams(("arbitrary", "arbitrary")),
        name="attn",
    )(sink, p_all, p_all, p_all, p_all, p_all, p_all, p_all, p_all, p_all)
    if not update_ctx:
        return attn
    return pl.pallas_call(
        _ctx_attn_kernel,
        grid=(BATCH,),
        in_specs=[
            smem,
            pl.BlockSpec((CTX_LEN, ATTN_WIDTH), lambda b: (ctx0 + b, 0)),
            pl.BlockSpec((CTX_LEN, kvw), lambda b: (ctx0 + b, ck)),
            pl.BlockSpec((CTX_LEN, kvw), lambda b: (ctx0 + b, cv)),
            pl.BlockSpec(memory_space=pl.ANY),
        ],
        out_specs=pl.BlockSpec((CTX_LEN, ATTN_WIDTH), lambda b: (ctx0 + b, 0)),
        out_shape=jax.ShapeDtypeStruct((T_ALL, ATTN_WIDTH), BF16),
        input_output_aliases={4: 0},
        compiler_params=_cparams(("arbitrary",)),
        name="ctx_attn",
    )(sink, p_all, p_all, p_all, attn)


def _fwd_chunk(b, s):
    ctx = T_LAT // CHUNK + b * NCHUNK_CTX + s
    lat = b * NCHUNK_LAT + (s - NCHUNK_CTX)
    return jnp.where(s < NCHUNK_CTX, ctx, lat)


def _bwd_chunk(b, s):
    ctx = T_LAT // CHUNK + b * NCHUNK_CTX + (NCHUNK_CTX - 1 - s)
    lat = b * NCHUNK_LAT + (NSTEP - 1 - s)
    return jnp.where(s < NCHUNK_CTX, ctx, lat)


def _ssd_kernel(xs_f, bm_f, cm_f, dt_f, xs_b, bm_b, cm_b, dt_b,
                bias_row, alog_row, bias_col, alog_col, yf_ref, yb_ref, h_scr):
    s = pl.program_id(1)

    @pl.when(s == 0)
    def _():
        h_scr[...] = jnp.zeros_like(h_scr)

    ii = lax.broadcasted_iota(jnp.int32, (CHUNK, CHUNK), 0)
    jj = lax.broadcasted_iota(jnp.int32, (CHUNK, CHUNK), 1)
    lane_lo = jj < SSD_P
    lower = jj <= ii
    upper = jj >= ii
    hg = SSD_HEADS // SSD_GROUPS
    dirs = ((xs_f, bm_f, cm_f, dt_f, yf_ref), (xs_b, bm_b, cm_b, dt_b, yb_ref))
    causal = (lower, upper)
    grp = lambda g: slice(g * SSD_STATE, (g + 1) * SSD_STATE)
    pair_lanes = lambda pair: slice(pair * 2 * SSD_P, (pair + 1) * 2 * SSD_P)

    acs, acs_t, row_t, dte_t, tot = {}, {}, {}, {}, {}
    for d in range(2):
        dt_ref = dirs[d][3]
        tri = jnp.where(causal[d], 1.0, 0.0).astype(BF16)
        tri_t = jnp.where(causal[1 - d], 1.0, 0.0).astype(BF16)
        last = CHUNK - 1 if d == 0 else 0
        r0 = d * SSD_HEADS
        dt_raw = dt_ref[...]
        dt_c = _softplus(dt_raw + bias_row[...])
        adt_c = dt_c * (-LOG2E * jnp.exp(alog_row[...]))
        acs[d] = sum(_dot(tri, p) for p in _split3(adt_c))
        dt_t = _softplus(dt_raw.T[r0:r0 + SSD_HEADS, :] + bias_col[r0:r0 + SSD_HEADS, :])
        adt_t = dt_t * (-LOG2E * jnp.exp(alog_col[r0:r0 + SSD_HEADS, :]))
        acs_t[d] = sum(_dot(p, tri_t) for p in _split3(adt_t))
        row_t[d] = acs_t[d] - jnp.log2(dt_t)
        dte_t[d] = jnp.exp2(acs_t[d][:, last:last + 1] - acs_t[d]) * dt_t
        tot[d] = acs[d][last:last + 1, :]

    cb, bm_t, cm = {}, {}, {}
    for d in range(2):
        _, bm_ref, cm_ref, _, _ = dirs[d]
        for g in range(SSD_GROUPS):
            cm[d, g] = cm_ref[:, grp(g)]
            cb[d, g] = _dot_nt(cm[d, g], bm_ref[:, grp(g)]).astype(BF16)
            bm_t[d, g] = bm_ref[:, grp(g)].astype(F32).T

    for d in range(2):
        xs_ref, y_ref = dirs[d][0], dirs[d][4]
        r0 = d * SSD_HEADS
        y_pairs, h_pairs = [], []
        for pair in range(SSD_HEADS // 2):
            g = pair // (hg // 2)
            x_pair = xs_ref[:, pair_lanes(pair)]
            h_pair = h_scr[d, :, pair_lanes(pair)]
            rhs = jnp.concatenate([x_pair, h_pair.astype(BF16)], axis=0)
            ys, ups, cds = [], [], []
            for h in (2 * pair, 2 * pair + 1):
                col = jnp.broadcast_to(acs[d][:, r0 + h:r0 + h + 1], (CHUNK, CHUNK))
                dec = jnp.exp2(jnp.where(causal[d], col - row_t[d][h:h + 1, :], NEG_INF))
                m_intra = cb[d, g] * dec.astype(BF16)
                m_state = cm[d, g] * jnp.exp2(col).astype(BF16)
                ys.append(_dot(jnp.concatenate([m_intra, m_state], axis=1), rhs))
                ups.append(_dot((bm_t[d, g] * dte_t[d][h:h + 1, :]).astype(BF16), x_pair))
                cds.append(jnp.exp2(tot[d][:, r0 + h:r0 + h + 1]))
            y_pairs.append(jnp.where(lane_lo, ys[0], ys[1]))
            h_pairs.append(jnp.where(lane_lo, cds[0], cds[1]) * h_pair
                           + jnp.where(lane_lo, ups[0], ups[1]))
        y_ref[...] = jnp.concatenate(y_pairs, axis=1).astype(BF16)
        h_scr[d] = jnp.concatenate(h_pairs, axis=1)


def _ssd_call(xbc, dt_all, bias_row, alog_row, bias_col, alog_col):
    bcol = SSD_INNER // (SSD_GROUPS * SSD_STATE)

    def specs(chunk_fn):
        return [
            pl.BlockSpec((CHUNK, SSD_INNER), lambda b, s: (chunk_fn(b, s), 0)),
            pl.BlockSpec((CHUNK, SSD_GROUPS * SSD_STATE), lambda b, s: (chunk_fn(b, s), bcol)),
            pl.BlockSpec((CHUNK, SSD_GROUPS * SSD_STATE), lambda b, s: (chunk_fn(b, s), bcol + 1)),
            pl.BlockSpec((CHUNK, DT_PAD), lambda b, s: (chunk_fn(b, s), 0)),
        ]

    const = lambda shape: pl.BlockSpec(shape, lambda b, s: (0, 0))
    return pl.pallas_call(
        _ssd_kernel,
        grid=(BATCH, NSTEP),
        in_specs=specs(_fwd_chunk) + specs(_bwd_chunk) + [
            const((1, DT_PAD)), const((1, DT_PAD)), const((DT_PAD, 1)), const((DT_PAD, 1))],
        out_specs=[
            pl.BlockSpec((CHUNK, SSD_INNER), lambda b, s: (_fwd_chunk(b, s), 0)),
            pl.BlockSpec((CHUNK, SSD_INNER), lambda b, s: (_bwd_chunk(b, s), 0)),
        ],
        out_shape=[jax.ShapeDtypeStruct((T_ALL, SSD_INNER), BF16)] * 2,
        scratch_shapes=[pltpu.VMEM((2, SSD_STATE, SSD_INNER), F32)],
        compiler_params=_cparams(("arbitrary", "arbitrary")),
        name="ssd",
    )(xbc, xbc, xbc, dt_all, xbc, xbc, xbc, dt_all, bias_row, alog_row, bias_col, alog_col)


def _ret_kernel(lg_ref, q_f, k_f, v_f, q_b, k_b, v_b, of_ref, ob_ref, s_scr, tab_scr):
    s = pl.program_id(1)

    @pl.when(s == 0)
    def _():
        s_scr[...] = jnp.zeros_like(s_scr)
        ii = lax.broadcasted_iota(jnp.int32, (CHUNK, CHUNK), 0)
        jj = lax.broadcasted_iota(jnp.int32, (CHUNK, CHUNK), 1)
        for d in range(2):
            if d == 0:
                dist = (ii - jj).astype(F32)
                row_pow = (ii + 1).astype(F32)
                key_pow = (CHUNK - 1 - ii).astype(F32)
            else:
                dist = (jj - ii).astype(F32)
                row_pow = (CHUNK - ii).astype(F32)
                key_pow = ii.astype(F32)
            for h in range(RET_HEADS):
                lg = lg_ref[d, h]
                tab_scr[d, h, 0] = jnp.where(dist >= 0.0, jnp.exp(jnp.maximum(dist, 0.0) * lg), 0.0)
                tab_scr[d, h, 1] = jnp.exp(row_pow * lg)
                tab_scr[d, h, 2] = jnp.exp(key_pow * lg)

    dirs = ((q_f, k_f, v_f, of_ref), (q_b, k_b, v_b, ob_ref))
    heads = [(d, h) for d in range(2) for h in range(RET_HEADS)]
    lanes = lambda h: slice(h * HEAD_DIM, (h + 1) * HEAD_DIM)
    raw = {}
    for d, h in heads:
        q_ref, k_ref, v_ref, _ = dirs[d]
        raw[d, h] = _dot_nt(q_ref[:, lanes(h)], k_ref[:, lanes(h)])
    kv = {}
    for d, h in heads:
        _, k_ref, v_ref, _ = dirs[d]
        k_dec = (k_ref[:, lanes(h)].astype(F32) * tab_scr[d, h, 2]).astype(BF16)
        kv[d, h] = _dot_tn(k_dec, v_ref[:, lanes(h)])
    outs = {}
    for d, h in heads:
        q_ref, _, v_ref, _ = dirs[d]
        scores = raw[d, h] * tab_scr[d, h, 0]
        q_cross = q_ref[:, lanes(h)].astype(F32) * tab_scr[d, h, 1]
        state = s_scr[d, h * HEAD_DIM:(h + 1) * HEAD_DIM, :]
        lhs = jnp.concatenate([scores.astype(BF16), q_cross.astype(BF16)], axis=1)
        rhs = jnp.concatenate([v_ref[:, lanes(h)], state.astype(BF16)], axis=0)
        outs[d, h] = _dot(lhs, rhs)
    for d in range(2):
        states = []
        for h in range(RET_HEADS):
            chunk_decay = jnp.exp(jnp.full((1, HEAD_DIM), float(CHUNK), F32) * lg_ref[d, h])
            states.append(chunk_decay * s_scr[d, h * HEAD_DIM:(h + 1) * HEAD_DIM, :] + kv[d, h])
        dirs[d][3][...] = jnp.concatenate([outs[d, h] for h in range(RET_HEADS)], axis=1).astype(BF16)
        s_scr[d] = jnp.concatenate(states, axis=0)


def _ret_call(p_all, log_decay):
    def specs(chunk_fn):
        return [
            pl.BlockSpec((CHUNK, RET_WIDTH), lambda b, s: (chunk_fn(b, s), COL_RQ // RET_WIDTH)),
            pl.BlockSpec((CHUNK, RET_WIDTH), lambda b, s: (chunk_fn(b, s), COL_RK // RET_WIDTH)),
            pl.BlockSpec((CHUNK, RET_WIDTH), lambda b, s: (chunk_fn(b, s), COL_RV // RET_WIDTH)),
        ]

    return pl.pallas_call(
        _ret_kernel,
        grid=(BATCH, NSTEP),
        in_specs=[pl.BlockSpec(memory_space=pltpu.SMEM)] + specs(_fwd_chunk) + specs(_bwd_chunk),
        out_specs=[
            pl.BlockSpec((CHUNK, RET_WIDTH), lambda b, s: (_fwd_chunk(b, s), 0)),
            pl.BlockSpec((CHUNK, RET_WIDTH), lambda b, s: (_bwd_chunk(b, s), 0)),
        ],
        out_shape=[jax.ShapeDtypeStruct((T_ALL, RET_WIDTH), BF16)] * 2,
        scratch_shapes=[pltpu.VMEM((2, RET_HEADS * HEAD_DIM, HEAD_DIM), F32),
                        pltpu.VMEM((2, RET_HEADS, 3, CHUNK, CHUNK), F32)],
        compiler_params=_cparams(("arbitrary", "arbitrary")),
        name="retention",
    )(log_decay, p_all, p_all, p_all, p_all, p_all, p_all)


def _merge_kernel(attn_ref, yf_ref, yb_ref, xs_ref, z0_ref, z1_ref, of_ref, ob_ref, rg_ref,
                  ga_ref, gs_ref, gr_ref, dskip_ref, ssd_g_ref, ret_g_ref,
                  wa_ref, ws_ref, wr_ref, o_ref):
    z = jnp.concatenate([z0_ref[...], z1_ref[...]], axis=1).astype(F32)
    y = (yf_ref[...].astype(F32) + yb_ref[...].astype(F32)
         + dskip_ref[...] * xs_ref[...].astype(F32))
    y = y * _silu(z)
    ssd_o = y * lax.rsqrt(jnp.mean(y * y, axis=-1, keepdims=True) + LN_EPS) * ssd_g_ref[...]

    o = of_ref[...].astype(F32) + ob_ref[...].astype(F32)
    normed = jnp.concatenate(
        [_ln_rows(o[:, h * HEAD_DIM:(h + 1) * HEAD_DIM]) for h in range(RET_HEADS)], axis=1)
    ret_o = normed * ret_g_ref[...] * _silu(rg_ref[...].astype(F32))

    merged = (_sigmoid(ga_ref[...].astype(F32)) * _dot(attn_ref[...], wa_ref[...])
              + _sigmoid(gs_ref[...].astype(F32)) * _dot(ssd_o.astype(BF16), ws_ref[...])
              + _sigmoid(gr_ref[...].astype(F32)) * _dot(ret_o.astype(BF16), wr_ref[...]))
    o_ref[...] = merged.astype(BF16)


def _merge_call(l, n_rows, attn, yf, yb, xbc, p_all, of, ob, dskip, ssd_g, ret_g, wa, ws, wr):
    tm = TM_MERGE
    zw = SSD_INNER // 2
    w1024 = lambda c: pl.BlockSpec((tm, 1024), lambda i: (i, c))
    zspec = lambda c: pl.BlockSpec((tm, zw), lambda i: (i, COL_Z // zw + c))
    gate = lambda c: pl.BlockSpec((tm, D_MODEL), lambda i: (i, COL_GATES // D_MODEL + c))
    vec = pl.BlockSpec((1, 1024), lambda i: (0, 0))
    wspec = pl.BlockSpec((None, 1024, D_MODEL), lambda i: (l, 0, 0), pipeline_mode=pl.Buffered(1))
    return pl.pallas_call(
        _merge_kernel,
        grid=(n_rows // tm,),
        in_specs=[
            w1024(0), w1024(0), w1024(0), w1024(0), zspec(0), zspec(1),
            w1024(0), w1024(0), w1024(COL_RG // 1024),
            gate(0), gate(1), gate(2), vec, vec, vec, wspec, wspec, wspec,
        ],
        out_specs=pl.BlockSpec((tm, D_MODEL), lambda i: (i, 0)),
        out_shape=jax.ShapeDtypeStruct((n_rows, D_MODEL), BF16),
        compiler_params=_cparams(("arbitrary",)),
        name="merge",
    )(attn, yf, yb, xbc, p_all, p_all, of, ob, p_all, p_all, p_all, p_all, dskip, ssd_g, ret_g,
      wa, ws, wr)


def _deepnorm(x, y, gate, g, b):
    return _ln_rows(DEEPNORM_ALPHA * x + gate * y) * g + b


def _outproj_kernel(m_ref, x_ref, w_ref, gate_ref, g_ref, b_ref, *rest):
    o_ref = rest[-1]
    for r in range(0, TM_OUT, RC_OUT):
        rows = slice(r, r + RC_OUT)
        mix = _dot(m_ref[rows, :], w_ref[...])
        o_ref[rows, :] = _deepnorm(x_ref[rows, :], mix, gate_ref[...], g_ref[...], b_ref[...])


def _outproj_call(l, n_rows, n_tiles, merged, x_src, tile0, w_out, mod, ln_g, ln_b, prev=None):
    tm = TM_OUT
    vec = pl.BlockSpec((1, D_MODEL), lambda i: (0, 0))
    in_specs = [
        pl.BlockSpec((tm, D_MODEL), lambda i: (i + tile0, 0)),
        pl.BlockSpec((tm, D_MODEL), lambda i: (i, 0)),
        pl.BlockSpec((None, D_MODEL, D_MODEL), lambda i: (l, 0, 0), pipeline_mode=pl.Buffered(1)),
        pl.BlockSpec((None, 1, D_MODEL), lambda i: (_batch_of_tile(i + tile0, tm) * 6 + 2, 0, 0)),
        vec, vec,
    ]
    args = [merged, x_src, w_out, mod, ln_g, ln_b]
    aliases = {}
    if prev is not None:
        aliases = {len(args): 0}
        in_specs.append(pl.BlockSpec(memory_space=pl.ANY))
        args.append(prev)
    return pl.pallas_call(
        _outproj_kernel,
        grid=(n_tiles,),
        in_specs=in_specs,
        out_specs=pl.BlockSpec((tm, D_MODEL), lambda i: (i + tile0, 0)),
        out_shape=jax.ShapeDtypeStruct((n_rows, D_MODEL), F32),
        input_output_aliases=aliases,
        compiler_params=_cparams(("arbitrary",)),
        name="outproj",
    )(*args)


def _mlp_kernel(x_ref, shift_ref, scale_ref, gate_ref, wup_ref, wdn_ref, g_ref, b_ref,
                o_ref, h_scr, acc_scr):
    j = pl.program_id(1)
    last = pl.num_programs(1) - 1
    chunks = [slice(r, r + RC_MLP) for r in range(0, TM_MLP, RC_MLP)]

    def up_down(rows):
        u = jnp.maximum(_dot(h_scr[rows, :], wup_ref[...]), 0.0)
        return _dot((u * u).astype(BF16), wdn_ref[...])

    @pl.when(j == 0)
    def _():
        for rows in chunks:
            h = _ln_rows(x_ref[rows, :]) * (1.0 + scale_ref[...]) + shift_ref[...]
            h_scr[rows, :] = h.astype(BF16)
        for rows in chunks:
            acc_scr[rows, :] = up_down(rows)

    @pl.when((j > 0) & (j < last))
    def _():
        acc_scr[...] += up_down(slice(None))

    @pl.when(j == last)
    def _():
        for rows in chunks:
            y = acc_scr[rows, :] + up_down(rows)
            o_ref[rows, :] = _deepnorm(x_ref[rows, :], y, gate_ref[...], g_ref[...], b_ref[...])


def _mlp_call(l, n_rows, x1, mod, w_up, w_down, ln_g, ln_b):
    tm, tf = TM_MLP, TF_MLP

    def mod_map(k):
        return lambda i, j: (_batch_of_tile(i, tm) * 6 + k, 0, 0)

    vec = pl.BlockSpec((1, D_MODEL), lambda i, j: (0, 0))
    return pl.pallas_call(
        _mlp_kernel,
        grid=(n_rows // tm, D_FF // tf),
        in_specs=[
            pl.BlockSpec((tm, D_MODEL), lambda i, j: (i, 0)),
            pl.BlockSpec((None, 1, D_MODEL), mod_map(3)),
            pl.BlockSpec((None, 1, D_MODEL), mod_map(4)),
            pl.BlockSpec((None, 1, D_MODEL), mod_map(5)),
            pl.BlockSpec((None, D_MODEL, tf), lambda i, j: (l, 0, j)),
            pl.BlockSpec((None, tf, D_MODEL), lambda i, j: (l, j, 0)),
            vec, vec,
        ],
        out_specs=pl.BlockSpec((tm, D_MODEL), lambda i, j: (i, 0)),
        out_shape=jax.ShapeDtypeStruct((n_rows, D_MODEL), F32),
        scratch_shapes=[pltpu.VMEM((tm, D_MODEL), BF16), pltpu.VMEM((tm, D_MODEL), F32)],
        compiler_params=_cparams(("arbitrary", "arbitrary")),
        name="mlp",
    )(x1, mod, mod, mod, w_up, w_down, ln_g, ln_b)


def _rope_tables():
    f32 = np.float32
    rows = SEQ // GRID_W
    row = np.repeat(np.arange(rows), GRID_W).astype(f32)
    col = (np.arange(rows * GRID_W) % GRID_W).astype(f32)
    n_freq = HEAD_DIM // 4
    inv = (f32(ROPE_BASE) ** (-np.arange(n_freq, dtype=f32) / f32(n_freq))).astype(f32)
    ang = np.concatenate([row[:, None] * inv, col[:, None] * inv], axis=-1).astype(f32)
    cos, sin = np.cos(ang).astype(f32), np.sin(ang).astype(f32)
    cos2 = np.concatenate([cos, cos], axis=-1)
    sin2 = np.concatenate([-sin, sin], axis=-1)
    cos2 = np.concatenate([cos2, np.ones((TM_IN, HEAD_DIM), f32)], axis=0)
    sin2 = np.concatenate([sin2, np.zeros((TM_IN, HEAD_DIM), f32)], axis=0)
    return jnp.asarray(cos2), jnp.asarray(sin2)


def _pad_heads(v):
    flat = jnp.pad(v.reshape(-1).astype(F32), (0, DT_PAD - DT_COLS))
    return flat.reshape(1, DT_PAD), flat.reshape(DT_PAD, 1)


def kernel(x, c, ctx, c_ctx, ada_w, ada_b, w_in, attn_sink, ssd_conv_w, ssd_conv_b, ssd_a_log,
           ssd_dt_bias, ssd_d, ssd_norm_g, ret_log_decay, ret_norm_g, w_branch_attn, w_branch_ssd,
           w_branch_ret, w_out, ln1_g, ln1_b, w_mlp_up, w_mlp_down, ln2_g, ln2_b):
    lat_tiles_in, lat_tiles_out = T_LAT // TM_IN, T_LAT // TM_OUT
    sources = [(x.reshape(T_LAT, D_MODEL), 0, 0), (ctx.reshape(T_CTX, D_MODEL), lat_tiles_in, lat_tiles_out)]
    cond = jnp.concatenate([c, c_ctx[None, :], jnp.zeros((8 - BATCH - 1, D_MODEL), F32)], axis=0)
    mod_all = _ada_call(cond, ada_w, ada_b).reshape(DEPTH, 8 * 6, 1, D_MODEL)
    cos2, sin2 = _rope_tables()
    wb_attn, wb_ssd, wb_ret = (_cast_stacked(w) for w in (w_branch_attn, w_branch_ssd, w_branch_ret))
    w_out_b, w_up_b, w_down_b = _cast_stacked(w_out), _cast_stacked(w_mlp_up), _cast_stacked(w_mlp_down)
    w_in_t, w_dt = _cast_w_in(w_in)

    for l in range(DEPTH):
        update_ctx = l < DEPTH - 1
        n_rows = T_ALL if update_ctx else T_LAT
        mod = mod_all[l]
        proj = None
        for x_src, tile_in, _ in sources:
            proj = _inproj_call(l, x_src, tile_in, mod, w_in_t, w_dt, cos2, sin2, not update_ctx, proj)
        p_all, dt_all = proj
        xbc = _conv_call(p_all, ssd_conv_w[l], ssd_conv_b[l])

        attn = _attn_calls(p_all, attn_sink[l].astype(F32), update_ctx)

        bias_row, bias_col = _pad_heads(ssd_dt_bias[l])
        alog_row, alog_col = _pad_heads(ssd_a_log[l])
        yf, yb = _ssd_call(xbc, dt_all, bias_row, alog_row, bias_col, alog_col)
        of, ob = _ret_call(p_all, ret_log_decay[l].astype(F32))

        dskip = jnp.repeat(ssd_d[l].astype(F32), SSD_P).reshape(1, SSD_INNER)
        merged = _merge_call(
            l, n_rows, attn, yf, yb, xbc, p_all, of, ob, dskip,
            ssd_norm_g[l].reshape(1, SSD_INNER), ret_norm_g[l].reshape(1, RET_WIDTH), wb_attn, wb_ssd, wb_ret)
        x1 = None
        for x_src, _, tile_out in sources:
            n_tiles = min(x_src.shape[0], n_rows - tile_out * TM_OUT) // TM_OUT
            x1 = _outproj_call(l, n_rows, n_tiles, merged, x_src, tile_out, w_out_b, mod,
                               ln1_g[l].reshape(1, D_MODEL), ln1_b[l].reshape(1, D_MODEL), x1)
        x_all = _mlp_call(l, n_rows, x1, mod, w_up_b, w_down_b,
                          ln2_g[l].reshape(1, D_MODEL), ln2_b[l].reshape(1, D_MODEL))
        sources = [(x_all, 0, 0)]
    return x_all.reshape(BATCH, SEQ, D_MODEL)
```
